```python
import math
import jax, jax.numpy as jnp
from jax import lax
import numpy as np

D_MODEL = 1024
BATCH = 32
SEQ = 256
DEPTH = 2
DEC_BATCH = 8
DEC_SEQ = 4096
PAST_LEN = 512

GRID_W = 64
N_EVEN = (DEPTH + 1) // 2
N_ODD = DEPTH // 2
EPS = 1e-6

RET_HEADS = 4
RET_DK = 128
RET_DV = 128
RET_WIDTH = RET_HEADS * RET_DV
RET_CHUNK = 128
RET_BWD_OFFSET = 0.5
CONV_WIDTH = D_MODEL - RET_WIDTH
CONV_K = 3
EVEN_IN = 4 * RET_WIDTH + 4 * CONV_WIDTH
EVEN_MIX = RET_WIDTH + CONV_WIDTH
MLA_HEADS = 8
QK_NOPE = 128
QK_ROPE = 64
V_HEAD = 128
Q_LORA = 384
KV_LORA = 256
MLA_WIDTH = MLA_HEADS * V_HEAD
ODD_IN = Q_LORA + KV_LORA + QK_ROPE + MLA_WIDTH
ROPE_BASE = 10000.0
Q_BLOCK = 128

kernel_name = "hybrid_retconv_mla_diffusion_step"

F32 = jnp.float32


def rms(x, g):
    xf = x.astype(F32)
    y = xf * lax.rsqrt(jnp.mean(xf * xf, axis=-1, keepdims=True) + EPS)
    return (y * g.astype(F32)).astype(x.dtype)


def ada(cvec, w, b):
    m = jax.nn.silu(cvec) @ w + b
    sh, sc, g = jnp.split(m, 3, axis=-1)
    return sh[:, None, :], sc[:, None, :], g[:, None, :]


def ret_log_decay(offset):
    h = jnp.arange(RET_HEADS, dtype=F32)
    return jnp.log(1.0 - 2.0 ** (-5.0 - h - offset))


def retention_scan(q, k, v, log_g, s0):
    B, T, H, _ = q.shape
    C = RET_CHUNK
    N = T // C
    idx = jnp.arange(C, dtype=F32)
    diff = idx[:, None] - idx[None, :]
    mask = jnp.where(diff >= 0, jnp.exp(log_g[:, None, None] * jnp.maximum(diff, 0.0)), 0.0)
    q_dec = jnp.exp(log_g[None, :] * (idx[:, None] + 1.0))
    k_dec = jnp.exp(log_g[None, :] * (C - 1.0 - idx[:, None]))
    c_dec = jnp.exp(log_g * C)

    def chunks(a):
        return a.reshape(B, N, C, H, a.shape[-1]).transpose(1, 0, 2, 3, 4)

    def step(s, qkv):
        qc, kc, vc = qkv
        att = jnp.einsum('bihd,bjhd->bhij', qc, kc) * mask
        o = (jnp.einsum('bhij,bjhe->bihe', att, vc)
             + jnp.einsum('bihd,bhde->bihe', qc, s) * q_dec[None, :, :, None])
        s = s * c_dec[None, :, None, None] + jnp.einsum('bjhd,bjhe->bhde', kc * k_dec[None, :, :, None], vc)
        return s, o

    s_fin, o = lax.scan(step, s0, (chunks(q), chunks(k), chunks(v)))
    return o.transpose(1, 0, 2, 3, 4).reshape(B, T, H, v.shape[-1]), s_fin


def ret_conv_mixer(h, w_in, conv_w, w_out, s_fwd0, s_bwd0):
    B, T, _ = h.shape
    R, Cw = RET_WIDTH, CONV_WIDTH
    p = h @ w_in
    q, k, v, g_a, bg, cg, xh, g_b = jnp.split(
        p, [R, 2 * R, 3 * R, 4 * R, 4 * R + Cw, 4 * R + 2 * Cw, 4 * R + 3 * Cw], axis=-1)
    q = q.reshape(B, T, RET_HEADS, RET_DK).astype(F32)
    k = k.reshape(B, T, RET_HEADS, RET_DK).astype(F32) * (RET_DK ** -0.5)
    v = v.reshape(B, T, RET_HEADS, RET_DV).astype(F32)
    o_f, s_f = retention_scan(q, k, v, ret_log_decay(0.0), s_fwd0)
    o_b, s_b = retention_scan(q[:, ::-1], k[:, ::-1], v[:, ::-1], ret_log_decay(RET_BWD_OFFSET), s_bwd0)
    o = o_f + o_b[:, ::-1]
    o = o * lax.rsqrt(jnp.mean(o * o, axis=-1, keepdims=True) + EPS)
    y_a = o.reshape(B, T, R).astype(h.dtype) * jax.nn.silu(g_a)
    z = cg * xh
    zp = jnp.pad(z, ((0, 0), (1, 1), (0, 0)))
    zc = zp[:, :-2] * conv_w[0] + zp[:, 1:-1] * conv_w[1] + zp[:, 2:] * conv_w[2]
    y_b = jax.nn.silu(g_b) * bg * zc
    return jnp.concatenate([y_a, y_b], axis=-1) @ w_out, s_f, s_b


def _rot(x, ang):
    f = ang.shape[-1]
    cos = jnp.cos(ang)[:, None, :]
    sin = jnp.sin(ang)[:, None, :]
    x1, x2 = x[..., :f], x[..., f:]
    return jnp.concatenate([x1 * cos - x2 * sin, x1 * sin + x2 * cos], axis=-1)


def rope_2d(x):
    T = x.shape[1]
    rows = T // GRID_W
    row = jnp.repeat(jnp.arange(rows, dtype=F32), GRID_W)
    col = jnp.tile(jnp.arange(GRID_W, dtype=F32), rows)
    f = QK_ROPE // 4
    inv = ROPE_BASE ** (-jnp.arange(f, dtype=F32) / f)
    xf = x.astype(F32)
    half = QK_ROPE // 2
    out = jnp.concatenate([_rot(xf[..., :half], row[:, None] * inv),
                           _rot(xf[..., half:], col[:, None] * inv)], axis=-1)
    return out.astype(x.dtype)


def mla_project(h, w_in, qn_g, kvn_g, q_up):
    B, T, _ = h.shape
    q_lat, kv_lat, k_rope, gate = jnp.split(h @ w_in, [Q_LORA, Q_LORA + KV_LORA, Q_LORA + KV_LORA + QK_ROPE], axis=-1)
    q = (rms(q_lat, qn_g) @ q_up).reshape(B, T, MLA_HEADS, QK_NOPE + QK_ROPE)
    ckv = rms(kv_lat, kvn_g)
    return q[..., :QK_NOPE], q[..., QK_NOPE:], ckv, k_rope, gate


def mla_expand(ckv, kv_up):
    B, L, _ = ckv.shape
    kv = (ckv @ kv_up).reshape(B, L, MLA_HEADS, QK_NOPE + V_HEAD)
    return kv[..., :QK_NOPE], kv[..., QK_NOPE:]


def mla_attend(q_nope, q_rope, k_nope, k_rope, v):
    B, T, H, _ = q_nope.shape
    NB = T // Q_BLOCK
    scale = (QK_NOPE + QK_ROPE) ** -0.5

    def blocks(a):
        return a.reshape(B, NB, Q_BLOCK, H, a.shape[-1]).transpose(1, 0, 2, 3, 4)

    def one(qs):
        qn, qr = qs
        s = jnp.einsum('bqhd,bkhd->bhqk', qn, k_nope) + jnp.einsum('bqhr,bkr->bhqk', qr, k_rope)
        pr = jax.nn.softmax(s.astype(F32) * scale, axis=-1).astype(v.dtype)
        return jnp.einsum('bhqk,bkhe->bqhe', pr, v)

    o = lax.map(one, (blocks(q_nope), blocks(q_rope)))
    return o.transpose(1, 0, 2, 3, 4).reshape(B, T, H * V_HEAD)


def mla_context(h, w_in, qn_g, kvn_g, q_up, kv_up, w_out):
    q_nope, q_rope, ckv, k_rope, gate = mla_project(h, w_in, qn_g, kvn_g, q_up)
    k_nope, v = mla_expand(ckv, kv_up)
    o = mla_attend(q_nope, q_rope, k_nope, k_rope, v)
    return (o * jax.nn.silu(gate)) @ w_out, ckv, k_rope


def mla_latent(h, w_in, qn_g, kvn_g, q_up, kv_up, w_out, ctx_ckv, ctx_krope):
    q_nope, q_rope, ckv, k_rope, gate = mla_project(h, w_in, qn_g, kvn_g, q_up)
    q_rope = rope_2d(q_rope)
    k_rope = rope_2d(k_rope[:, :, None, :])[:, :, 0, :]
    k_nope_l, v_l = mla_expand(ckv, kv_up)
    k_nope_c, v_c = mla_expand(ctx_ckv, kv_up)
    k_nope = jnp.concatenate([k_nope_c, k_nope_l], axis=1)
    k_r = jnp.concatenate([ctx_krope, k_rope], axis=1)
    v = jnp.concatenate([v_c, v_l], axis=1)
    o = mla_attend(q_nope, q_rope, k_nope, k_r, v)
    return (o * jax.nn.silu(gate)) @ w_out


def setup_inputs(seed: int = 0) -> dict:
    key = jax.random.key(seed)
    ks = jax.random.split(key, 24)

    def nrm(k, shape, scale):
        return jax.random.normal(k, shape, F32) * scale

    return {
        "x_prompt": nrm(ks[0], (BATCH, SEQ, D_MODEL), 1.0),
        "x_sample": nrm(ks[1], (DEC_BATCH, DEC_SEQ, D_MODEL), 1.0),
        "c": nrm(ks[2], (DEC_BATCH, D_MODEL), 1.0),
        "state_ret_fwd": nrm(ks[3], (DEC_BATCH, N_EVEN, RET_HEADS, RET_DK, RET_DV), 0.5),
        "state_ret_bwd": nrm(ks[4], (DEC_BATCH, N_EVEN, RET_HEADS, RET_DK, RET_DV), 0.5),
        "cache_mla_ckv": nrm(ks[5], (DEC_BATCH, N_ODD, PAST_LEN, KV_LORA), 1.0),
        "cache_mla_krope": nrm(ks[6], (DEC_BATCH, N_ODD, PAST_LEN, QK_ROPE), 1.0),
        "c_ctx": nrm(ks[7], (D_MODEL,), 1.0),
        "ada_w": nrm(ks[8], (DEPTH, D_MODEL, 3 * D_MODEL), 0.5 * D_MODEL ** -0.5),
        "ada_b": nrm(ks[9], (DEPTH, 3 * D_MODEL), 0.02),
        "norm_g": 1.0 + nrm(ks[10], (DEPTH, D_MODEL), 0.02),
        "even_in_w": nrm(ks[11], (N_EVEN, D_MODEL, EVEN_IN), D_MODEL ** -0.5),
        "even_conv_w": nrm(ks[12], (N_EVEN, CONV_K, CONV_WIDTH), CONV_K ** -0.5),
        "even_out_w": nrm(ks[13], (N_EVEN, EVEN_MIX, D_MODEL), EVEN_MIX ** -0.5),
        "odd_in_w": nrm(ks[14], (N_ODD, D_MODEL, ODD_IN), D_MODEL ** -0.5),
        "odd_q_norm_g": 1.0 + nrm(ks[15], (N_ODD, Q_LORA), 0.02),
        "odd_kv_norm_g": 1.0 + nrm(ks[16], (N_ODD, KV_LORA), 0.02),
        "odd_q_up_w": nrm(ks[17], (N_ODD, Q_LORA, MLA_HEADS * (QK_NOPE + QK_ROPE)), Q_LORA ** -0.5),
        "odd_kv_up_w": nrm(ks[18], (N_ODD, KV_LORA, MLA_HEADS * (QK_NOPE + V_HEAD)), KV_LORA ** -0.5),
        "odd_out_w": nrm(ks[19], (N_ODD, MLA_WIDTH, D_MODEL), MLA_WIDTH ** -0.5),
        "final_norm_g": 1.0 + nrm(ks[20], (D_MODEL,), 0.02),
    }


def reference(x_prompt, x_sample, c, state_ret_fwd, state_ret_bwd, cache_mla_ckv, cache_mla_krope, c_ctx,
              ada_w, ada_b, norm_g, even_in_w, even_conv_w, even_out_w, odd_in_w, odd_q_norm_g,
              odd_kv_norm_g, odd_q_up_w, odd_kv_up_w, odd_out_w, final_norm_g):
    xp, xs = x_prompt, x_sample
    B = xp.shape[0]
    new_sf, new_sb, new_ckv, new_kr = [], [], [], []
    for l in range(DEPTH):
        sh_p, sc_p, g_p = ada(c_ctx[None, :], ada_w[l], ada_b[l])
        sh_s, sc_s, g_s = ada(c, ada_w[l], ada_b[l])
        hp = rms(xp, norm_g[l]) * (1.0 + sc_p) + sh_p
        hs = rms(xs, norm_g[l]) * (1.0 + sc_s) + sh_s
        i = l // 2
        if l % 2 == 0:
            zero = jnp.zeros((B, RET_HEADS, RET_DK, RET_DV), F32)
            yp, sf, sb = ret_conv_mixer(hp, even_in_w[i], even_conv_w[i], even_out_w[i], zero, zero)
            ys, _, _ = ret_conv_mixer(hs, even_in_w[i], even_conv_w[i], even_out_w[i],
                                      state_ret_fwd[:, i].astype(F32), state_ret_bwd[:, i].astype(F32))
            new_sf.append(sf.astype(xp.dtype))
            new_sb.append(sb.astype(xp.dtype))
        else:
            yp, ckv, kr = mla_context(hp, odd_in_w[i], odd_q_norm_g[i], odd_kv_norm_g[i],
                                      odd_q_up_w[i], odd_kv_up_w[i], odd_out_w[i])
            ys = mla_latent(hs, odd_in_w[i], odd_q_norm_g[i], odd_kv_norm_g[i], odd_q_up_w[i],
                            odd_kv_up_w[i], odd_out_w[i], cache_mla_ckv[:, i], cache_mla_krope[:, i])
            new_ckv.append(ckv)
            new_kr.append(kr)
        xp = xp + g_p * yp
        xs = xs + g_s * ys
    y_prompt = rms(xp, final_norm_g)
    y_sample = rms(xs, final_norm_g)
    return (y_prompt, y_sample, jnp.stack(new_sf, axis=1), jnp.stack(new_sb, axis=1),
            jnp.stack(new_ckv, axis=1), jnp.stack(new_kr, axis=1))
```

```python
import functools

import jax
import jax.numpy as jnp
from jax import lax
from jax.experimental import pallas as pl
from jax.experimental.pallas import tpu as pltpu

F32 = jnp.float32
BF16 = jnp.bfloat16

D_MODEL = 1024
GRID_W = 64
EPS = 1e-6
RET_HEADS = 4
RET_DK = 128
RET_DV = 128
RET_WIDTH = RET_HEADS * RET_DV
RET_CHUNK = 128
RET_BWD_OFFSET = 0.5
CONV_WIDTH = D_MODEL - RET_WIDTH
MLA_HEADS = 8
QK_NOPE = 128
QK_ROPE = 64
V_HEAD = 128
Q_LORA = 384
KV_LORA = 256
ROPE_BASE = 10000.0
QK_PAD = 256
ADA_ROWS = 16
BF16_SUBLANES = 16
VMEM_LIMIT = 56 * 1024 * 1024


def _silu(x):
    return x * (1.0 / (1.0 + jnp.exp(-x)))


def _rms_rows(x, g):
    return x * lax.rsqrt(jnp.mean(x * x, axis=-1, keepdims=True) + EPS) * g


def _dot(a, b):
    return jnp.dot(a, b, preferred_element_type=F32)


def _dot_nt(a, b):
    return lax.dot_general(a, b, (((1,), (1,)), ((), ())), preferred_element_type=F32)


def _dot_tn(a, b):
    return lax.dot_general(a, b, (((0,), (0,)), ((), ())), preferred_element_type=F32)


def _params(*sem):
    return pltpu.CompilerParams(dimension_semantics=sem, vmem_limit_bytes=VMEM_LIMIT)


def _ada_kernel(c_ref, w_ref, b_ref, o_ref):
    s = _silu(c_ref[...]).astype(BF16)
    o_ref[0] = _dot(s, w_ref[0].astype(BF16)) + b_ref[0]


def _ada_call(cvec, ada_w, ada_b):
    depth = ada_w.shape[0]
    return pl.pallas_call(
        _ada_kernel,
        grid=(depth, 3),
        in_specs=[
            pl.BlockSpec((ADA_ROWS, D_MODEL), lambda l, j: (0, 0)),
            pl.BlockSpec((1, D_MODEL, D_MODEL), lambda l, j: (l, 0, j)),
            pl.BlockSpec((1, 1, D_MODEL), lambda l, j: (l, 0, j)),
        ],
        out_specs=pl.BlockSpec((1, ADA_ROWS, D_MODEL), lambda l, j: (l, 0, j)),
        out_shape=jax.ShapeDtypeStruct((depth, ADA_ROWS, 3 * D_MODEL), F32),
        compiler_params=_params("arbitrary", "arbitrary"),
        name="ada",
    )(cvec, ada_w, ada_b.reshape(depth, 1, 3 * D_MODEL))


def _mod_spec(mod):
    if mod.shape[0] == 1:
        return pl.BlockSpec((1, 1, D_MODEL), lambda b, *_: (0, 0, 0))
    return pl.BlockSpec((1, 1, D_MODEL), lambda b, *_: (b, 0, 0))


EVEN_COLS = 6 * RET_WIDTH


def _even_in_kernel(x_ref, sh_ref, sc_ref, ng_ref, w_ref, o_ref):
    h = _rms_rows(x_ref[0], ng_ref[...]) * (1.0 + sc_ref[0]) + sh_ref[0]
    hb = h.astype(BF16)
    W = RET_WIDTH

    def proj(j):
        return _dot(hb, w_ref[:, j * W:(j + 1) * W])

    o_ref[0, :, 0 * W:1 * W] = proj(0).astype(BF16)
    o_ref[0, :, 1 * W:2 * W] = (proj(1) * (RET_DK ** -0.5)).astype(BF16)
    o_ref[0, :, 2 * W:3 * W] = proj(2).astype(BF16)
    o_ref[0, :, 3 * W:4 * W] = _silu(proj(3)).astype(BF16)
    o_ref[0, :, 4 * W:5 * W] = (_silu(proj(7)) * proj(4)).astype(BF16)
    o_ref[0, :, 5 * W:6 * W] = (proj(5) * proj(6)).astype(BF16)


def _even_in_call(x, sh, sc, ng, w, tm):
    B, T, _ = x.shape
    return pl.pallas_call(
        _even_in_kernel,
        grid=(B, T // tm),
        in_specs=[
            pl.BlockSpec((1, tm, D_MODEL), lambda b, i: (b, i, 0)),
            _mod_spec(sh), _mod_spec(sc),
            pl.BlockSpec((1, D_MODEL), lambda b, i: (0, 0)),
            pl.BlockSpec(w.shape, lambda b, i: (0, 0)),
        ],
        out_specs=pl.BlockSpec((1, tm, EVEN_COLS), lambda b, i: (b, i, 0)),
        out_shape=jax.ShapeDtypeStruct((B, T, EVEN_COLS), BF16),
        compiler_params=_params("parallel", "arbitrary"),
        name="even_in",
    )(x, sh, sc, ng, w)


def _even_mix_kernel(k_ref, v_ref, q_ref, sga_ref, u_ref, z_ref, zp_ref, zn_ref,
                     x_ref, g_ref, cw_ref, wo_ref, mask_ref, dec_ref, cdec_ref,
                     s0f_ref, s0b_ref,
                     xo_ref, nsf_ref, nsb_ref,
                     sb_all, sf_run, sb_run, ymix,
                     *, nc, nb, zero_init):
    p = pl.program_id(1)
    j = pl.program_id(2)
    C = RET_CHUNK
    H = RET_HEADS
    dv = RET_DV

    @pl.when((p == 0) & (j == 0))
    def _():
        if zero_init:
            sf_run[...] = jnp.zeros_like(sf_run)
            sb_run[...] = jnp.zeros_like(sb_run)
        else:
            sf_run[...] = s0f_ref[0]
            sb_run[...] = s0b_ref[0]

    @pl.when(p == 0)
    def _():
        blk = nb - 1 - j
        for cc in reversed(range(nc)):
            gc = blk * nc + cc
            rows = slice(cc * C, (cc + 1) * C)
            for h in range(H):
                cols = slice(h * dv, (h + 1) * dv)
                sb_all[gc, h] = sb_run[h].astype(BF16)
                kd = (k_ref[0, rows, cols].astype(F32) * dec_ref[3, h]).astype(BF16)
                sb_run[h] = sb_run[h] * cdec_ref[1, h] + _dot_tn(kd, v_ref[0, rows, cols])

        @pl.when(j == nb - 1)
        def _():
            nsb_ref[0] = sb_run[...]

    @pl.when(p == 1)
    def _():
        for cc in range(nc):
            gc = j * nc + cc
            rows = slice(cc * C, (cc + 1) * C)
            for h in range(H):
                cols = slice(h * dv, (h + 1) * dv)
                qh = q_ref[0, rows, cols]
                kh = k_ref[0, rows, cols]
                vh = v_ref[0, rows, cols]
                att = (_dot_nt(qh, kh) * mask_ref[h]).astype(BF16)
                o = (_dot(att, vh)
                     + _dot(qh, sf_run[h].astype(BF16)) * dec_ref[0, h]
                     + _dot(qh, sb_all[gc, h]) * dec_ref[1, h])
                kd = (kh.astype(F32) * dec_ref[2, h]).astype(BF16)
                sf_run[h] = sf_run[h] * cdec_ref[0, h] + _dot_tn(kd, vh)
                on = o * lax.rsqrt(jnp.mean(o * o, axis=-1, keepdims=True) + EPS)
                ymix[rows, cols] = (on * sga_ref[0, rows, cols].astype(F32)).astype(BF16)

        z = z_ref[0].astype(F32)
        tb = z.shape[0]
        row = lax.broadcasted_iota(jnp.int32, z.shape, 0)
        prev_row = zp_ref[0, BF16_SUBLANES - 1:BF16_SUBLANES, :].astype(F32)
        next_row = zn_ref[0, 0:1, :].astype(F32)
        prev_row = jnp.where(j > 0, prev_row, 0.0)
        next_row = jnp.where(j < nb - 1, next_row, 0.0)
        z_prev = jnp.where(row == 0, prev_row, pltpu.roll(z, 1, 0))
        z_next = jnp.where(row == tb - 1, next_row, pltpu.roll(z, tb - 1, 0))
        zc = z_prev * cw_ref[0:1, :] + z * cw_ref[1:2, :] + z_next * cw_ref[2:3, :]
        ymix[:, RET_WIDTH:] = (u_ref[0].astype(F32) * zc).astype(BF16)

        y = _dot(ymix[...], wo_ref[...])
        xo_ref[0] = x_ref[0] + g_ref[0] * y

        @pl.when(j == nb - 1)
        def _():
            nsf_ref[0] = sf_run[...]


def _even_mix_call(pe, x, g, conv_w, w_out, mask, dec, cdec, s0f, s0b, tb):
    B, T, _ = x.shape
    nb = T // tb
    nc = tb // RET_CHUNK
    zero_init = s0f is None
    if zero_init:
        s0f = jnp.zeros((1, RET_HEADS, RET_DK, RET_DV), F32)
        s0b = s0f
        s0_spec = pl.BlockSpec((1, RET_HEADS, RET_DK, RET_DV), lambda b, p, j: (0, 0, 0, 0))
    else:
        s0_spec = pl.BlockSpec((1, RET_HEADS, RET_DK, RET_DV), lambda b, p, j: (b, 0, 0, 0))
    W = RET_WIDTH
    hb = tb // BF16_SUBLANES

    def kv_map(col):
        return lambda b, p, j: (b, p * j + (1 - p) * (nb - 1 - j), col)

    def fwd_map(col):
        return lambda b, p, j: (b, p * j, col)

    kernel = functools.partial(_even_mix_kernel, nc=nc, nb=nb, zero_init=zero_init)
    state_shape = jax.ShapeDtypeStruct((B, RET_HEADS, RET_DK, RET_DV), F32)
    state_spec = pl.BlockSpec((1, RET_HEADS, RET_DK, RET_DV), lambda b, p, j: (b, 0, 0, 0))
    return pl.pallas_call(
        kernel,
        grid=(B, 2, nb),
        in_specs=[
            pl.BlockSpec((1, tb, W), kv_map(1)),
            pl.BlockSpec((1, tb, W), kv_map(2)),
            pl.BlockSpec((1, tb, W), fwd_map(0)),
            pl.BlockSpec((1, tb, W), fwd_map(3)),
            pl.BlockSpec((1, tb, W), fwd_map(4)),
            pl.BlockSpec((1, tb, W), fwd_map(5)),
            pl.BlockSpec((1, BF16_SUBLANES, W),
                         lambda b, p, j: (b, jnp.maximum(p * j * hb - 1, 0), 5)),
            pl.BlockSpec((1, BF16_SUBLANES, W),
                         lambda b, p, j: (b, jnp.minimum((p * j + 1) * hb, T // BF16_SUBLANES - 1), 5)),
            pl.BlockSpec((1, tb, D_MODEL), fwd_map(0)),
            _mod_spec(g),
            pl.BlockSpec(conv_w.shape, lambda b, p, j: (0, 0)),
            pl.BlockSpec(w_out.shape, lambda b, p, j: (0, 0)),
            pl.BlockSpec(mask.shape, lambda b, p, j: (0, 0, 0)),
            pl.BlockSpec(dec.shape, lambda b, p, j: (0, 0, 0, 0)),
            pl.BlockSpec(memory_space=pltpu.SMEM),
            s0_spec, s0_spec,
        ],
        out_specs=[
            pl.BlockSpec((1, tb, D_MODEL), fwd_map(0)),
            state_spec, state_spec,
        ],
        out_shape=[jax.ShapeDtypeStruct((B, T, D_MODEL), F32), state_shape, state_shape],
        scratch_shapes=[
            pltpu.VMEM((T // RET_CHUNK, RET_HEADS, RET_DK, RET_DV), BF16),
            pltpu.VMEM((RET_HEADS, RET_DK, RET_DV), F32),
            pltpu.VMEM((RET_HEADS, RET_DK, RET_DV), F32),
            pltpu.VMEM((tb, D_MODEL), BF16),
        ],
        compiler_params=_params("parallel", "arbitrary", "arbitrary"),
        name="even_mix",
    )(pe, pe, pe, pe, pe, pe, pe, pe, x, g, conv_w, w_out, mask, dec, cdec, s0f, s0b)


def _retention_tables():
    C = RET_CHUNK
    hh = jnp.arange(RET_HEADS, dtype=F32)
    lg_f = jnp.log(1.0 - 2.0 ** (-5.0 - hh))
    lg_b = jnp.log(1.0 - 2.0 ** (-5.0 - hh - RET_BWD_OFFSET))
    idx = jnp.arange(C, dtype=F32)
    diff = idx[:, None] - idx[None, :]
    m_f = jnp.where(diff >= 0, jnp.exp(lg_f[:, None, None] * jnp.maximum(diff, 0.0)), 0.0)
    m_b = jnp.where(diff <= 0, jnp.exp(lg_b[:, None, None] * jnp.maximum(-diff, 0.0)), 0.0)
    mask = m_f + m_b
    q_f = jnp.exp(lg_f[:, None] * (idx[None, :] + 1.0))
    q_b = jnp.exp(lg_b[:, None] * (C - idx[None, :]))
    k_f = jnp.exp(lg_f[:, None] * (C - 1.0 - idx[None, :]))
    k_b = jnp.exp(lg_b[:, None] * idx[None, :])
    dec = jnp.stack([q_f, q_b, k_f, k_b])
    dec = jnp.broadcast_to(dec[..., None], dec.shape + (RET_DV,))
    cdec = jnp.stack([jnp.exp(lg_f * C), jnp.exp(lg_b * C)])
    return mask, dec, cdec


def _rope_mix(r, cs_ref, first):
    if cs_ref is not None:
        t = r * cs_ref[...]
        r = t + pltpu.roll(t, QK_ROPE, 1)
    return jnp.where(first, r, 0.0)


def _mla_in_kernel(*refs, rope, emit_cache):
    x_ref, sh_ref, sc_ref, ng_ref, wqkv_ref, wkr_ref, wg_ref, qng_ref, kvng_ref, qup_ref, kvup_ref = refs[:11]
    pos = 11
    cs_ref = None
    if rope:
        cs_ref = refs[pos]
        pos += 1
    q_ref, k_ref, v_ref, sg_ref = refs[pos:pos + 4]
    pos += 4
    if emit_cache:
        ckv_ref, kr_ref = refs[pos:pos + 2]

    h = _rms_rows(x_ref[0], ng_ref[...]) * (1.0 + sc_ref[0]) + sh_ref[0]
    hb = h.astype(BF16)
    lat = _dot(hb, wqkv_ref[...])
    kr2 = _dot(hb, wkr_ref[...])
    sg_ref[0] = _silu(_dot(hb, wg_ref[...])).astype(BF16)
    qn = _rms_rows(lat[:, :Q_LORA], qng_ref[...]).astype(BF16)
    ckv = _rms_rows(lat[:, Q_LORA:], kvng_ref[...])
    qf = _dot(qn, qup_ref[...])
    kv = _dot(ckv.astype(BF16), kvup_ref[...])

    scale = (QK_NOPE + QK_ROPE) ** -0.5
    first = lax.broadcasted_iota(jnp.int32, kr2.shape, 1) < QK_ROPE
    k_rot = _rope_mix(kr2, cs_ref, first).astype(BF16)
    for hd in range(MLA_HEADS):
        base = hd * QK_PAD
        q_ref[0, hd, :, 0:QK_NOPE] = (qf[:, base:base + QK_NOPE] * scale).astype(BF16)
        q_rot = _rope_mix(qf[:, base + QK_NOPE:base + QK_PAD], cs_ref, first)
        q_ref[0, hd, :, QK_NOPE:QK_PAD] = (q_rot * scale).astype(BF16)
        k_ref[0, hd, :, 0:QK_NOPE] = kv[:, base:base + QK_NOPE].astype(BF16)
        k_ref[0, hd, :, QK_NOPE:QK_PAD] = k_rot
        v_ref[0, hd] = kv[:, base + QK_NOPE:base + QK_NOPE + V_HEAD].astype(BF16)
    if emit_cache:
        ckv_ref[0] = ckv
        kr_ref[0] = kr2[:, :QK_ROPE]


def _mla_in_call(x, sh, sc, ng, wts, cs, tm, emit_cache):
    B, T, _ = x.shape
    rope = cs is not None
    wqkv, wkr, wg, qng, kvng, qup, kvup = wts

    def full(a):
        return pl.BlockSpec(a.shape, lambda b, i: (0,) * a.ndim)

    in_specs = [
        pl.BlockSpec((1, tm, D_MODEL), lambda b, i: (b, i, 0)),
        _mod_spec(sh), _mod_spec(sc), full(ng),
        full(wqkv), full(wkr), full(wg), full(qng), full(kvng), full(qup), full(kvup),
    ]
    args = [x, sh, sc, ng, wqkv, wkr, wg, qng, kvng, qup, kvup]
    if rope:
        in_specs.append(pl.BlockSpec((tm, 2 * QK_ROPE), lambda b, i: (i, 0)))
        args.append(cs)
    head_spec = lambda w: pl.BlockSpec((1, MLA_HEADS, tm, w), lambda b, i: (b, 0, i, 0))
    out_specs = [head_spec(QK_PAD), head_spec(QK_PAD), head_spec(V_HEAD),
                 pl.BlockSpec((1, tm, D_MODEL), lambda b, i: (b, i, 0))]
    out_shape = [jax.ShapeDtypeStruct((B, MLA_HEADS, T, QK_PAD), BF16),
                 jax.ShapeDtypeStruct((B, MLA_HEADS, T, QK_PAD), BF16),
                 jax.ShapeDtypeStruct((B, MLA_HEADS, T, V_HEAD), BF16),
                 jax.ShapeDtypeStruct((B, T, D_MODEL), BF16)]
    if emit_cache:
        out_specs += [pl.BlockSpec((1, tm, KV_LORA), lambda b, i: (b, i, 0)),
                      pl.BlockSpec((1, tm, QK_ROPE), lambda b, i: (b, i, 0))]
        out_shape += [jax.ShapeDtypeStruct((B, T, KV_LORA), F32),
                      jax.ShapeDtypeStruct((B, T, QK_ROPE), F32)]
    return pl.pallas_call(
        functools.partial(_mla_in_kernel, rope=rope, emit_cache=emit_cache),
        grid=(B, T // tm),
        in_specs=in_specs, out_specs=out_specs, out_shape=out_shape,
        compiler_params=_params("parallel", "arbitrary"),
        name="mla_in",
    )(*args)


def _ctx_expand_kernel(ckv_ref, kr_ref, kvup_ref, k_ref, v_ref):
    kv = _dot(ckv_ref[0].astype(BF16), kvup_ref[...])
    kr = kr_ref[0].astype(BF16)
    zeros = jnp.zeros_like(kr)
    for hd in range(MLA_HEADS):
        base = hd * QK_PAD
        k_ref[0, hd, :, 0:QK_NOPE] = kv[:, base:base + QK_NOPE].astype(BF16)
        k_ref[0, hd, :, QK_NOPE:QK_NOPE + QK_ROPE] = kr
        k_ref[0, hd, :, QK_NOPE + QK_ROPE:QK_PAD] = zeros
        v_ref[0, hd] = kv[:, base + QK_NOPE:base + QK_NOPE + V_HEAD].astype(BF16)


def _ctx_expand_call(ckv, kr, kvup):
    B, L, _ = ckv.shape
    return pl.pallas_call(
        _ctx_expand_kernel,
        grid=(B,),
        in_specs=[
            pl.BlockSpec((1, L, KV_LORA), lambda b: (b, 0, 0)),
            pl.BlockSpec((1, L, QK_ROPE), lambda b: (b, 0, 0)),
            pl.BlockSpec(kvup.shape, lambda b: (0, 0)),
        ],
        out_specs=[pl.BlockSpec((1, MLA_HEADS, L, QK_PAD), lambda b: (b, 0, 0, 0)),
                   pl.BlockSpec((1, MLA_HEADS, L, V_HEAD), lambda b: (b, 0, 0, 0))],
        out_shape=[jax.ShapeDtypeStruct((B, MLA_HEADS, L, QK_PAD), BF16),
                   jax.ShapeDtypeStruct((B, MLA_HEADS, L, V_HEAD), BF16)],
        compiler_params=_params("parallel"),
        name="ctx_expand",
    )(ckv, kr, kvup)


def _softmax_av(q, ks, vs):
    s = [_dot_nt(q, k) for k in ks]
    m = functools.reduce(jnp.maximum, [jnp.max(a, axis=-1, keepdims=True) for a in s])
    p = [jnp.exp(a - m) for a in s]
    l = functools.reduce(jnp.add, [jnp.sum(a, axis=-1, keepdims=True) for a in p])
    o = functools.reduce(jnp.add, [_dot(a.astype(BF16), v) for a, v in zip(p, vs)])
    return o * (1.0 / l)


def _attn_ctx_kernel(q_ref, kc_ref, kl_ref, vc_ref, vl_ref, o_ref):
    o = _softmax_av(q_ref[0, 0], [kc_ref[0, 0], kl_ref[0, 0]], [vc_ref[0, 0], vl_ref[0, 0]])
    o_ref[0] = o.astype(BF16)


def _attn_ctx_call(q, kc, kl, vc, vl, tq):
    B, H, T, _ = q.shape
    Lc = kc.shape[2]
    return pl.pallas_call(
        _attn_ctx_kernel,
        grid=(B, H, T // tq),
        in_specs=[
            pl.BlockSpec((1, 1, tq, QK_PAD), lambda b, h, i: (b, h, i, 0)),
            pl.BlockSpec((1, 1, Lc, QK_PAD), lambda b, h, i: (b, h, 0, 0)),
            pl.BlockSpec((1, 1, T, QK_PAD), lambda b, h, i: (b, h, 0, 0)),
            pl.BlockSpec((1, 1, Lc, V_HEAD), lambda b, h, i: (b, h, 0, 0)),
            pl.BlockSpec((1, 1, T, V_HEAD), lambda b, h, i: (b, h, 0, 0)),
        ],
        out_specs=pl.BlockSpec((1, tq, V_HEAD), lambda b, h, i: (b, i, h)),
        out_shape=jax.ShapeDtypeStruct((B, T, H * V_HEAD), BF16),
        compiler_params=_params("parallel", "parallel", "arbitrary"),
        name="attn_ctx",
    )(q, kc, kl, vc, vl)


def _attn_self_kernel(q_ref, k_ref, v_ref, o_ref):
    for hd in range(MLA_HEADS):
        o = _softmax_av(q_ref[0, hd], [k_ref[0, hd]], [v_ref[0, hd]])
        o_ref[0, :, hd * V_HEAD:(hd + 1) * V_HEAD] = o.astype(BF16)


def _attn_self_call(q, k, v, seq):
    _, H, N, _ = q.shape
    return pl.pallas_call(
        _attn_self_kernel,
        grid=(N // seq,),
        in_specs=[
            pl.BlockSpec((1, H, seq, QK_PAD), lambda b: (0, 0, b, 0)),
            pl.BlockSpec((1, H, seq, QK_PAD), lambda b: (0, 0, b, 0)),
            pl.BlockSpec((1, H, seq, V_HEAD), lambda b: (0, 0, b, 0)),
        ],
        out_specs=pl.BlockSpec((1, seq, H * V_HEAD), lambda b: (0, b, 0)),
        out_shape=jax.ShapeDtypeStruct((1, N, H * V_HEAD), BF16),
        compiler_params=_params("parallel"),
        name="attn_self",
    )(q, k, v)


def _mla_out_kernel(o_ref, sg_ref, x_ref, g_ref, wo_ref, fg_ref, y_ref):
    a = (o_ref[0].astype(F32) * sg_ref[0].astype(F32)).astype(BF16)
    x2 = x_ref[0] + g_ref[0] * _dot(a, wo_ref[...])
    y_ref[0] = _rms_rows(x2, fg_ref[...])


def _mla_out_call(o, sg, x, g, w_out, fg, tm):
    B, T, _ = x.shape
    tok = lambda: pl.BlockSpec((1, tm, D_MODEL), lambda b, i: (b, i, 0))
    return pl.pallas_call(
        _mla_out_kernel,
        grid=(B, T // tm),
        in_specs=[tok(), tok(), tok(), _mod_spec(g),
                  pl.BlockSpec(w_out.shape, lambda b, i: (0, 0)),
                  pl.BlockSpec((1, D_MODEL), lambda b, i: (0, 0))],
        out_specs=tok(),
        out_shape=jax.ShapeDtypeStruct((B, T, D_MODEL), F32),
        compiler_params=_params("parallel", "arbitrary"),
        name="mla_out",
    )(o, sg, x, g, w_out, fg)


def _swap_halves(w):
    f = QK_ROPE // 4
    return jnp.concatenate([-w[..., f:2 * f], w[..., 0:f], -w[..., 3 * f:4 * f], w[..., 2 * f:3 * f]], axis=-1)


def _rope_table(T):
    f = QK_ROPE // 4
    t = jnp.arange(T)
    row = (t // GRID_W).astype(F32)
    col = (t % GRID_W).astype(F32)
    inv = ROPE_BASE ** (-jnp.arange(f, dtype=F32) / f)
    ar = row[:, None] * inv
    ac = col[:, None] * inv
    cos = jnp.concatenate([jnp.cos(ar), jnp.cos(ar), jnp.cos(ac), jnp.cos(ac)], axis=-1)
    sin = jnp.concatenate([jnp.sin(ar), jnp.sin(ar), jnp.sin(ac), jnp.sin(ac)], axis=-1)
    return jnp.concatenate([cos, sin], axis=-1)


def kernel(x_prompt, x_sample, c, state_ret_fwd, state_ret_bwd, cache_mla_ckv, cache_mla_krope, c_ctx,
           ada_w, ada_b, norm_g, even_in_w, even_conv_w, even_out_w, odd_in_w, odd_q_norm_g,
           odd_kv_norm_g, odd_q_up_w, odd_kv_up_w, odd_out_w, final_norm_g):
    BP, SEQ, D = x_prompt.shape
    BS, TS, _ = x_sample.shape
    NP = BP * SEQ

    cvec = jnp.concatenate([c, c_ctx[None, :], jnp.zeros((ADA_ROWS - BS - 1, D), F32)], axis=0)
    mod = _ada_call(cvec, ada_w, ada_b)

    def mods(l):
        m = mod[l].reshape(ADA_ROWS, 3, 1, D)
        return ([m[BS:BS + 1, i] for i in range(3)], [m[:BS, i] for i in range(3)])

    (sh_p, sc_p, g_p), (sh_s, sc_s, g_s) = mods(0)
    ng = norm_g[0][None, :]
    w_in = even_in_w[0].astype(BF16)
    w_out = even_out_w[0].astype(BF16)
    conv_w = even_conv_w[0]
    mask, dec, cdec = _retention_tables()

    xp = x_prompt.reshape(1, NP, D)
    pe_p = _even_in_call(xp, sh_p, sc_p, ng, w_in, 512)
    pe_s = _even_in_call(x_sample, sh_s, sc_s, ng, w_in, 512)
    xp1, nsf, nsb = _even_mix_call(pe_p.reshape(BP, SEQ, EVEN_COLS), x_prompt, g_p, conv_w, w_out,
                                   mask, dec, cdec, None, None, SEQ)
    xs1, _, _ = _even_mix_call(pe_s, x_sample, g_s, conv_w, w_out, mask, dec, cdec,
                               state_ret_fwd[:, 0], state_ret_bwd[:, 0], 512)

    (sh_p, sc_p, g_p), (sh_s, sc_s, g_s) = mods(1)
    ng = norm_g[1][None, :]
    w_in = odd_in_w[0]
    nq = Q_LORA + KV_LORA
    wqkv = w_in[:, :nq].astype(BF16)
    wkr = w_in[:, nq:nq + QK_ROPE]
    wkr = jnp.concatenate([wkr, _swap_halves(wkr)], axis=-1).astype(BF16)
    wg = w_in[:, nq + QK_ROPE:].astype(BF16)
    qup = odd_q_up_w[0].reshape(Q_LORA, MLA_HEADS, QK_NOPE + QK_ROPE)
    qup = jnp.concatenate([qup, _swap_halves(qup[..., QK_NOPE:])], axis=-1)
    qup = qup.reshape(Q_LORA, MLA_HEADS * QK_PAD).astype(BF16)
    kvup = odd_kv_up_w[0].astype(BF16)
    wts = (wqkv, wkr, wg, odd_q_norm_g[0][None, :], odd_kv_norm_g[0][None, :], qup, kvup)
    w_out = odd_out_w[0].astype(BF16)
    fg = final_norm_g[None, :]

    xp1f = xp1.reshape(1, NP, D)
    q_p, k_p, v_p, sg_p, ckv_p, kr_p = _mla_in_call(xp1f, sh_p, sc_p, ng, wts, None, 512, True)
    q_s, k_s, v_s, sg_s = _mla_in_call(xs1, sh_s, sc_s, ng, wts, _rope_table(TS), 512, False)
    k_c, v_c = _ctx_expand_call(cache_mla_ckv[:, 0], cache_mla_krope[:, 0], kvup)

    o_p = _attn_self_call(q_p, k_p, v_p, SEQ)
    o_s = _attn_ctx_call(q_s, k_c, k_s, v_c, v_s, 256)

    y_p = _mla_out_call(o_p, sg_p, xp1f, g_p, w_out, fg, 512)
    y_s = _mla_out_call(o_s, sg_s, xs1, g_s, w_out, fg, 512)

    return (y_p.reshape(BP, SEQ, D), y_s,
            nsf[:, None], nsb[:, None],
            ckv_p.reshape(BP, 1, SEQ, KV_LORA), kr_p.reshape(BP, 1, SEQ, QK_ROPE))
```

```python
import functools

import jax
import jax.numpy as jnp
from jax import lax
from jax.experimental import pallas as pl
from jax.experimental.pallas import tpu as pltpu

F32 = jnp.float32
BF16 = jnp.bfloat16

D_MODEL = 1024
GRID_W = 64
EPS = 1e-6
RET_HEADS = 4
RET_DK = 128
RET_DV = 128
RET_WIDTH = RET_HEADS * RET_DV
RET_CHUNK = 128
RET_BWD_OFFSET = 0.5
CONV_WIDTH = D_MODEL - RET_WIDTH
MLA_HEADS = 8
QK_NOPE = 128
QK_ROPE = 64
V_HEAD = 128
Q_LORA = 384
KV_LORA = 256
ROPE_BASE = 10000.0
QK_PAD = 256
ADA_ROWS = 16
BF16_SUBLANES = 16
F32_SUBLANES = 8
MXU_DIM = 256
LOG2_E = 1.4426950408889634
VMEM_LIMIT = 56 * 1024 * 1024


def _silu(x):
    return x * (1.0 / (1.0 + jnp.exp(-x)))


def _rms_rows(x, g):
    return x * lax.rsqrt(jnp.mean(x * x, axis=-1, keepdims=True) + EPS) * g


def _dot(a, b):
    return jnp.dot(a, b, preferred_element_type=F32)


def _dot_nt(a, b):
    return lax.dot_general(a, b, (((1,), (1,)), ((), ())), preferred_element_type=F32)


def _dot_tn(a, b):
    return lax.dot_general(a, b, (((0,), (0,)), ((), ())), preferred_element_type=F32)


def _params(*sem):
    return pltpu.CompilerParams(dimension_semantics=sem, vmem_limit_bytes=VMEM_LIMIT)


def _ada_kernel(c_ref, w_ref, b_ref, o_ref):
    s = _silu(c_ref[...]).astype(BF16)
    o_ref[0] = _dot(s, w_ref[0].astype(BF16)) + b_ref[0]


def _ada_call(cvec, ada_w, ada_b):
    depth = ada_w.shape[0]
    return pl.pallas_call(
        _ada_kernel,
        grid=(depth, 3),
        in_specs=[
            pl.BlockSpec((ADA_ROWS, D_MODEL), lambda l, j: (0, 0)),
            pl.BlockSpec((1, D_MODEL, D_MODEL), lambda l, j: (l, 0, j)),
            pl.BlockSpec((1, 1, D_MODEL), lambda l, j: (l, 0, j)),
        ],
        out_specs=pl.BlockSpec((1, ADA_ROWS, D_MODEL), lambda l, j: (l, 0, j)),
        out_shape=jax.ShapeDtypeStruct((depth, ADA_ROWS, 3 * D_MODEL), F32),
        compiler_params=_params("arbitrary", "arbitrary"),
        name="ada",
    )(cvec, ada_w, ada_b.reshape(depth, 1, 3 * D_MODEL))


def _mod_spec(mod):
    if mod.shape[0] == 1:
        return pl.BlockSpec((1, 1, D_MODEL), lambda b, *_: (0, 0, 0))
    return pl.BlockSpec((1, 1, D_MODEL), lambda b, *_: (b, 0, 0))


EVEN_COLS = 6 * RET_WIDTH


def _even_in_kernel(x_ref, sh_ref, sc_ref, ng_ref, w_ref, o_ref):
    h = _rms_rows(x_ref[0], ng_ref[...]) * (1.0 + sc_ref[0]) + sh_ref[0]
    hb = h.astype(BF16)
    W = RET_WIDTH

    def proj(j):
        return _dot(hb, w_ref[:, j * W:(j + 1) * W])

    o_ref[0, :, 0 * W:1 * W] = proj(0).astype(BF16)
    o_ref[0, :, 1 * W:2 * W] = (proj(1) * (RET_DK ** -0.5)).astype(BF16)
    o_ref[0, :, 2 * W:3 * W] = proj(2).astype(BF16)
    o_ref[0, :, 3 * W:4 * W] = _silu(proj(3)).astype(BF16)
    o_ref[0, :, 4 * W:5 * W] = (_silu(proj(7)) * proj(4)).astype(BF16)
    o_ref[0, :, 5 * W:6 * W] = (proj(5) * proj(6)).astype(BF16)


def _even_in_call(x, sh, sc, ng, w, tm):
    B, T, _ = x.shape
    return pl.pallas_call(
        _even_in_kernel,
        grid=(B, T // tm),
        in_specs=[
            pl.BlockSpec((1, tm, D_MODEL), lambda b, i: (b, i, 0)),
            _mod_spec(sh), _mod_spec(sc),
            pl.BlockSpec((1, D_MODEL), lambda b, i: (0, 0)),
            pl.BlockSpec(w.shape, lambda b, i: (0, 0)),
        ],
        out_specs=pl.BlockSpec((1, tm, EVEN_COLS), lambda b, i: (b, i, 0)),
        out_shape=jax.ShapeDtypeStruct((B, T, EVEN_COLS), BF16),
        compiler_params=_params("parallel", "arbitrary"),
        name="even_in",
    )(x, sh, sc, ng, w)


def _even_mix_kernel(k_ref, v_ref, q_ref, sga_ref, u_ref, z_ref, zp_ref, zn_ref,
                     x_ref, g_ref, cw_ref, wo_ref, mask_ref, dec_ref, cdec_ref,
                     s0f_ref, s0b_ref,
                     xo_ref, nsf_ref, nsb_ref,
                     sb_all, sf_run, sb_run, ymix,
                     *, nc, nb, zero_init):
    p = pl.program_id(1)
    j = pl.program_id(2)
    C = RET_CHUNK
    H = RET_HEADS
    dv = RET_DV

    @pl.when((p == 0) & (j == 0))
    def _():
        if zero_init:
            sf_run[...] = jnp.zeros_like(sf_run)
            sb_run[...] = jnp.zeros_like(sb_run)
        else:
            sf_run[...] = s0f_ref[0]
            sb_run[...] = s0b_ref[0]

    @pl.when(p == 0)
    def _():
        blk = nb - 1 - j
        for cc in reversed(range(nc)):
            gc = blk * nc + cc
            rows = slice(cc * C, (cc + 1) * C)
            for h in range(H):
                cols = slice(h * dv, (h + 1) * dv)
                sb_all[gc, h] = sb_run[h].astype(BF16)
                kd = (k_ref[0, rows, cols].astype(F32) * dec_ref[3, h]).astype(BF16)
                sb_run[h] = sb_run[h] * cdec_ref[1, h] + _dot_tn(kd, v_ref[0, rows, cols])

        @pl.when(j == nb - 1)
        def _():
            nsb_ref[0] = sb_run[...]

    @pl.when(p == 1)
    def _():
        for cc in range(nc):
            gc = j * nc + cc
            rows = slice(cc * C, (cc + 1) * C)
            for h in range(H):
                cols = slice(h * dv, (h + 1) * dv)
                qh = q_ref[0, rows, cols]
                kh = k_ref[0, rows, cols]
                vh = v_ref[0, rows, cols]
                att = (_dot_nt(qh, kh) * mask_ref[h]).astype(BF16)
                o = (_dot(att, vh)
                     + _dot(qh, sf_run[h].astype(BF16)) * dec_ref[0, h]
                     + _dot(qh, sb_all[gc, h]) * dec_ref[1, h])
                kd = (kh.astype(F32) * dec_ref[2, h]).astype(BF16)
                sf_run[h] = sf_run[h] * cdec_ref[0, h] + _dot_tn(kd, vh)
                on = o * lax.rsqrt(jnp.mean(o * o, axis=-1, keepdims=True) + EPS)
                ymix[rows, cols] = (on * sga_ref[0, rows, cols].astype(F32)).astype(BF16)

        z = z_ref[0].astype(F32)
        tb = z.shape[0]
        row = lax.broadcasted_iota(jnp.int32, z.shape, 0)
        prev_row = zp_ref[0, BF16_SUBLANES - 1:BF16_SUBLANES, :].astype(F32)
        next_row = zn_ref[0, 0:1, :].astype(F32)
        prev_row = jnp.where(j > 0, prev_row, 0.0)
        next_row = jnp.where(j < nb - 1, next_row, 0.0)
        z_prev = jnp.where(row == 0, prev_row, pltpu.roll(z, 1, 0))
        z_next = jnp.where(row == tb - 1, next_row, pltpu.roll(z, tb - 1, 0))
        zc = z_prev * cw_ref[0:1, :] + z * cw_ref[1:2, :] + z_next * cw_ref[2:3, :]
        ymix[:, RET_WIDTH:] = (u_ref[0].astype(F32) * zc).astype(BF16)

        y = _dot(ymix[...], wo_ref[...])
        xo_ref[0] = x_ref[0] + g_ref[0] * y

        @pl.when(j == nb - 1)
        def _():
            nsf_ref[0] = sf_run[...]


def _even_mix_call(pe, x, g, conv_w, w_out, mask, dec, cdec, s0f, s0b, tb):
    B, T, _ = x.shape
    nb = T // tb
    nc = tb // RET_CHUNK
    zero_init = s0f is None
    if zero_init:
        s0f = jnp.zeros((1, RET_HEADS, RET_DK, RET_DV), F32)
        s0b = s0f
        s0_spec = pl.BlockSpec((1, RET_HEADS, RET_DK, RET_DV), lambda b, p, j: (0, 0, 0, 0))
    else:
        s0_spec = pl.BlockSpec((1, RET_HEADS, RET_DK, RET_DV), lambda b, p, j: (b, 0, 0, 0))
    W = RET_WIDTH
    hb = tb // BF16_SUBLANES

    def kv_map(col):
        return lambda b, p, j: (b, p * j + (1 - p) * (nb - 1 - j), col)

    def fwd_map(col):
        return lambda b, p, j: (b, p * j, col)

    kernel = functools.partial(_even_mix_kernel, nc=nc, nb=nb, zero_init=zero_init)
    state_shape = jax.ShapeDtypeStruct((B, RET_HEADS, RET_DK, RET_DV), F32)
    state_spec = pl.BlockSpec((1, RET_HEADS, RET_DK, RET_DV), lambda b, p, j: (b, 0, 0, 0))
    return pl.pallas_call(
        kernel,
        grid=(B, 2, nb),
        in_specs=[
            pl.BlockSpec((1, tb, W), kv_map(1)),
            pl.BlockSpec((1, tb, W), kv_map(2)),
            pl.BlockSpec((1, tb, W), fwd_map(0)),
            pl.BlockSpec((1, tb, W), fwd_map(3)),
            pl.BlockSpec((1, tb, W), fwd_map(4)),
            pl.BlockSpec((1, tb, W), fwd_map(5)),
            pl.BlockSpec((1, BF16_SUBLANES, W),
                         lambda b, p, j: (b, jnp.maximum(p * j * hb - 1, 0), 5)),
            pl.BlockSpec((1, BF16_SUBLANES, W),
                         lambda b, p, j: (b, jnp.minimum((p * j + 1) * hb, T // BF16_SUBLANES - 1), 5)),
            pl.BlockSpec((1, tb, D_MODEL), fwd_map(0)),
            _mod_spec(g),
            pl.BlockSpec(conv_w.shape, lambda b, p, j: (0, 0)),
            pl.BlockSpec(w_out.shape, lambda b, p, j: (0, 0)),
            pl.BlockSpec(mask.shape, lambda b, p, j: (0, 0, 0)),
            pl.BlockSpec(dec.shape, lambda b, p, j: (0, 0, 0, 0)),
            pl.BlockSpec(memory_space=pltpu.SMEM),
            s0_spec, s0_spec,
        ],
        out_specs=[
            pl.BlockSpec((1, tb, D_MODEL), fwd_map(0)),
            state_spec, state_spec,
        ],
        out_shape=[jax.ShapeDtypeStruct((B, T, D_MODEL), F32), state_shape, state_shape],
        scratch_shapes=[
            pltpu.VMEM((T // RET_CHUNK, RET_HEADS, RET_DK, RET_DV), BF16),
            pltpu.VMEM((RET_HEADS, RET_DK, RET_DV), F32),
            pltpu.VMEM((RET_HEADS, RET_DK, RET_DV), F32),
            pltpu.VMEM((tb, D_MODEL), BF16),
        ],
        compiler_params=_params("parallel", "arbitrary", "arbitrary"),
        name="even_mix",
    )(pe, pe, pe, pe, pe, pe, pe, pe, x, g, conv_w, w_out, mask, dec, cdec, s0f, s0b)


def _retention_tables():
    C = RET_CHUNK
    hh = jnp.arange(RET_HEADS, dtype=F32)
    lg_f = jnp.log(1.0 - 2.0 ** (-5.0 - hh))
    lg_b = jnp.log(1.0 - 2.0 ** (-5.0 - hh - RET_BWD_OFFSET))
    idx = jnp.arange(C, dtype=F32)
    diff = idx[:, None] - idx[None, :]
    m_f = jnp.where(diff >= 0, jnp.exp(lg_f[:, None, None] * jnp.maximum(diff, 0.0)), 0.0)
    m_b = jnp.where(diff <= 0, jnp.exp(lg_b[:, None, None] * jnp.maximum(-diff, 0.0)), 0.0)
    mask = m_f + m_b
    q_f = jnp.exp(lg_f[:, None] * (idx[None, :] + 1.0))
    q_b = jnp.exp(lg_b[:, None] * (C - idx[None, :]))
    k_f = jnp.exp(lg_f[:, None] * (C - 1.0 - idx[None, :]))
    k_b = jnp.exp(lg_b[:, None] * idx[None, :])
    dec = jnp.stack([q_f, q_b, k_f, k_b])
    dec = jnp.broadcast_to(dec[..., None], dec.shape + (RET_DV,))
    cdec = jnp.stack([jnp.exp(lg_f * C), jnp.exp(lg_b * C)])
    return mask, dec, cdec


def _rope_mix(r, cs_ref, first):
    if cs_ref is not None:
        t = r * cs_ref[...]
        r = t + pltpu.roll(t, QK_ROPE, 1)
    return jnp.where(first, r, 0.0)


def _mla_in_kernel(*refs, rope, emit_cache):
    (x_ref, sh_ref, sc_ref, ng_ref, wqkv_ref, wkr_ref, wg_ref, qng_ref, kvng_ref, qup_ref,
     kup_ref, vupt_ref) = refs[:12]
    pos = 12
    cs_ref = None
    if rope:
        cs_ref = refs[pos]
        pos += 1
    q_ref, k_ref, v_ref, sg_ref = refs[pos:pos + 4]
    pos += 4
    if emit_cache:
        ckv_ref, kr_ref = refs[pos:pos + 2]

    h = _rms_rows(x_ref[0], ng_ref[...]) * (1.0 + sc_ref[0]) + sh_ref[0]
    hb = h.astype(BF16)
    lat = _dot(hb, wqkv_ref[...])
    kr2 = _dot(hb, wkr_ref[...])
    sg_ref[0] = _silu(_dot(hb, wg_ref[...])).astype(BF16)
    qn = _rms_rows(lat[:, :Q_LORA], qng_ref[...]).astype(BF16)
    ckv = _rms_rows(lat[:, Q_LORA:], kvng_ref[...])
    qf = _dot(qn, qup_ref[...])
    ckv_b = ckv.astype(BF16)
    kn = _dot(ckv_b, kup_ref[...])
    vt = _dot_nt(vupt_ref[...], ckv_b)

    scale = (QK_NOPE + QK_ROPE) ** -0.5 * LOG2_E
    first = lax.broadcasted_iota(jnp.int32, kr2.shape, 1) < QK_ROPE
    k_rot = _rope_mix(kr2, cs_ref, first).astype(BF16)
    for hd in range(MLA_HEADS):
        base = hd * QK_PAD
        q_ref[0, hd, :, 0:QK_NOPE] = (qf[:, base:base + QK_NOPE] * scale).astype(BF16)
        q_rot = _rope_mix(qf[:, base + QK_NOPE:base + QK_PAD], cs_ref, first)
        q_ref[0, hd, :, QK_NOPE:QK_PAD] = (q_rot * scale).astype(BF16)
        k_ref[0, hd, :, 0:QK_NOPE] = kn[:, hd * QK_NOPE:(hd + 1) * QK_NOPE].astype(BF16)
        k_ref[0, hd, :, QK_NOPE:QK_PAD] = k_rot
        v_ref[0, hd] = vt[hd * V_HEAD:(hd + 1) * V_HEAD, :].astype(BF16)
    if emit_cache:
        ckv_ref[0] = ckv
        kr_ref[0] = kr2[:, :QK_ROPE]


def _mla_in_call(x, sh, sc, ng, wts, cs, tm, emit_cache):
    B, T, _ = x.shape
    rope = cs is not None
    wqkv, wkr, wg, qng, kvng, qup, kup, vupt = wts

    def full(a):
        return pl.BlockSpec(a.shape, lambda b, i: (0,) * a.ndim)

    in_specs = [
        pl.BlockSpec((1, tm, D_MODEL), lambda b, i: (b, i, 0)),
        _mod_spec(sh), _mod_spec(sc), full(ng),
        full(wqkv), full(wkr), full(wg), full(qng), full(kvng), full(qup), full(kup), full(vupt),
    ]
    args = [x, sh, sc, ng, wqkv, wkr, wg, qng, kvng, qup, kup, vupt]
    if rope:
        in_specs.append(pl.BlockSpec((tm, 2 * QK_ROPE), lambda b, i: (i, 0)))
        args.append(cs)
    head_spec = lambda w: pl.BlockSpec((1, MLA_HEADS, tm, w), lambda b, i: (b, 0, i, 0))
    out_specs = [head_spec(QK_PAD), head_spec(QK_PAD),
                 pl.BlockSpec((1, MLA_HEADS, V_HEAD, tm), lambda b, i: (b, 0, 0, i)),
                 pl.BlockSpec((1, tm, D_MODEL), lambda b, i: (b, i, 0))]
    out_shape = [jax.ShapeDtypeStruct((B, MLA_HEADS, T, QK_PAD), BF16),
                 jax.ShapeDtypeStruct((B, MLA_HEADS, T, QK_PAD), BF16),
                 jax.ShapeDtypeStruct((B, MLA_HEADS, V_HEAD, T), BF16),
                 jax.ShapeDtypeStruct((B, T, D_MODEL), BF16)]
    if emit_cache:
        out_specs += [pl.BlockSpec((1, tm, KV_LORA), lambda b, i: (b, i, 0)),
                      pl.BlockSpec((1, tm, QK_ROPE), lambda b, i: (b, i, 0))]
        out_shape += [jax.ShapeDtypeStruct((B, T, KV_LORA), F32),
                      jax.ShapeDtypeStruct((B, T, QK_ROPE), F32)]
    return pl.pallas_call(
        functools.partial(_mla_in_kernel, rope=rope, emit_cache=emit_cache),
        grid=(B, T // tm),
        in_specs=in_specs, out_specs=out_specs, out_shape=out_shape,
        compiler_params=_params("parallel", "arbitrary"),
        name="mla_in",
    )(*args)


def _ctx_expand_kernel(ckv_ref, kr_ref, kup_ref, vupt_ref, k_ref, v_ref):
    ckv_b = ckv_ref[0].astype(BF16)
    kn = _dot(ckv_b, kup_ref[...])
    vt = _dot_nt(vupt_ref[...], ckv_b)
    kr = kr_ref[0].astype(BF16)
    zeros = jnp.zeros_like(kr)
    for hd in range(MLA_HEADS):
        k_ref[0, hd, :, 0:QK_NOPE] = kn[:, hd * QK_NOPE:(hd + 1) * QK_NOPE].astype(BF16)
        k_ref[0, hd, :, QK_NOPE:QK_NOPE + QK_ROPE] = kr
        k_ref[0, hd, :, QK_NOPE + QK_ROPE:QK_PAD] = zeros
        v_ref[0, hd] = vt[hd * V_HEAD:(hd + 1) * V_HEAD, :].astype(BF16)


def _ctx_expand_call(ckv, kr, kup, vupt):
    B, L, _ = ckv.shape
    return pl.pallas_call(
        _ctx_expand_kernel,
        grid=(B,),
        in_specs=[
            pl.BlockSpec((1, L, KV_LORA), lambda b: (b, 0, 0)),
            pl.BlockSpec((1, L, QK_ROPE), lambda b: (b, 0, 0)),
            pl.BlockSpec(kup.shape, lambda b: (0, 0)),
            pl.BlockSpec(vupt.shape, lambda b: (0, 0)),
        ],
        out_specs=[pl.BlockSpec((1, MLA_HEADS, L, QK_PAD), lambda b: (b, 0, 0, 0)),
                   pl.BlockSpec((1, MLA_HEADS, V_HEAD, L), lambda b: (b, 0, 0, 0))],
        out_shape=[jax.ShapeDtypeStruct((B, MLA_HEADS, L, QK_PAD), BF16),
                   jax.ShapeDtypeStruct((B, MLA_HEADS, V_HEAD, L), BF16)],
        compiler_params=_params("parallel"),
        name="ctx_expand",
    )(ckv, kr, kup, vupt)


def _attend_t(q, chunks):
    loaders = [lambda k=k: _dot_nt(k(), q) for k, _ in chunks]
    s_next = loaders[0]()
    m = l = acc = None
    for c, (_, vt) in enumerate(chunks):
        s = s_next
        if c + 1 < len(chunks):
            s_next = loaders[c + 1]()
        cmax = jnp.max(s, axis=0, keepdims=True)
        m_new = cmax if m is None else jnp.maximum(m, cmax)
        p = jnp.exp2(s - m_new)
        csum = jnp.sum(p, axis=0, keepdims=True)
        pv = _dot(vt(), p.astype(BF16))
        if m is None:
            l, acc = csum, pv
        else:
            alpha = jnp.exp2(m - m_new)
            l = l * alpha + csum
            acc = acc * alpha + pv
        m = m_new
    return (acc * (1.0 / l)).T


def _attn_ctx_kernel(q_ref, kc_ref, kl_ref, vc_ref, vl_ref, o_ref, s0_scr, s1_scr, *, tq):
    T = q_ref.shape[2]
    Lc = kc_ref.shape[2]
    nq = T // tq
    tiles = ([(kc_ref, vc_ref, j * MXU_DIM) for j in range(Lc // MXU_DIM)]
             + [(kl_ref, vl_ref, j * MXU_DIM) for j in range(T // MXU_DIM)])
    groups = MXU_DIM // F32_SUBLANES

    def score_tile(qb, t, s_dst):
        kref, _, off = tiles[t]
        s = _dot_nt(kref[0, 0, off:off + MXU_DIM, :], qb)
        s_dst[t * MXU_DIM:(t + 1) * MXU_DIM, :] = s
        return jnp.max(s.reshape(groups, F32_SUBLANES, tq), axis=0)

    def step(row, row_next, m_cur, s_cur, s_nxt):
        q_next = q_ref[0, 0, pl.ds(row_next, tq), :]
        mrun = lacc = acc = None
        for t, (_, vref, off) in enumerate(tiles):
            part = score_tile(q_next, t, s_nxt)
            mrun = part if mrun is None else jnp.maximum(mrun, part)
            p = jnp.exp2(s_cur[t * MXU_DIM:(t + 1) * MXU_DIM, :] - m_cur)
            ps = jnp.sum(p.reshape(groups, F32_SUBLANES, tq), axis=0)
            lacc = ps if lacc is None else lacc + ps
            pv = _dot(vref[0, 0, :, off:off + MXU_DIM], p.astype(BF16))
            acc = pv if acc is None else acc + pv
        l = jnp.sum(lacc, axis=0, keepdims=True)
        o_ref[0, pl.ds(row, tq), :] = (acc * (1.0 / l)).T.astype(BF16)
        return jnp.max(mrun, axis=0, keepdims=True)

    q0 = q_ref[0, 0, 0:tq, :]
    mrun = score_tile(q0, 0, s0_scr)
    for t in range(1, len(tiles)):
        mrun = jnp.maximum(mrun, score_tile(q0, t, s0_scr))
    m0 = jnp.max(mrun, axis=0, keepdims=True)

    def body(i, m_even):
        r0 = pl.multiple_of(i * (2 * tq), 2 * tq)
        r1 = r0 + tq
        r2 = jnp.minimum(r1 + tq, T - tq)
        m_odd = step(r0, r1, m_even, s0_scr, s1_scr)
        return step(r1, pl.multiple_of(r2, tq), m_odd, s1_scr, s0_scr)

    lax.fori_loop(0, nq // 2, body, m0)


def _attn_ctx_call(q, kc, kl, vc, vl, tq):
    B, H, T, _ = q.shape
    Lc = kc.shape[2]
    assert (T // tq) % 2 == 0
    return pl.pallas_call(
        functools.partial(_attn_ctx_kernel, tq=tq),
        scratch_shapes=[pltpu.VMEM((Lc + T, tq), F32), pltpu.VMEM((Lc + T, tq), F32)],
        grid=(B, H),
        in_specs=[
            pl.BlockSpec((1, 1, T, QK_PAD), lambda b, h: (b, h, 0, 0)),
            pl.BlockSpec((1, 1, Lc, QK_PAD), lambda b, h: (b, h, 0, 0)),
            pl.BlockSpec((1, 1, T, QK_PAD), lambda b, h: (b, h, 0, 0)),
            pl.BlockSpec((1, 1, V_HEAD, Lc), lambda b, h: (b, h, 0, 0)),
            pl.BlockSpec((1, 1, V_HEAD, T), lambda b, h: (b, h, 0, 0)),
        ],
        out_specs=pl.BlockSpec((1, T, V_HEAD), lambda b, h: (b, 0, h)),
        out_shape=jax.ShapeDtypeStruct((B, T, H * V_HEAD), BF16),
        compiler_params=_params("parallel", "arbitrary"),
        name="attn_ctx",
    )(q, kc, kl, vc, vl)


def _attn_self_kernel(q_ref, k_ref, v_ref, o_ref):
    for hd in range(MLA_HEADS):
        o = _attend_t(q_ref[0, hd], [(lambda hd=hd: k_ref[0, hd], lambda hd=hd: v_ref[0, hd])])
        o_ref[0, :, hd * V_HEAD:(hd + 1) * V_HEAD] = o.astype(BF16)


def _attn_self_call(q, k, v, seq):
    _, H, N, _ = q.shape
    return pl.pallas_call(
        _attn_self_kernel,
        grid=(N // seq,),
        in_specs=[
            pl.BlockSpec((1, H, seq, QK_PAD), lambda b: (0, 0, b, 0)),
            pl.BlockSpec((1, H, seq, QK_PAD), lambda b: (0, 0, b, 0)),
            pl.BlockSpec((1, H, V_HEAD, seq), lambda b: (0, 0, 0, b)),
        ],
        out_specs=pl.BlockSpec((1, seq, H * V_HEAD), lambda b: (0, b, 0)),
        out_shape=jax.ShapeDtypeStruct((1, N, H * V_HEAD), BF16),
        compiler_params=_params("parallel"),
        name="attn_self",
    )(q, k, v)


def _mla_out_kernel(o_ref, sg_ref, x_ref, g_ref, wo_ref, fg_ref, y_ref):
    a = (o_ref[0].astype(F32) * sg_ref[0].astype(F32)).astype(BF16)
    x2 = x_ref[0] + g_ref[0] * _dot(a, wo_ref[...])
    y_ref[0] = _rms_rows(x2, fg_ref[...])


def _mla_out_call(o, sg, x, g, w_out, fg, tm):
    B, T, _ = x.shape
    tok = lambda: pl.BlockSpec((1, tm, D_MODEL), lambda b, i: (b, i, 0))
    return pl.pallas_call(
        _mla_out_kernel,
        grid=(B, T // tm),
        in_specs=[tok(), tok(), tok(), _mod_spec(g),
                  pl.BlockSpec(w_out.shape, lambda b, i: (0, 0)),
                  pl.BlockSpec((1, D_MODEL), lambda b, i: (0, 0))],
        out_specs=tok(),
        out_shape=jax.ShapeDtypeStruct((B, T, D_MODEL), F32),
        compiler_params=_params("parallel", "arbitrary"),
        name="mla_out",
    )(o, sg, x, g, w_out, fg)


def _swap_halves(w):
    f = QK_ROPE // 4
    return jnp.concatenate([-w[..., f:2 * f], w[..., 0:f], -w[..., 3 * f:4 * f], w[..., 2 * f:3 * f]], axis=-1)


def _rope_table(T):
    f = QK_ROPE // 4
    t = jnp.arange(T)
    row = (t // GRID_W).astype(F32)
    col = (t % GRID_W).astype(F32)
    inv = ROPE_BASE ** (-jnp.arange(f, dtype=F32) / f)
    ar = row[:, None] * inv
    ac = col[:, None] * inv
    cos = jnp.concatenate([jnp.cos(ar), jnp.cos(ar), jnp.cos(ac), jnp.cos(ac)], axis=-1)
    sin = jnp.concatenate([jnp.sin(ar), jnp.sin(ar), jnp.sin(ac), jnp.sin(ac)], axis=-1)
    return jnp.concatenate([cos, sin], axis=-1)


def kernel(x_prompt, x_sample, c, state_ret_fwd, state_ret_bwd, cache_mla_ckv, cache_mla_krope, c_ctx,
           ada_w, ada_b, norm_g, even_in_w, even_conv_w, even_out_w, odd_in_w, odd_q_norm_g,
           odd_kv_norm_g, odd_q_up_w, odd_kv_up_w, odd_out_w, final_norm_g):
    BP, SEQ, D = x_prompt.shape
    BS, TS, _ = x_sample.shape
    NP = BP * SEQ

    cvec = jnp.concatenate([c, c_ctx[None, :], jnp.zeros((ADA_ROWS - BS - 1, D), F32)], axis=0)
    mod = _ada_call(cvec, ada_w, ada_b)

    def mods(l):
        m = mod[l].reshape(ADA_ROWS, 3, 1, D)
        return ([m[BS:BS + 1, i] for i in range(3)], [m[:BS, i] for i in range(3)])

    (sh_p, sc_p, g_p), (sh_s, sc_s, g_s) = mods(0)
    ng = norm_g[0][None, :]
    w_in = even_in_w[0].astype(BF16)
    w_out = even_out_w[0].astype(BF16)
    conv_w = even_conv_w[0]
    mask, dec, cdec = _retention_tables()

    xp = x_prompt.reshape(1, NP, D)
    pe_p = _even_in_call(xp, sh_p, sc_p, ng, w_in, 512)
    pe_s = _even_in_call(x_sample, sh_s, sc_s, ng, w_in, 512)
    xp1, nsf, nsb = _even_mix_call(pe_p.reshape(BP, SEQ, EVEN_COLS), x_prompt, g_p, conv_w, w_out,
                                   mask, dec, cdec, None, None, SEQ)
    xs1, _, _ = _even_mix_call(pe_s, x_sample, g_s, conv_w, w_out, mask, dec, cdec,
                               state_ret_fwd[:, 0], state_ret_bwd[:, 0], 512)

    (sh_p, sc_p, g_p), (sh_s, sc_s, g_s) = mods(1)
    ng = norm_g[1][None, :]
    w_in = odd_in_w[0]
    nq = Q_LORA + KV_LORA
    wqkv = w_in[:, :nq].astype(BF16)
    wkr = w_in[:, nq:nq + QK_ROPE]
    wkr = jnp.concatenate([wkr, _swap_halves(wkr)], axis=-1).astype(BF16)
    wg = w_in[:, nq + QK_ROPE:].astype(BF16)
    qup = odd_q_up_w[0].reshape(Q_LORA, MLA_HEADS, QK_NOPE + QK_ROPE)
    qup = jnp.concatenate([qup, _swap_halves(qup[..., QK_NOPE:])], axis=-1)
    qup = qup.reshape(Q_LORA, MLA_HEADS * QK_PAD).astype(BF16)
    kvup = odd_kv_up_w[0].reshape(KV_LORA, MLA_HEADS, QK_NOPE + V_HEAD)
    kup = kvup[..., :QK_NOPE].reshape(KV_LORA, MLA_HEADS * QK_NOPE).astype(BF16)
    vupt = kvup[..., QK_NOPE:].reshape(KV_LORA, MLA_HEADS * V_HEAD).T.astype(BF16)
    wts = (wqkv, wkr, wg, odd_q_norm_g[0][None, :], odd_kv_norm_g[0][None, :], qup, kup, vupt)
    w_out = odd_out_w[0].astype(BF16)
    fg = final_norm_g[None, :]

    xp1f = xp1.reshape(1, NP, D)
    q_p, k_p, v_p, sg_p, ckv_p, kr_p = _mla_in_call(xp1f, sh_p, sc_p, ng, wts, None, 512, True)
    q_s, k_s, v_s, sg_s = _mla_in_call(xs1, sh_s, sc_s, ng, wts, _rope_table(TS), 512, False)
    k_c, v_c = _ctx_expand_call(cache_mla_ckv[:, 0], cache_mla_krope[:, 0], kup, vupt)

    o_p = _attn_self_call(q_p, k_p, v_p, SEQ)
    o_s = _attn_ctx_call(q_s, k_c, k_s, v_c, v_s, MXU_DIM)

    y_p = _mla_out_call(o_p, sg_p, xp1f, g_p, w_out, fg, 512)
    y_s = _mla_out_call(o_s, sg_s, xs1, g_s, w_out, fg, 512)

    return (y_p.reshape(BP, SEQ, D), y_s,
            nsf[:, None], nsb[:, None],
            ckv_p.reshape(BP, 1, SEQ, KV_LORA), kr_p.reshape(BP, 1, SEQ, QK_ROPE))
```

```python
import functools

import jax
import jax.numpy as jnp
from jax import lax
from jax.experimental import pallas as pl
from jax.experimental.pallas import tpu as pltpu

F32 = jnp.float32
BF16 = jnp.bfloat16

D_MODEL = 1024
GRID_W = 64
EPS = 1e-6
RET_HEADS = 4
RET_DK = 128
RET_DV = 128
RET_WIDTH = RET_HEADS * RET_DV
RET_CHUNK = 128
RET_BWD_OFFSET = 0.5
CONV_WIDTH = D_MODEL - RET_WIDTH
MLA_HEADS = 8
QK_NOPE = 128
QK_ROPE = 64
V_HEAD = 128
Q_LORA = 384
KV_LORA = 256
ROPE_BASE = 10000.0
QK_PAD = 256
ADA_ROWS = 16
BF16_SUBLANES = 16
F32_SUBLANES = 8
MXU_DIM = 256
LOG2_E = 1.4426950408889634
VMEM_LIMIT = 56 * 1024 * 1024


def _silu(x):
    return x * (1.0 / (1.0 + jnp.exp(-x)))


def _rms_rows(x, g):
    return x * lax.rsqrt(jnp.mean(x * x, axis=-1, keepdims=True) + EPS) * g


def _dot(a, b):
    return jnp.dot(a, b, preferred_element_type=F32)


def _dot_nt(a, b):
    return lax.dot_general(a, b, (((1,), (1,)), ((), ())), preferred_element_type=F32)


def _dot_tn(a, b):
    return lax.dot_general(a, b, (((0,), (0,)), ((), ())), preferred_element_type=F32)


def _params(*sem):
    return pltpu.CompilerParams(dimension_semantics=sem, vmem_limit_bytes=VMEM_LIMIT)


def _ada_kernel(c_ref, w_ref, b_ref, o_ref):
    s = _silu(c_ref[...]).astype(BF16)
    o_ref[0] = _dot(s, w_ref[0].astype(BF16)) + b_ref[0]


def _ada_call(cvec, ada_w, ada_b):
    depth = ada_w.shape[0]
    return pl.pallas_call(
        _ada_kernel,
        grid=(depth, 3),
        in_specs=[
            pl.BlockSpec((ADA_ROWS, D_MODEL), lambda l, j: (0, 0)),
            pl.BlockSpec((1, D_MODEL, D_MODEL), lambda l, j: (l, 0, j)),
            pl.BlockSpec((1, 1, D_MODEL), lambda l, j: (l, 0, j)),
        ],
        out_specs=pl.BlockSpec((1, ADA_ROWS, D_MODEL), lambda l, j: (l, 0, j)),
        out_shape=jax.ShapeDtypeStruct((depth, ADA_ROWS, 3 * D_MODEL), F32),
        compiler_params=_params("arbitrary", "arbitrary"),
        name="ada",
    )(cvec, ada_w, ada_b.reshape(depth, 1, 3 * D_MODEL))


def _mod_spec(mod):
    if mod.shape[0] == 1:
        return pl.BlockSpec((1, 1, D_MODEL), lambda b, *_: (0, 0, 0))
    return pl.BlockSpec((1, 1, D_MODEL), lambda b, *_: (b, 0, 0))


EVEN_COLS = 6 * RET_WIDTH


def _even_in_kernel(x_ref, sh_ref, sc_ref, ng_ref, w_ref, o_ref):
    h = _rms_rows(x_ref[0], ng_ref[...]) * (1.0 + sc_ref[0]) + sh_ref[0]
    hb = h.astype(BF16)
    W = RET_WIDTH

    def proj(j):
        return _dot(hb, w_ref[:, j * W:(j + 1) * W])

    o_ref[0, :, 0 * W:1 * W] = proj(0).astype(BF16)
    o_ref[0, :, 1 * W:2 * W] = (proj(1) * (RET_DK ** -0.5)).astype(BF16)
    o_ref[0, :, 2 * W:3 * W] = proj(2).astype(BF16)
    o_ref[0, :, 3 * W:4 * W] = _silu(proj(3)).astype(BF16)
    o_ref[0, :, 4 * W:5 * W] = (_silu(proj(7)) * proj(4)).astype(BF16)
    o_ref[0, :, 5 * W:6 * W] = (proj(5) * proj(6)).astype(BF16)


def _even_in_call(x, sh, sc, ng, w, tm):
    B, T, _ = x.shape
    return pl.pallas_call(
        _even_in_kernel,
        grid=(B, T // tm),
        in_specs=[
            pl.BlockSpec((1, tm, D_MODEL), lambda b, i: (b, i, 0)),
            _mod_spec(sh), _mod_spec(sc),
            pl.BlockSpec((1, D_MODEL), lambda b, i: (0, 0)),
            pl.BlockSpec(w.shape, lambda b, i: (0, 0)),
        ],
        out_specs=pl.BlockSpec((1, tm, EVEN_COLS), lambda b, i: (b, i, 0)),
        out_shape=jax.ShapeDtypeStruct((B, T, EVEN_COLS), BF16),
        compiler_params=_params("parallel", "arbitrary"),
        name="even_in",
    )(x, sh, sc, ng, w)


def _even_mix_kernel(k_ref, v_ref, q_ref, sga_ref, u_ref, z_ref, zp_ref, zn_ref,
                     x_ref, g_ref, cw_ref, wo_ref, mask_ref, dec_ref, cdec_ref,
                     s0f_ref, s0b_ref,
                     xo_ref, nsf_ref, nsb_ref,
                     sb_all, sf_run, sb_run, ymix,
                     *, nc, nb, zero_init):
    p = pl.program_id(1)
    j = pl.program_id(2)
    C = RET_CHUNK
    H = RET_HEADS
    dv = RET_DV

    @pl.when((p == 0) & (j == 0))
    def _():
        if zero_init:
            sf_run[...] = jnp.zeros_like(sf_run)
            sb_run[...] = jnp.zeros_like(sb_run)
        else:
            sf_run[...] = s0f_ref[0]
            sb_run[...] = s0b_ref[0]

    @pl.when(p == 0)
    def _():
        blk = nb - 1 - j
        for cc in reversed(range(nc)):
            gc = blk * nc + cc
            rows = slice(cc * C, (cc + 1) * C)
            for h in range(H):
                cols = slice(h * dv, (h + 1) * dv)
                sb_all[gc, h] = sb_run[h].astype(BF16)
                kd = (k_ref[0, rows, cols].astype(F32) * dec_ref[3, h]).astype(BF16)
                sb_run[h] = sb_run[h] * cdec_ref[1, h] + _dot_tn(kd, v_ref[0, rows, cols])

        @pl.when(j == nb - 1)
        def _():
            nsb_ref[0] = sb_run[...]

    @pl.when(p == 1)
    def _():
        for cc in range(nc):
            gc = j * nc + cc
            rows = slice(cc * C, (cc + 1) * C)
            for h in range(H):
                cols = slice(h * dv, (h + 1) * dv)
                qh = q_ref[0, rows, cols]
                kh = k_ref[0, rows, cols]
                vh = v_ref[0, rows, cols]
                att = (_dot_nt(qh, kh) * mask_ref[h]).astype(BF16)
                o = (_dot(att, vh)
                     + _dot(qh, sf_run[h].astype(BF16)) * dec_ref[0, h]
                     + _dot(qh, sb_all[gc, h]) * dec_ref[1, h])
                kd = (kh.astype(F32) * dec_ref[2, h]).astype(BF16)
                sf_run[h] = sf_run[h] * cdec_ref[0, h] + _dot_tn(kd, vh)
                on = o * lax.rsqrt(jnp.mean(o * o, axis=-1, keepdims=True) + EPS)
                ymix[rows, cols] = (on * sga_ref[0, rows, cols].astype(F32)).astype(BF16)

        z = z_ref[0].astype(F32)
        tb = z.shape[0]
        row = lax.broadcasted_iota(jnp.int32, z.shape, 0)
        prev_row = zp_ref[0, BF16_SUBLANES - 1:BF16_SUBLANES, :].astype(F32)
        next_row = zn_ref[0, 0:1, :].astype(F32)
        prev_row = jnp.where(j > 0, prev_row, 0.0)
        next_row = jnp.where(j < nb - 1, next_row, 0.0)
        z_prev = jnp.where(row == 0, prev_row, pltpu.roll(z, 1, 0))
        z_next = jnp.where(row == tb - 1, next_row, pltpu.roll(z, tb - 1, 0))
        zc = z_prev * cw_ref[0:1, :] + z * cw_ref[1:2, :] + z_next * cw_ref[2:3, :]
        ymix[:, RET_WIDTH:] = (u_ref[0].astype(F32) * zc).astype(BF16)

        y = _dot(ymix[...], wo_ref[...])
        xo_ref[0] = x_ref[0] + g_ref[0] * y

        @pl.when(j == nb - 1)
        def _():
            nsf_ref[0] = sf_run[...]


def _even_mix_call(pe, x, g, conv_w, w_out, mask, dec, cdec, s0f, s0b, tb):
    B, T, _ = x.shape
    nb = T // tb
    nc = tb // RET_CHUNK
    zero_init = s0f is None
    if zero_init:
        s0f = jnp.zeros((1, RET_HEADS, RET_DK, RET_DV), F32)
        s0b = s0f
        s0_spec = pl.BlockSpec((1, RET_HEADS, RET_DK, RET_DV), lambda b, p, j: (0, 0, 0, 0))
    else:
        s0_spec = pl.BlockSpec((1, RET_HEADS, RET_DK, RET_DV), lambda b, p, j: (b, 0, 0, 0))
    W = RET_WIDTH
    hb = tb // BF16_SUBLANES

    def kv_map(col):
        return lambda b, p, j: (b, p * j + (1 - p) * (nb - 1 - j), col)

    def fwd_map(col):
        return lambda b, p, j: (b, p * j, col)

    kernel = functools.partial(_even_mix_kernel, nc=nc, nb=nb, zero_init=zero_init)
    state_shape = jax.ShapeDtypeStruct((B, RET_HEADS, RET_DK, RET_DV), F32)
    state_spec = pl.BlockSpec((1, RET_HEADS, RET_DK, RET_DV), lambda b, p, j: (b, 0, 0, 0))
    return pl.pallas_call(
        kernel,
        grid=(B, 2, nb),
        in_specs=[
            pl.BlockSpec((1, tb, W), kv_map(1)),
            pl.BlockSpec((1, tb, W), kv_map(2)),
            pl.BlockSpec((1, tb, W), fwd_map(0)),
            pl.BlockSpec((1, tb, W), fwd_map(3)),
            pl.BlockSpec((1, tb, W), fwd_map(4)),
            pl.BlockSpec((1, tb, W), fwd_map(5)),
            pl.BlockSpec((1, BF16_SUBLANES, W),
                         lambda b, p, j: (b, jnp.maximum(p * j * hb - 1, 0), 5)),
            pl.BlockSpec((1, BF16_SUBLANES, W),
                         lambda b, p, j: (b, jnp.minimum((p * j + 1) * hb, T // BF16_SUBLANES - 1), 5)),
            pl.BlockSpec((1, tb, D_MODEL), fwd_map(0)),
            _mod_spec(g),
            pl.BlockSpec(conv_w.shape, lambda b, p, j: (0, 0)),
            pl.BlockSpec(w_out.shape, lambda b, p, j: (0, 0)),
            pl.BlockSpec(mask.shape, lambda b, p, j: (0, 0, 0)),
            pl.BlockSpec(dec.shape, lambda b, p, j: (0, 0, 0, 0)),
            pl.BlockSpec(memory_space=pltpu.SMEM),
            s0_spec, s0_spec,
        ],
        out_specs=[
            pl.BlockSpec((1, tb, D_MODEL), fwd_map(0)),
            state_spec, state_spec,
        ],
        out_shape=[jax.ShapeDtypeStruct((B, T, D_MODEL), F32), state_shape, state_shape],
        scratch_shapes=[
            pltpu.VMEM((T // RET_CHUNK, RET_HEADS, RET_DK, RET_DV), BF16),
            pltpu.VMEM((RET_HEADS, RET_DK, RET_DV), F32),
            pltpu.VMEM((RET_HEADS, RET_DK, RET_DV), F32),
            pltpu.VMEM((tb, D_MODEL), BF16),
        ],
        compiler_params=_params("parallel", "arbitrary", "arbitrary"),
        name="even_mix",
    )(pe, pe, pe, pe, pe, pe, pe, pe, x, g, conv_w, w_out, mask, dec, cdec, s0f, s0b)


def _retention_tables():
    C = RET_CHUNK
    hh = jnp.arange(RET_HEADS, dtype=F32)
    lg_f = jnp.log(1.0 - 2.0 ** (-5.0 - hh))
    lg_b = jnp.log(1.0 - 2.0 ** (-5.0 - hh - RET_BWD_OFFSET))
    idx = jnp.arange(C, dtype=F32)
    diff = idx[:, None] - idx[None, :]
    m_f = jnp.where(diff >= 0, jnp.exp(lg_f[:, None, None] * jnp.maximum(diff, 0.0)), 0.0)
    m_b = jnp.where(diff <= 0, jnp.exp(lg_b[:, None, None] * jnp.maximum(-diff, 0.0)), 0.0)
    mask = m_f + m_b
    q_f = jnp.exp(lg_f[:, None] * (idx[None, :] + 1.0))
    q_b = jnp.exp(lg_b[:, None] * (C - idx[None, :]))
    k_f = jnp.exp(lg_f[:, None] * (C - 1.0 - idx[None, :]))
    k_b = jnp.exp(lg_b[:, None] * idx[None, :])
    dec = jnp.stack([q_f, q_b, k_f, k_b])
    dec = jnp.broadcast_to(dec[..., None], dec.shape + (RET_DV,))
    cdec = jnp.stack([jnp.exp(lg_f * C), jnp.exp(lg_b * C)])
    return mask, dec, cdec


def _rope_mix(r, cs_ref, first):
    if cs_ref is not None:
        t = r * cs_ref[...]
        r = t + pltpu.roll(t, QK_ROPE, 1)
    return jnp.where(first, r, 0.0)


def _mla_in_kernel(*refs, rope, emit_cache, tq):
    (x_ref, sh_ref, sc_ref, ng_ref, wqkv_ref, wkr_ref, wg_ref, qng_ref, kvng_ref, qupt_ref,
     kup_ref, vupt_ref) = refs[:12]
    pos = 12
    cs_ref = cst_ref = None
    if rope:
        cs_ref, cst_ref = refs[pos:pos + 2]
        pos += 2
    q_ref, k_ref, v_ref, sg_ref = refs[pos:pos + 4]
    pos += 4
    if emit_cache:
        ckv_ref, kr_ref = refs[pos:pos + 2]

    h = _rms_rows(x_ref[0], ng_ref[...]) * (1.0 + sc_ref[0]) + sh_ref[0]
    hb = h.astype(BF16)
    lat = _dot(hb, wqkv_ref[...])
    kr2 = _dot(hb, wkr_ref[...])
    sg_ref[0] = _silu(_dot(hb, wg_ref[...])).astype(BF16)
    qn = _rms_rows(lat[:, :Q_LORA], qng_ref[...]).astype(BF16)
    ckv = _rms_rows(lat[:, Q_LORA:], kvng_ref[...])
    qt = _dot_nt(qupt_ref[...], qn)
    ckv_b = ckv.astype(BF16)
    kn = _dot(ckv_b, kup_ref[...])
    vt = _dot_nt(vupt_ref[...], ckv_b)

    scale = (QK_NOPE + QK_ROPE) ** -0.5 * LOG2_E
    tm = kr2.shape[0]
    first = lax.broadcasted_iota(jnp.int32, kr2.shape, 1) < QK_ROPE
    k_rot = _rope_mix(kr2, cs_ref, first).astype(BF16)
    q_pad = jnp.zeros((QK_PAD - QK_NOPE - QK_ROPE, tq), BF16)
    f = QK_ROPE // 4
    for hd in range(MLA_HEADS):
        base = hd * (QK_NOPE + QK_ROPE)
        q_nope = (qt[base:base + QK_NOPE, :] * scale).astype(BF16)
        q_rot = qt[base + QK_NOPE:base + QK_NOPE + QK_ROPE, :]
        if rope:
            partner = jnp.concatenate([q_rot[f:2 * f], q_rot[0:f], q_rot[3 * f:4 * f], q_rot[2 * f:3 * f]], axis=0)
            q_rot = q_rot * cst_ref[0:QK_ROPE, :] + partner * cst_ref[QK_ROPE:2 * QK_ROPE, :]
        q_rot = (q_rot * scale).astype(BF16)
        for j in range(tm // tq):
            cols = slice(j * tq, (j + 1) * tq)
            q_ref[0, hd, j, 0:QK_NOPE, :] = q_nope[:, cols]
            q_ref[0, hd, j, QK_NOPE:QK_NOPE + QK_ROPE, :] = q_rot[:, cols]
            q_ref[0, hd, j, QK_NOPE + QK_ROPE:QK_PAD, :] = q_pad
        k_ref[0, hd, :, 0:QK_NOPE] = kn[:, hd * QK_NOPE:(hd + 1) * QK_NOPE].astype(BF16)
        k_ref[0, hd, :, QK_NOPE:QK_PAD] = k_rot
        v_ref[0, hd] = vt[hd * V_HEAD:(hd + 1) * V_HEAD, :].astype(BF16)
    if emit_cache:
        ckv_ref[0] = ckv
        kr_ref[0] = kr2[:, :QK_ROPE]


def _mla_in_call(x, sh, sc, ng, wts, cs, tm, tq, emit_cache):
    B, T, _ = x.shape
    rope = cs is not None
    wqkv, wkr, wg, qng, kvng, qup, kup, vupt = wts

    def full(a):
        return pl.BlockSpec(a.shape, lambda b, i: (0,) * a.ndim)

    in_specs = [
        pl.BlockSpec((1, tm, D_MODEL), lambda b, i: (b, i, 0)),
        _mod_spec(sh), _mod_spec(sc), full(ng),
        full(wqkv), full(wkr), full(wg), full(qng), full(kvng), full(qup), full(kup), full(vupt),
    ]
    args = [x, sh, sc, ng, wqkv, wkr, wg, qng, kvng, qup, kup, vupt]
    if rope:
        in_specs += [pl.BlockSpec((tm, 2 * QK_ROPE), lambda b, i: (i, 0)),
                     pl.BlockSpec((2 * QK_ROPE, tm), lambda b, i: (0, i))]
        f = QK_ROPE // 4
        sign = jnp.concatenate([jnp.ones((QK_ROPE,), F32)] + [-jnp.ones((f,), F32), jnp.ones((f,), F32)] * 2)
        args += [cs, (cs * sign).T]
    out_specs = [pl.BlockSpec((1, MLA_HEADS, tm // tq, QK_PAD, tq), lambda b, i: (b, 0, i, 0, 0)),
                 pl.BlockSpec((1, MLA_HEADS, tm, QK_PAD), lambda b, i: (b, 0, i, 0)),
                 pl.BlockSpec((1, MLA_HEADS, V_HEAD, tm), lambda b, i: (b, 0, 0, i)),
                 pl.BlockSpec((1, tm, D_MODEL), lambda b, i: (b, i, 0))]
    out_shape = [jax.ShapeDtypeStruct((B, MLA_HEADS, T // tq, QK_PAD, tq), BF16),
                 jax.ShapeDtypeStruct((B, MLA_HEADS, T, QK_PAD), BF16),
                 jax.ShapeDtypeStruct((B, MLA_HEADS, V_HEAD, T), BF16),
                 jax.ShapeDtypeStruct((B, T, D_MODEL), BF16)]
    if emit_cache:
        out_specs += [pl.BlockSpec((1, tm, KV_LORA), lambda b, i: (b, i, 0)),
                      pl.BlockSpec((1, tm, QK_ROPE), lambda b, i: (b, i, 0))]
        out_shape += [jax.ShapeDtypeStruct((B, T, KV_LORA), F32),
                      jax.ShapeDtypeStruct((B, T, QK_ROPE), F32)]
    return pl.pallas_call(
        functools.partial(_mla_in_kernel, rope=rope, emit_cache=emit_cache, tq=tq),
        grid=(B, T // tm),
        in_specs=in_specs, out_specs=out_specs, out_shape=out_shape,
        compiler_params=_params("parallel", "arbitrary"),
        name="mla_in",
    )(*args)


def _ctx_expand_kernel(ckv_ref, kr_ref, kup_ref, vupt_ref, k_ref, v_ref):
    ckv_b = ckv_ref[0].astype(BF16)
    kn = _dot(ckv_b, kup_ref[...])
    vt = _dot_nt(vupt_ref[...], ckv_b)
    kr = kr_ref[0].astype(BF16)
    zeros = jnp.zeros_like(kr)
    for hd in range(MLA_HEADS):
        k_ref[0, hd, :, 0:QK_NOPE] = kn[:, hd * QK_NOPE:(hd + 1) * QK_NOPE].astype(BF16)
        k_ref[0, hd, :, QK_NOPE:QK_NOPE + QK_ROPE] = kr
        k_ref[0, hd, :, QK_NOPE + QK_ROPE:QK_PAD] = zeros
        v_ref[0, hd] = vt[hd * V_HEAD:(hd + 1) * V_HEAD, :].astype(BF16)


def _ctx_expand_call(ckv, kr, kup, vupt):
    B, L, _ = ckv.shape
    return pl.pallas_call(
        _ctx_expand_kernel,
        grid=(B,),
        in_specs=[
            pl.BlockSpec((1, L, KV_LORA), lambda b: (b, 0, 0)),
            pl.BlockSpec((1, L, QK_ROPE), lambda b: (b, 0, 0)),
            pl.BlockSpec(kup.shape, lambda b: (0, 0)),
            pl.BlockSpec(vupt.shape, lambda b: (0, 0)),
        ],
        out_specs=[pl.BlockSpec((1, MLA_HEADS, L, QK_PAD), lambda b: (b, 0, 0, 0)),
                   pl.BlockSpec((1, MLA_HEADS, V_HEAD, L), lambda b: (b, 0, 0, 0))],
        out_shape=[jax.ShapeDtypeStruct((B, MLA_HEADS, L, QK_PAD), BF16),
                   jax.ShapeDtypeStruct((B, MLA_HEADS, V_HEAD, L), BF16)],
        compiler_params=_params("parallel"),
        name="ctx_expand",
    )(ckv, kr, kup, vupt)


ATTN_UNROLL = 4


def _attn_ctx_kernel(q_ref, kc_ref, kl_ref, vc_ref, vl_ref, sg_ref, o_ref, s0_scr, s1_scr):
    nq, _, tq = q_ref.shape[2:]
    T = nq * tq
    Lc = kc_ref.shape[2]
    tiles = ([(kc_ref, vc_ref, j * MXU_DIM) for j in range(Lc // MXU_DIM)]
             + [(kl_ref, vl_ref, j * MXU_DIM) for j in range(T // MXU_DIM)])
    groups = MXU_DIM // F32_SUBLANES

    def score_tile(qb, t, s_dst):
        kref, _, off = tiles[t]
        s = _dot(kref[0, 0, off:off + MXU_DIM, :], qb)
        s_dst[t * MXU_DIM:(t + 1) * MXU_DIM, :] = s
        return jnp.max(s.reshape(groups, F32_SUBLANES, tq), axis=0)

    def step(blk, blk_next, m_cur, s_cur, s_nxt):
        q_next = q_ref[0, 0, blk_next]
        row = pl.multiple_of(blk * tq, tq)
        mrun = lacc = acc = None
        for t, (_, vref, off) in enumerate(tiles):
            part = score_tile(q_next, t, s_nxt)
            mrun = part if mrun is None else jnp.maximum(mrun, part)
            p = jnp.exp2(s_cur[t * MXU_DIM:(t + 1) * MXU_DIM, :] - m_cur)
            ps = jnp.sum(p.reshape(groups, F32_SUBLANES, tq), axis=0)
            lacc = ps if lacc is None else lacc + ps
            pv = _dot(vref[0, 0, :, off:off + MXU_DIM], p.astype(BF16))
            acc = pv if acc is None else acc + pv
        l = jnp.sum(lacc, axis=0, keepdims=True)
        gate = sg_ref[0, pl.ds(row, tq), :].astype(F32)
        o_ref[0, pl.ds(row, tq), :] = ((acc * (1.0 / l)).T * gate).astype(BF16)
        return jnp.max(mrun, axis=0, keepdims=True)

    q0 = q_ref[0, 0, 0]
    mrun = score_tile(q0, 0, s0_scr)
    for t in range(1, len(tiles)):
        mrun = jnp.maximum(mrun, score_tile(q0, t, s0_scr))
    m0 = jnp.max(mrun, axis=0, keepdims=True)

    def body(i, m):
        bufs = (s0_scr, s1_scr)
        for u in range(ATTN_UNROLL):
            blk = i * ATTN_UNROLL + u
            blk_next = jnp.minimum(blk + 1, nq - 1)
            m = step(blk, blk_next, m, bufs[u % 2], bufs[(u + 1) % 2])
        return m

    lax.fori_loop(0, nq // ATTN_UNROLL, body, m0)


def _attn_ctx_call(q, kc, kl, vc, vl, sg):
    B, H, nq, _, tq = q.shape
    T = nq * tq
    Lc = kc.shape[2]
    assert nq % ATTN_UNROLL == 0 and ATTN_UNROLL % 2 == 0
    return pl.pallas_call(
        _attn_ctx_kernel,
        scratch_shapes=[pltpu.VMEM((Lc + T, tq), F32), pltpu.VMEM((Lc + T, tq), F32)],
        grid=(B, H),
        in_specs=[
            pl.BlockSpec((1, 1, nq, QK_PAD, tq), lambda b, h: (b, h, 0, 0, 0)),
            pl.BlockSpec((1, 1, Lc, QK_PAD), lambda b, h: (b, h, 0, 0)),
            pl.BlockSpec((1, 1, T, QK_PAD), lambda b, h: (b, h, 0, 0)),
            pl.BlockSpec((1, 1, V_HEAD, Lc), lambda b, h: (b, h, 0, 0)),
            pl.BlockSpec((1, 1, V_HEAD, T), lambda b, h: (b, h, 0, 0)),
            pl.BlockSpec((1, T, V_HEAD), lambda b, h: (b, 0, h)),
        ],
        out_specs=pl.BlockSpec((1, T, V_HEAD), lambda b, h: (b, 0, h)),
        out_shape=jax.ShapeDtypeStruct((B, T, H * V_HEAD), BF16),
        compiler_params=_params("parallel", "arbitrary"),
        name="attn_ctx",
    )(q, kc, kl, vc, vl, sg)


def _attn_self_kernel(q_ref, k_ref, v_ref, sg_ref, o_ref):
    heads = range(MLA_HEADS)
    s = [_dot(k_ref[0, hd], q_ref[0, hd, 0]) for hd in heads]
    p = [jnp.exp2(a - jnp.max(a, axis=0, keepdims=True)) for a in s]
    inv = [1.0 / jnp.sum(a, axis=0, keepdims=True) for a in p]
    o = [_dot(v_ref[0, hd], p[hd].astype(BF16)) * inv[hd] for hd in heads]
    for hd in heads:
        cols = slice(hd * V_HEAD, (hd + 1) * V_HEAD)
        o_ref[0, :, cols] = (o[hd].T * sg_ref[0, :, cols].astype(F32)).astype(BF16)


def _attn_self_call(q, k, v, sg):
    _, H, nseq, _, seq = q.shape
    N = nseq * seq
    return pl.pallas_call(
        _attn_self_kernel,
        grid=(nseq,),
        in_specs=[
            pl.BlockSpec((1, H, 1, QK_PAD, seq), lambda b: (0, 0, b, 0, 0)),
            pl.BlockSpec((1, H, seq, QK_PAD), lambda b: (0, 0, b, 0)),
            pl.BlockSpec((1, H, V_HEAD, seq), lambda b: (0, 0, 0, b)),
            pl.BlockSpec((1, seq, H * V_HEAD), lambda b: (0, b, 0)),
        ],
        out_specs=pl.BlockSpec((1, seq, H * V_HEAD), lambda b: (0, b, 0)),
        out_shape=jax.ShapeDtypeStruct((1, N, H * V_HEAD), BF16),
        compiler_params=_params("parallel"),
        name="attn_self",
    )(q, k, v, sg)


def _mla_out_kernel(o_ref, x_ref, g_ref, wo_ref, fg_ref, y_ref):
    x2 = x_ref[0] + g_ref[0] * _dot(o_ref[0], wo_ref[...])
    y_ref[0] = _rms_rows(x2, fg_ref[...])


def _mla_out_call(o, x, g, w_out, fg, tm):
    B, T, _ = x.shape
    tok = lambda: pl.BlockSpec((1, tm, D_MODEL), lambda b, i: (b, i, 0))
    return pl.pallas_call(
        _mla_out_kernel,
        grid=(B, T // tm),
        in_specs=[tok(), tok(), _mod_spec(g),
                  pl.BlockSpec(w_out.shape, lambda b, i: (0, 0)),
                  pl.BlockSpec((1, D_MODEL), lambda b, i: (0, 0))],
        out_specs=tok(),
        out_shape=jax.ShapeDtypeStruct((B, T, D_MODEL), F32),
        compiler_params=_params("parallel", "arbitrary"),
        name="mla_out",
    )(o, x, g, w_out, fg)


def _swap_halves(w):
    f = QK_ROPE // 4
    return jnp.concatenate([-w[..., f:2 * f], w[..., 0:f], -w[..., 3 * f:4 * f], w[..., 2 * f:3 * f]], axis=-1)


def _rope_table(T):
    f = QK_ROPE // 4
    rows = T // GRID_W
    inv = ROPE_BASE ** (-jnp.arange(f, dtype=F32) / f)
    ar = jnp.arange(rows, dtype=F32)[:, None] * inv
    ac = jnp.arange(GRID_W, dtype=F32)[:, None] * inv

    def by_row(a):
        return jnp.broadcast_to(a[:, None, :], (rows, GRID_W, f)).reshape(T, f)

    def by_col(a):
        return jnp.broadcast_to(a[None, :, :], (rows, GRID_W, f)).reshape(T, f)

    cr, sr, cc, sc = by_row(jnp.cos(ar)), by_row(jnp.sin(ar)), by_col(jnp.cos(ac)), by_col(jnp.sin(ac))
    return jnp.concatenate([cr, cr, cc, cc, sr, sr, sc, sc], axis=-1)


def kernel(x_prompt, x_sample, c, state_ret_fwd, state_ret_bwd, cache_mla_ckv, cache_mla_krope, c_ctx,
           ada_w, ada_b, norm_g, even_in_w, even_conv_w, even_out_w, odd_in_w, odd_q_norm_g,
           odd_kv_norm_g, odd_q_up_w, odd_kv_up_w, odd_out_w, final_norm_g):
    BP, SEQ, D = x_prompt.shape
    BS, TS, _ = x_sample.shape
    NP = BP * SEQ

    cvec = jnp.concatenate([c, c_ctx[None, :], jnp.zeros((ADA_ROWS - BS - 1, D), F32)], axis=0)
    mod = _ada_call(cvec, ada_w, ada_b)

    def mods(l):
        m = mod[l].reshape(ADA_ROWS, 3, 1, D)
        return ([m[BS:BS + 1, i] for i in range(3)], [m[:BS, i] for i in range(3)])

    (sh_p, sc_p, g_p), (sh_s, sc_s, g_s) = mods(0)
    ng = norm_g[0][None, :]
    w_in = even_in_w[0].astype(BF16)
    w_out = even_out_w[0].astype(BF16)
    conv_w = even_conv_w[0]
    mask, dec, cdec = _retention_tables()

    xp = x_prompt.reshape(1, NP, D)
    pe_p = _even_in_call(xp, sh_p, sc_p, ng, w_in, 512)
    pe_s = _even_in_call(x_sample, sh_s, sc_s, ng, w_in, 512)
    xp1, nsf, nsb = _even_mix_call(pe_p.reshape(BP, SEQ, EVEN_COLS), x_prompt, g_p, conv_w, w_out,
                                   mask, dec, cdec, None, None, SEQ)
    xs1, _, _ = _even_mix_call(pe_s, x_sample, g_s, conv_w, w_out, mask, dec, cdec,
                               state_ret_fwd[:, 0], state_ret_bwd[:, 0], 512)

    (sh_p, sc_p, g_p), (sh_s, sc_s, g_s) = mods(1)
    ng = norm_g[1][None, :]
    w_in = odd_in_w[0]
    nq = Q_LORA + KV_LORA
    wqkv = w_in[:, :nq].astype(BF16)
    wkr = w_in[:, nq:nq + QK_ROPE]
    wkr = jnp.concatenate([wkr, _swap_halves(wkr)], axis=-1).astype(BF16)
    wg = w_in[:, nq + QK_ROPE:].astype(BF16)
    qup = odd_q_up_w[0].T.astype(BF16)
    kvup = odd_kv_up_w[0].reshape(KV_LORA, MLA_HEADS, QK_NOPE + V_HEAD)
    kup = kvup[..., :QK_NOPE].reshape(KV_LORA, MLA_HEADS * QK_NOPE).astype(BF16)
    vupt = kvup[..., QK_NOPE:].reshape(KV_LORA, MLA_HEADS * V_HEAD).T.astype(BF16)
    wts = (wqkv, wkr, wg, odd_q_norm_g[0][None, :], odd_kv_norm_g[0][None, :], qup, kup, vupt)
    w_out = odd_out_w[0].astype(BF16)
    fg = final_norm_g[None, :]

    xp1f = xp1.reshape(1, NP, D)
    q_p, k_p, v_p, sg_p, ckv_p, kr_p = _mla_in_call(xp1f, sh_p, sc_p, ng, wts, None, 512, SEQ, True)
    q_s, k_s, v_s, sg_s = _mla_in_call(xs1, sh_s, sc_s, ng, wts, _rope_table(TS), 512, MXU_DIM, False)
    k_c, v_c = _ctx_expand_call(cache_mla_ckv[:, 0], cache_mla_krope[:, 0], kup, vupt)

    o_p = _attn_self_call(q_p, k_p, v_p, sg_p)
    o_s = _attn_ctx_call(q_s, k_c, k_s, v_c, v_s, sg_s)

    y_p = _mla_out_call(o_p, xp1f, g_p, w_out, fg, 512)
    y_s = _mla_out_call(o_s, xs1, g_s, w_out, fg, 512)

    return (y_p.reshape(BP, SEQ, D), y_s,
            nsf[:, None], nsb[:, None],
            ckv_p.reshape(BP, 1, SEQ, KV_LORA), kr_p.reshape(BP, 1, SEQ, QK_ROPE))
```

```python
import functools

import jax
import jax.numpy as jnp
from jax import lax
from jax.experimental import pallas as pl
from jax.experimental.pallas import tpu as pltpu

F32 = jnp.float32
BF16 = jnp.bfloat16

D_MODEL = 1024
GRID_W = 64
EPS = 1e-6
RET_HEADS = 4
RET_DK = 128
RET_DV = 128
RET_WIDTH = RET_HEADS * RET_DV
RET_CHUNK = 128
RET_BWD_OFFSET = 0.5
CONV_WIDTH = D_MODEL - RET_WIDTH
MLA_HEADS = 8
QK_NOPE = 128
QK_ROPE = 64
V_HEAD = 128
Q_LORA = 384
KV_LORA = 256
ROPE_BASE = 10000.0
QK_PAD = 256
ADA_ROWS = 16
BF16_SUBLANES = 16
F32_SUBLANES = 8
MXU_DIM = 256
LOG2_E = 1.4426950408889634
VMEM_LIMIT = 56 * 1024 * 1024


def _silu(x):
    return x * (1.0 / (1.0 + jnp.exp(-x)))


def _rms_rows(x, g):
    return x * lax.rsqrt(jnp.mean(x * x, axis=-1, keepdims=True) + EPS) * g


def _dot(a, b):
    return jnp.dot(a, b, preferred_element_type=F32)


def _dot_nt(a, b):
    return lax.dot_general(a, b, (((1,), (1,)), ((), ())), preferred_element_type=F32)


def _dot_tn(a, b):
    return lax.dot_general(a, b, (((0,), (0,)), ((), ())), preferred_element_type=F32)


def _params(*sem):
    return pltpu.CompilerParams(dimension_semantics=sem, vmem_limit_bytes=VMEM_LIMIT)


def _ada_kernel(c_ref, w_ref, b_ref, o_ref):
    s = _silu(c_ref[...]).astype(BF16)
    o_ref[0] = _dot(s, w_ref[0].astype(BF16)) + b_ref[0]


def _ada_call(cvec, ada_w, ada_b):
    depth = ada_w.shape[0]
    return pl.pallas_call(
        _ada_kernel,
        grid=(depth, 3),
        in_specs=[
            pl.BlockSpec((ADA_ROWS, D_MODEL), lambda l, j: (0, 0)),
            pl.BlockSpec((1, D_MODEL, D_MODEL), lambda l, j: (l, 0, j)),
            pl.BlockSpec((1, 1, D_MODEL), lambda l, j: (l, 0, j)),
        ],
        out_specs=pl.BlockSpec((1, ADA_ROWS, D_MODEL), lambda l, j: (l, 0, j)),
        out_shape=jax.ShapeDtypeStruct((depth, ADA_ROWS, 3 * D_MODEL), F32),
        compiler_params=_params("arbitrary", "arbitrary"),
        name="ada",
    )(cvec, ada_w, ada_b.reshape(depth, 1, 3 * D_MODEL))


def _mod_spec(mod):
    if mod.shape[0] == 1:
        return pl.BlockSpec((1, 1, D_MODEL), lambda b, *_: (0, 0, 0))
    return pl.BlockSpec((1, 1, D_MODEL), lambda b, *_: (b, 0, 0))


EVEN_PLANES = 6


def _even_in_kernel(x_ref, sh_ref, sc_ref, ng_ref, w_ref, o_ref):
    h = _rms_rows(x_ref[0], ng_ref[...]) * (1.0 + sc_ref[0]) + sh_ref[0]
    hb = h.astype(BF16)
    W = RET_WIDTH

    def proj(j):
        return _dot(hb, w_ref[:, j * W:(j + 1) * W])

    o_ref[0, 0] = proj(0).astype(BF16)
    o_ref[1, 0] = (proj(1) * (RET_DK ** -0.5)).astype(BF16)
    o_ref[2, 0] = proj(2).astype(BF16)
    o_ref[3, 0] = _silu(proj(3)).astype(BF16)
    o_ref[4, 0] = (_silu(proj(7)) * proj(4)).astype(BF16)
    o_ref[5, 0] = (proj(5) * proj(6)).astype(BF16)


def _even_in_call(x, sh, sc, ng, w, tm):
    B, T, _ = x.shape
    return pl.pallas_call(
        _even_in_kernel,
        grid=(B, T // tm),
        in_specs=[
            pl.BlockSpec((1, tm, D_MODEL), lambda b, i: (b, i, 0)),
            _mod_spec(sh), _mod_spec(sc),
            pl.BlockSpec((1, D_MODEL), lambda b, i: (0, 0)),
            pl.BlockSpec(w.shape, lambda b, i: (0, 0)),
        ],
        out_specs=pl.BlockSpec((EVEN_PLANES, 1, tm, RET_WIDTH), lambda b, i: (0, b, i, 0)),
        out_shape=jax.ShapeDtypeStruct((EVEN_PLANES, B, T, RET_WIDTH), BF16),
        compiler_params=_params("parallel", "arbitrary"),
        name="even_in",
    )(x, sh, sc, ng, w)


def _even_mix_kernel(k_ref, v_ref, q_ref, sga_ref, u_ref, z_ref, zp_ref, zn_ref,
                     x_ref, g_ref, cw_ref, wo_ref, mask_ref, dec_ref, cdec_ref,
                     s0f_ref, s0b_ref,
                     xo_ref, nsf_ref, nsb_ref,
                     sb_all, sf_run, sb_run, ymix,
                     *, nc, nb, zero_init):
    p = pl.program_id(1)
    j = pl.program_id(2)
    C = RET_CHUNK
    H = RET_HEADS
    dv = RET_DV

    @pl.when((p == 0) & (j == 0))
    def _():
        if zero_init:
            sf_run[...] = jnp.zeros_like(sf_run)
            sb_run[...] = jnp.zeros_like(sb_run)
        else:
            sf_run[...] = s0f_ref[0]
            sb_run[...] = s0b_ref[0]

    @pl.when(p == 0)
    def _():
        blk = nb - 1 - j
        for cc in reversed(range(nc)):
            gc = blk * nc + cc
            rows = slice(cc * C, (cc + 1) * C)
            for h in range(H):
                cols = slice(h * dv, (h + 1) * dv)
                sb_all[gc, h] = sb_run[h].astype(BF16)
                kd = (k_ref[0, rows, cols].astype(F32) * dec_ref[3, h]).astype(BF16)
                sb_run[h] = sb_run[h] * cdec_ref[1, h] + _dot_tn(kd, v_ref[0, rows, cols])

        @pl.when(j == nb - 1)
        def _():
            nsb_ref[0] = sb_run[...]

    @pl.when(p == 1)
    def _():
        items = [(cc, h) for cc in range(nc) for h in range(H)]

        def tile(cc, h):
            return slice(cc * C, (cc + 1) * C), slice(h * dv, (h + 1) * dv)

        def scores(cc, h):
            rows, cols = tile(cc, h)
            return _dot_nt(q_ref[0, rows, cols], k_ref[0, rows, cols])

        s_next = scores(*items[0])
        for i, (cc, h) in enumerate(items):
            s = s_next
            if i + 1 < len(items):
                s_next = scores(*items[i + 1])
            gc = j * nc + cc
            rows, cols = tile(cc, h)
            qh = q_ref[0, rows, cols]
            kh = k_ref[0, rows, cols]
            vh = v_ref[0, rows, cols]
            inter = (_dot(qh, sf_run[h].astype(BF16)) * dec_ref[0, h]
                     + _dot(qh, sb_all[gc, h]) * dec_ref[1, h])
            kd = (kh.astype(F32) * dec_ref[2, h]).astype(BF16)
            sf_run[h] = sf_run[h] * cdec_ref[0, h] + _dot_tn(kd, vh)
            att = (s * mask_ref[h]).astype(BF16)
            o = _dot(att, vh) + inter
            on = o * lax.rsqrt(jnp.mean(o * o, axis=-1, keepdims=True) + EPS)
            ymix[rows, cols] = (on * sga_ref[0, rows, cols].astype(F32)).astype(BF16)

        z = z_ref[0].astype(F32)
        tb = z.shape[0]
        row = lax.broadcasted_iota(jnp.int32, z.shape, 0)
        prev_row = zp_ref[0, BF16_SUBLANES - 1:BF16_SUBLANES, :].astype(F32)
        next_row = zn_ref[0, 0:1, :].astype(F32)
        prev_row = jnp.where(j > 0, prev_row, 0.0)
        next_row = jnp.where(j < nb - 1, next_row, 0.0)
        z_prev = jnp.where(row == 0, prev_row, pltpu.roll(z, 1, 0))
        z_next = jnp.where(row == tb - 1, next_row, pltpu.roll(z, tb - 1, 0))
        zc = z_prev * cw_ref[0:1, :] + z * cw_ref[1:2, :] + z_next * cw_ref[2:3, :]
        ymix[:, RET_WIDTH:] = (u_ref[0].astype(F32) * zc).astype(BF16)

        y = _dot(ymix[...], wo_ref[...])
        xo_ref[0] = x_ref[0] + g_ref[0] * y

        @pl.when(j == nb - 1)
        def _():
            nsf_ref[0] = sf_run[...]


def _even_mix_call(pe, x, g, conv_w, w_out, mask, dec, cdec, s0f, s0b, tb):
    B, T, _ = x.shape
    nb = T // tb
    nc = tb // RET_CHUNK
    zero_init = s0f is None
    if zero_init:
        s0f = jnp.zeros((1, RET_HEADS, RET_DK, RET_DV), F32)
        s0b = s0f
        s0_spec = pl.BlockSpec((1, RET_HEADS, RET_DK, RET_DV), lambda b, p, j: (0, 0, 0, 0))
    else:
        s0_spec = pl.BlockSpec((1, RET_HEADS, RET_DK, RET_DV), lambda b, p, j: (b, 0, 0, 0))
    W = RET_WIDTH
    hb = tb // BF16_SUBLANES

    def plane(k, rows, row_map):
        return pl.BlockSpec((None, 1, rows, W), lambda b, p, j: (k, b, row_map(p, j), 0))

    def bwd_blk(p, j):
        return p * j + (1 - p) * (nb - 1 - j)

    def fwd_blk(p, j):
        return p * j

    def fwd_map(col):
        return lambda b, p, j: (b, p * j, col)

    kernel = functools.partial(_even_mix_kernel, nc=nc, nb=nb, zero_init=zero_init)
    state_shape = jax.ShapeDtypeStruct((B, RET_HEADS, RET_DK, RET_DV), F32)
    state_spec = pl.BlockSpec((1, RET_HEADS, RET_DK, RET_DV), lambda b, p, j: (b, 0, 0, 0))
    return pl.pallas_call(
        kernel,
        grid=(B, 2, nb),
        in_specs=[
            plane(1, tb, bwd_blk),
            plane(2, tb, bwd_blk),
            plane(0, tb, fwd_blk),
            plane(3, tb, fwd_blk),
            plane(4, tb, fwd_blk),
            plane(5, tb, fwd_blk),
            plane(5, BF16_SUBLANES, lambda p, j: jnp.maximum(p * j * hb - 1, 0)),
            plane(5, BF16_SUBLANES, lambda p, j: jnp.minimum((p * j + 1) * hb, T // BF16_SUBLANES - 1)),
            pl.BlockSpec((1, tb, D_MODEL), fwd_map(0)),
            _mod_spec(g),
            pl.BlockSpec(conv_w.shape, lambda b, p, j: (0, 0)),
            pl.BlockSpec(w_out.shape, lambda b, p, j: (0, 0)),
            pl.BlockSpec(mask.shape, lambda b, p, j: (0, 0, 0)),
            pl.BlockSpec(dec.shape, lambda b, p, j: (0, 0, 0, 0)),
            pl.BlockSpec(memory_space=pltpu.SMEM),
            s0_spec, s0_spec,
        ],
        out_specs=[
            pl.BlockSpec((1, tb, D_MODEL), fwd_map(0)),
            state_spec, state_spec,
        ],
        out_shape=[jax.ShapeDtypeStruct((B, T, D_MODEL), F32), state_shape, state_shape],
        scratch_shapes=[
            pltpu.VMEM((T // RET_CHUNK, RET_HEADS, RET_DK, RET_DV), BF16),
            pltpu.VMEM((RET_HEADS, RET_DK, RET_DV), F32),
            pltpu.VMEM((RET_HEADS, RET_DK, RET_DV), F32),
            pltpu.VMEM((tb, D_MODEL), BF16),
        ],
        compiler_params=_params("parallel", "arbitrary", "arbitrary"),
        name="even_mix",
    )(pe, pe, pe, pe, pe, pe, pe, pe, x, g, conv_w, w_out, mask, dec, cdec, s0f, s0b)


def _retention_tables():
    C = RET_CHUNK
    hh = jnp.arange(RET_HEADS, dtype=F32)
    lg_f = jnp.log(1.0 - 2.0 ** (-5.0 - hh))
    lg_b = jnp.log(1.0 - 2.0 ** (-5.0 - hh - RET_BWD_OFFSET))
    idx = jnp.arange(C, dtype=F32)
    diff = idx[:, None] - idx[None, :]
    m_f = jnp.where(diff >= 0, jnp.exp(lg_f[:, None, None] * jnp.maximum(diff, 0.0)), 0.0)
    m_b = jnp.where(diff <= 0, jnp.exp(lg_b[:, None, None] * jnp.maximum(-diff, 0.0)), 0.0)
    mask = m_f + m_b
    q_f = jnp.exp(lg_f[:, None] * (idx[None, :] + 1.0))
    q_b = jnp.exp(lg_b[:, None] * (C - idx[None, :]))
    k_f = jnp.exp(lg_f[:, None] * (C - 1.0 - idx[None, :]))
    k_b = jnp.exp(lg_b[:, None] * idx[None, :])
    dec = jnp.stack([q_f, q_b, k_f, k_b])
    dec = jnp.broadcast_to(dec[..., None], dec.shape + (RET_DV,))
    cdec = jnp.stack([jnp.exp(lg_f * C), jnp.exp(lg_b * C)])
    return mask, dec, cdec


def _rope_mix(r, cs_ref, first):
    if cs_ref is not None:
        t = r * cs_ref[...]
        r = t + pltpu.roll(t, QK_ROPE, 1)
    return jnp.where(first, r, 0.0)


def _mla_in_kernel(*refs, rope, emit_cache, tq):
    (x_ref, sh_ref, sc_ref, ng_ref, wlat_ref, wg_ref, qng_ref, kvng_ref, qupt_ref,
     kup_ref, vupt_ref) = refs[:11]
    pos = 11
    cs_ref = cst_ref = None
    if rope:
        cs_ref, cst_ref = refs[pos:pos + 2]
        pos += 2
    q_ref, k_ref, v_ref, sg_ref = refs[pos:pos + 4]
    pos += 4
    if emit_cache:
        ckv_ref, kr_ref = refs[pos:pos + 2]

    h = _rms_rows(x_ref[0], ng_ref[...]) * (1.0 + sc_ref[0]) + sh_ref[0]
    hb = h.astype(BF16)
    lat = _dot(hb, wlat_ref[...])
    kr2 = lat[:, Q_LORA + KV_LORA:]
    sg_ref[0] = _silu(_dot(hb, wg_ref[...])).astype(BF16)
    qn = _rms_rows(lat[:, :Q_LORA], qng_ref[...]).astype(BF16)
    ckv = _rms_rows(lat[:, Q_LORA:Q_LORA + KV_LORA], kvng_ref[...])
    qt = _dot_nt(qupt_ref[...], qn)
    ckv_b = ckv.astype(BF16)
    kn = _dot(ckv_b, kup_ref[...])
    vt = _dot_nt(vupt_ref[...], ckv_b)

    scale = (QK_NOPE + QK_ROPE) ** -0.5 * LOG2_E
    tm = kr2.shape[0]
    first = lax.broadcasted_iota(jnp.int32, kr2.shape, 1) < QK_ROPE
    k_rot = _rope_mix(kr2, cs_ref, first).astype(BF16)
    q_pad = jnp.zeros((QK_PAD - QK_NOPE - QK_ROPE, tq), BF16)
    f = QK_ROPE // 4
    for hd in range(MLA_HEADS):
        base = hd * (QK_NOPE + QK_ROPE)
        q_nope = (qt[base:base + QK_NOPE, :] * scale).astype(BF16)
        q_rot = qt[base + QK_NOPE:base + QK_NOPE + QK_ROPE, :]
        if rope:
            partner = jnp.concatenate([q_rot[f:2 * f], q_rot[0:f], q_rot[3 * f:4 * f], q_rot[2 * f:3 * f]], axis=0)
            q_rot = q_rot * cst_ref[0:QK_ROPE, :] + partner * cst_ref[QK_ROPE:2 * QK_ROPE, :]
        q_rot = (q_rot * scale).astype(BF16)
        for j in range(tm // tq):
            cols = slice(j * tq, (j + 1) * tq)
            q_ref[0, hd, j, 0:QK_NOPE, :] = q_nope[:, cols]
            q_ref[0, hd, j, QK_NOPE:QK_NOPE + QK_ROPE, :] = q_rot[:, cols]
            q_ref[0, hd, j, QK_NOPE + QK_ROPE:QK_PAD, :] = q_pad
        k_ref[0, hd, :, 0:QK_NOPE] = kn[:, hd * QK_NOPE:(hd + 1) * QK_NOPE].astype(BF16)
        k_ref[0, hd, :, QK_NOPE:QK_PAD] = k_rot
        v_ref[0, hd] = vt[hd * V_HEAD:(hd + 1) * V_HEAD, :].astype(BF16)
    if emit_cache:
        ckv_ref[0] = ckv
        kr_ref[0] = kr2[:, :QK_ROPE]


def _mla_in_call(x, sh, sc, ng, wts, cs, tm, tq, emit_cache):
    B, T, _ = x.shape
    rope = cs is not None
    wlat, wg, qng, kvng, qup, kup, vupt = wts

    def full(a):
        return pl.BlockSpec(a.shape, lambda b, i: (0,) * a.ndim)

    in_specs = [
        pl.BlockSpec((1, tm, D_MODEL), lambda b, i: (b, i, 0)),
        _mod_spec(sh), _mod_spec(sc), full(ng),
        full(wlat), full(wg), full(qng), full(kvng), full(qup), full(kup), full(vupt),
    ]
    args = [x, sh, sc, ng, wlat, wg, qng, kvng, qup, kup, vupt]
    if rope:
        in_specs += [pl.BlockSpec((tm, 2 * QK_ROPE), lambda b, i: (i, 0)),
                     pl.BlockSpec((2 * QK_ROPE, tm), lambda b, i: (0, i))]
        f = QK_ROPE // 4
        sign = jnp.concatenate([jnp.ones((QK_ROPE,), F32)] + [-jnp.ones((f,), F32), jnp.ones((f,), F32)] * 2)
        args += [cs, (cs * sign).T]
    out_specs = [pl.BlockSpec((1, MLA_HEADS, tm // tq, QK_PAD, tq), lambda b, i: (b, 0, i, 0, 0)),
                 pl.BlockSpec((1, MLA_HEADS, tm, QK_PAD), lambda b, i: (b, 0, i, 0)),
                 pl.BlockSpec((1, MLA_HEADS, V_HEAD, tm), lambda b, i: (b, 0, 0, i)),
                 pl.BlockSpec((1, tm, D_MODEL), lambda b, i: (b, i, 0))]
    out_shape = [jax.ShapeDtypeStruct((B, MLA_HEADS, T // tq, QK_PAD, tq), BF16),
                 jax.ShapeDtypeStruct((B, MLA_HEADS, T, QK_PAD), BF16),
                 jax.ShapeDtypeStruct((B, MLA_HEADS, V_HEAD, T), BF16),
                 jax.ShapeDtypeStruct((B, T, D_MODEL), BF16)]
    if emit_cache:
        out_specs += [pl.BlockSpec((1, tm, KV_LORA), lambda b, i: (b, i, 0)),
                      pl.BlockSpec((1, tm, QK_ROPE), lambda b, i: (b, i, 0))]
        out_shape += [jax.ShapeDtypeStruct((B, T, KV_LORA), F32),
                      jax.ShapeDtypeStruct((B, T, QK_ROPE), F32)]
    return pl.pallas_call(
        functools.partial(_mla_in_kernel, rope=rope, emit_cache=emit_cache, tq=tq),
        grid=(B, T // tm),
        in_specs=in_specs, out_specs=out_specs, out_shape=out_shape,
        compiler_params=_params("parallel", "arbitrary"),
        name="mla_in",
    )(*args)


def _ctx_expand_kernel(ckv_ref, kr_ref, kup_ref, vupt_ref, k_ref, v_ref):
    ckv_b = ckv_ref[0].astype(BF16)
    kn = _dot(ckv_b, kup_ref[...])
    vt = _dot_nt(vupt_ref[...], ckv_b)
    kr = kr_ref[0].astype(BF16)
    zeros = jnp.zeros_like(kr)
    for hd in range(MLA_HEADS):
        k_ref[0, hd, :, 0:QK_NOPE] = kn[:, hd * QK_NOPE:(hd + 1) * QK_NOPE].astype(BF16)
        k_ref[0, hd, :, QK_NOPE:QK_NOPE + QK_ROPE] = kr
        k_ref[0, hd, :, QK_NOPE + QK_ROPE:QK_PAD] = zeros
        v_ref[0, hd] = vt[hd * V_HEAD:(hd + 1) * V_HEAD, :].astype(BF16)


def _ctx_expand_call(ckv, kr, kup, vupt):
    B, L, _ = ckv.shape
    return pl.pallas_call(
        _ctx_expand_kernel,
        grid=(B,),
        in_specs=[
            pl.BlockSpec((1, L, KV_LORA), lambda b: (b, 0, 0)),
            pl.BlockSpec((1, L, QK_ROPE), lambda b: (b, 0, 0)),
            pl.BlockSpec(kup.shape, lambda b: (0, 0)),
            pl.BlockSpec(vupt.shape, lambda b: (0, 0)),
        ],
        out_specs=[pl.BlockSpec((1, MLA_HEADS, L, QK_PAD), lambda b: (b, 0, 0, 0)),
                   pl.BlockSpec((1, MLA_HEADS, V_HEAD, L), lambda b: (b, 0, 0, 0))],
        out_shape=[jax.ShapeDtypeStruct((B, MLA_HEADS, L, QK_PAD), BF16),
                   jax.ShapeDtypeStruct((B, MLA_HEADS, V_HEAD, L), BF16)],
        compiler_params=_params("parallel"),
        name="ctx_expand",
    )(ckv, kr, kup, vupt)


ATTN_UNROLL = 4
ATTN_LEAD = 6


def _attn_ctx_kernel(q_ref, kc_ref, kl_ref, vc_ref, vl_ref, sg_ref, o_ref, s0_scr, s1_scr):
    nq, _, tq = q_ref.shape[2:]
    T = nq * tq
    Lc = kc_ref.shape[2]
    tiles = ([(kc_ref, vc_ref, j * MXU_DIM) for j in range(Lc // MXU_DIM)]
             + [(kl_ref, vl_ref, j * MXU_DIM) for j in range(T // MXU_DIM)])
    groups = MXU_DIM // F32_SUBLANES

    def score_tile(qb, t, s_dst):
        kref, _, off = tiles[t]
        s = _dot(kref[0, 0, off:off + MXU_DIM, :], qb)
        s_dst[t * MXU_DIM:(t + 1) * MXU_DIM, :] = s

    def tile_max(s_buf, t):
        s = s_buf[t * MXU_DIM:(t + 1) * MXU_DIM, :]
        return jnp.max(s.reshape(groups, F32_SUBLANES, tq), axis=0)

    def step(blk, blk_next, m_cur, s_cur, s_nxt):
        q_next = q_ref[0, 0, blk_next]
        row = pl.multiple_of(blk * tq, tq)
        for t in range(ATTN_LEAD):
            score_tile(q_next, t, s_nxt)
        acc = lacc = None
        parts = []
        for t, (_, vref, off) in enumerate(tiles):
            p = jnp.exp2(s_cur[t * MXU_DIM:(t + 1) * MXU_DIM, :] - m_cur)
            ps = jnp.sum(p.reshape(groups, F32_SUBLANES, tq), axis=0)
            lacc = ps if lacc is None else lacc + ps
            pv = _dot(vref[0, 0, :, off:off + MXU_DIM], p.astype(BF16))
            acc = pv if acc is None else acc + pv
            if t + ATTN_LEAD < len(tiles):
                score_tile(q_next, t + ATTN_LEAD, s_nxt)
            parts.append(tile_max(s_nxt, t))
        mrun = functools.reduce(jnp.maximum, parts)
        l = jnp.sum(lacc, axis=0, keepdims=True)
        gate = sg_ref[0, pl.ds(row, tq), :].astype(F32)
        o_ref[0, pl.ds(row, tq), :] = ((acc * (1.0 / l)).T * gate).astype(BF16)
        return jnp.max(mrun, axis=0, keepdims=True)

    q0 = q_ref[0, 0, 0]
    for t in range(len(tiles)):
        score_tile(q0, t, s0_scr)
    mrun = functools.reduce(jnp.maximum, [tile_max(s0_scr, t) for t in range(len(tiles))])
    m0 = jnp.max(mrun, axis=0, keepdims=True)

    def body(i, m):
        bufs = (s0_scr, s1_scr)
        for u in range(ATTN_UNROLL):
            blk = i * ATTN_UNROLL + u
            blk_next = jnp.minimum(blk + 1, nq - 1)
            m = step(blk, blk_next, m, bufs[u % 2], bufs[(u + 1) % 2])
        return m

    lax.fori_loop(0, nq // ATTN_UNROLL, body, m0)


def _attn_ctx_call(q, kc, kl, vc, vl, sg):
    B, H, nq, _, tq = q.shape
    T = nq * tq
    Lc = kc.shape[2]
    assert nq % ATTN_UNROLL == 0 and ATTN_UNROLL % 2 == 0
    return pl.pallas_call(
        _attn_ctx_kernel,
        scratch_shapes=[pltpu.VMEM((Lc + T, tq), F32), pltpu.VMEM((Lc + T, tq), F32)],
        grid=(B, H),
        in_specs=[
            pl.BlockSpec((1, 1, nq, QK_PAD, tq), lambda b, h: (b, h, 0, 0, 0)),
            pl.BlockSpec((1, 1, Lc, QK_PAD), lambda b, h: (b, h, 0, 0)),
            pl.BlockSpec((1, 1, T, QK_PAD), lambda b, h: (b, h, 0, 0)),
            pl.BlockSpec((1, 1, V_HEAD, Lc), lambda b, h: (b, h, 0, 0)),
            pl.BlockSpec((1, 1, V_HEAD, T), lambda b, h: (b, h, 0, 0)),
            pl.BlockSpec((1, T, V_HEAD), lambda b, h: (b, 0, h)),
        ],
        out_specs=pl.BlockSpec((1, T, V_HEAD), lambda b, h: (b, 0, h)),
        out_shape=jax.ShapeDtypeStruct((B, T, H * V_HEAD), BF16),
        compiler_params=_params("parallel", "arbitrary"),
        name="attn_ctx",
    )(q, kc, kl, vc, vl, sg)


def _attn_self_kernel(q_ref, k_ref, v_ref, sg_ref, o_ref):
    heads = range(MLA_HEADS)
    s = [_dot(k_ref[0, hd], q_ref[0, hd, 0]) for hd in heads]
    p = [jnp.exp2(a - jnp.max(a, axis=0, keepdims=True)) for a in s]
    inv = [1.0 / jnp.sum(a, axis=0, keepdims=True) for a in p]
    o = [_dot(v_ref[0, hd], p[hd].astype(BF16)) * inv[hd] for hd in heads]
    for hd in heads:
        cols = slice(hd * V_HEAD, (hd + 1) * V_HEAD)
        o_ref[0, :, cols] = (o[hd].T * sg_ref[0, :, cols].astype(F32)).astype(BF16)


def _attn_self_call(q, k, v, sg):
    _, H, nseq, _, seq = q.shape
    N = nseq * seq
    return pl.pallas_call(
        _attn_self_kernel,
        grid=(nseq,),
        in_specs=[
            pl.BlockSpec((1, H, 1, QK_PAD, seq), lambda b: (0, 0, b, 0, 0)),
            pl.BlockSpec((1, H, seq, QK_PAD), lambda b: (0, 0, b, 0)),
            pl.BlockSpec((1, H, V_HEAD, seq), lambda b: (0, 0, 0, b)),
            pl.BlockSpec((1, seq, H * V_HEAD), lambda b: (0, b, 0)),
        ],
        out_specs=pl.BlockSpec((1, seq, H * V_HEAD), lambda b: (0, b, 0)),
        out_shape=jax.ShapeDtypeStruct((1, N, H * V_HEAD), BF16),
        compiler_params=_params("parallel"),
        name="attn_self",
    )(q, k, v, sg)


def _mla_out_kernel(o_ref, x_ref, g_ref, wo_ref, fg_ref, y_ref):
    x2 = x_ref[0] + g_ref[0] * _dot(o_ref[0], wo_ref[...])
    y_ref[0] = _rms_rows(x2, fg_ref[...])


def _mla_out_call(o, x, g, w_out, fg, tm):
    B, T, _ = x.shape
    tok = lambda: pl.BlockSpec((1, tm, D_MODEL), lambda b, i: (b, i, 0))
    return pl.pallas_call(
        _mla_out_kernel,
        grid=(B, T // tm),
        in_specs=[tok(), tok(), _mod_spec(g),
                  pl.BlockSpec(w_out.shape, lambda b, i: (0, 0)),
                  pl.BlockSpec((1, D_MODEL), lambda b, i: (0, 0))],
        out_specs=tok(),
        out_shape=jax.ShapeDtypeStruct((B, T, D_MODEL), F32),
        compiler_params=_params("parallel", "arbitrary"),
        name="mla_out",
    )(o, x, g, w_out, fg)


def _swap_halves(w):
    f = QK_ROPE // 4
    return jnp.concatenate([-w[..., f:2 * f], w[..., 0:f], -w[..., 3 * f:4 * f], w[..., 2 * f:3 * f]], axis=-1)


def _rope_table(T):
    f = QK_ROPE // 4
    rows = T // GRID_W
    inv = ROPE_BASE ** (-jnp.arange(f, dtype=F32) / f)
    ar = jnp.arange(rows, dtype=F32)[:, None] * inv
    ac = jnp.arange(GRID_W, dtype=F32)[:, None] * inv

    def by_row(a):
        return jnp.broadcast_to(a[:, None, :], (rows, GRID_W, f)).reshape(T, f)

    def by_col(a):
        return jnp.broadcast_to(a[None, :, :], (rows, GRID_W, f)).reshape(T, f)

    cr, sr, cc, sc = by_row(jnp.cos(ar)), by_row(jnp.sin(ar)), by_col(jnp.cos(ac)), by_col(jnp.sin(ac))
    return jnp.concatenate([cr, cr, cc, cc, sr, sr, sc, sc], axis=-1)


def kernel(x_prompt, x_sample, c, state_ret_fwd, state_ret_bwd, cache_mla_ckv, cache_mla_krope, c_ctx,
           ada_w, ada_b, norm_g, even_in_w, even_conv_w, even_out_w, odd_in_w, odd_q_norm_g,
           odd_kv_norm_g, odd_q_up_w, odd_kv_up_w, odd_out_w, final_norm_g):
    BP, SEQ, D = x_prompt.shape
    BS, TS, _ = x_sample.shape
    NP = BP * SEQ

    cvec = jnp.concatenate([c, c_ctx[None, :], jnp.zeros((ADA_ROWS - BS - 1, D), F32)], axis=0)
    mod = _ada_call(cvec, ada_w, ada_b)

    def mods(l):
        m = mod[l].reshape(ADA_ROWS, 3, 1, D)
        return ([m[BS:BS + 1, i] for i in range(3)], [m[:BS, i] for i in range(3)])

    (sh_p, sc_p, g_p), (sh_s, sc_s, g_s) = mods(0)
    ng = norm_g[0][None, :]
    w_in = even_in_w[0].astype(BF16)
    w_out = even_out_w[0].astype(BF16)
    conv_w = even_conv_w[0]
    mask, dec, cdec = _retention_tables()

    xp = x_prompt.reshape(1, NP, D)
    pe_p = _even_in_call(xp, sh_p, sc_p, ng, w_in, 512)
    pe_s = _even_in_call(x_sample, sh_s, sc_s, ng, w_in, 512)
    xp1, nsf, nsb = _even_mix_call(pe_p.reshape(EVEN_PLANES, BP, SEQ, RET_WIDTH), x_prompt, g_p, conv_w, w_out,
                                   mask, dec, cdec, None, None, SEQ)
    xs1, _, _ = _even_mix_call(pe_s, x_sample, g_s, conv_w, w_out, mask, dec, cdec,
                               state_ret_fwd[:, 0], state_ret_bwd[:, 0], 512)

    (sh_p, sc_p, g_p), (sh_s, sc_s, g_s) = mods(1)
    ng = norm_g[1][None, :]
    w_in = odd_in_w[0]
    nq = Q_LORA + KV_LORA
    wkr = w_in[:, nq:nq + QK_ROPE]
    wlat = jnp.concatenate([w_in[:, :nq + QK_ROPE], _swap_halves(wkr)], axis=-1).astype(BF16)
    wg = w_in[:, nq + QK_ROPE:].astype(BF16)
    qup = odd_q_up_w[0].T.astype(BF16)
    kvup = odd_kv_up_w[0].reshape(KV_LORA, MLA_HEADS, QK_NOPE + V_HEAD)
    kup = kvup[..., :QK_NOPE].reshape(KV_LORA, MLA_HEADS * QK_NOPE).astype(BF16)
    vupt = kvup[..., QK_NOPE:].reshape(KV_LORA, MLA_HEADS * V_HEAD).T.astype(BF16)
    wts = (wlat, wg, odd_q_norm_g[0][None, :], odd_kv_norm_g[0][None, :], qup, kup, vupt)
    w_out = odd_out_w[0].astype(BF16)
    fg = final_norm_g[None, :]

    xp1f = xp1.reshape(1, NP, D)
    q_p, k_p, v_p, sg_p, ckv_p, kr_p = _mla_in_call(xp1f, sh_p, sc_p, ng, wts, None, 512, SEQ, True)
    q_s, k_s, v_s, sg_s = _mla_in_call(xs1, sh_s, sc_s, ng, wts, _rope_table(TS), 512, MXU_DIM, False)
    k_c, v_c = _ctx_expand_call(cache_mla_ckv[:, 0], cache_mla_krope[:, 0], kup, vupt)

    o_p = _attn_self_call(q_p, k_p, v_p, sg_p)
    o_s = _attn_ctx_call(q_s, k_c, k_s, v_c, v_s, sg_s)

    y_p = _mla_out_call(o_p, xp1f, g_p, w_out, fg, 512)
    y_s = _mla_out_call(o_s, xs1, g_s, w_out, fg, 512)

    return (y_p.reshape(BP, SEQ, D), y_s,
            nsf[:, None], nsb[:, None],
            ckv_p.reshape(BP, 1, SEQ, KV_LORA), kr_p.reshape(BP, 1, SEQ, QK_ROPE))
```

```python
import functools

import jax
import jax.numpy as jnp
from jax import lax
from jax.experimental import pallas as pl
from jax.experimental.pallas import tpu as pltpu

F32 = jnp.float32
BF16 = jnp.bfloat16

D_MODEL = 1024
GRID_W = 64
EPS = 1e-6
RET_HEADS = 4
RET_DK = 128
RET_DV = 128
RET_WIDTH = RET_HEADS * RET_DV
RET_CHUNK = 128
RET_BWD_OFFSET = 0.5
CONV_WIDTH = D_MODEL - RET_WIDTH
MLA_HEADS = 8
QK_NOPE = 128
QK_ROPE = 64
V_HEAD = 128
Q_LORA = 384
KV_LORA = 256
ROPE_BASE = 10000.0
QK_PAD = 256
ADA_ROWS = 16
BF16_SUBLANES = 16
F32_SUBLANES = 8
MXU_DIM = 256
LOG2_E = 1.4426950408889634
VMEM_LIMIT = 56 * 1024 * 1024


def _silu(x):
    return x * (1.0 / (1.0 + jnp.exp(-x)))


def _rms_rows(x, g):
    return x * lax.rsqrt(jnp.mean(x * x, axis=-1, keepdims=True) + EPS) * g


def _dot(a, b):
    return jnp.dot(a, b, preferred_element_type=F32)


def _dot_nt(a, b):
    return lax.dot_general(a, b, (((1,), (1,)), ((), ())), preferred_element_type=F32)


def _dot_tn(a, b):
    return lax.dot_general(a, b, (((0,), (0,)), ((), ())), preferred_element_type=F32)


def _params(*sem):
    return pltpu.CompilerParams(dimension_semantics=sem, vmem_limit_bytes=VMEM_LIMIT)


def _ada_kernel(c_ref, w_ref, b_ref, o_ref):
    s = _silu(c_ref[...]).astype(BF16)
    o_ref[0] = _dot(s, w_ref[0].astype(BF16)) + b_ref[0]


def _ada_call(cvec, ada_w, ada_b):
    depth = ada_w.shape[0]
    return pl.pallas_call(
        _ada_kernel,
        grid=(depth, 3),
        in_specs=[
            pl.BlockSpec((ADA_ROWS, D_MODEL), lambda l, j: (0, 0)),
            pl.BlockSpec((1, D_MODEL, D_MODEL), lambda l, j: (l, 0, j)),
            pl.BlockSpec((1, 1, D_MODEL), lambda l, j: (l, 0, j)),
        ],
        out_specs=pl.BlockSpec((1, ADA_ROWS, D_MODEL), lambda l, j: (l, 0, j)),
        out_shape=jax.ShapeDtypeStruct((depth, ADA_ROWS, 3 * D_MODEL), F32),
        compiler_params=_params("arbitrary", "arbitrary"),
        name="ada",
    )(cvec, ada_w, ada_b.reshape(depth, 1, 3 * D_MODEL))


def _mod_spec(mod):
    if mod.shape[0] == 1:
        return pl.BlockSpec((1, 1, D_MODEL), lambda b, *_: (0, 0, 0))
    return pl.BlockSpec((1, 1, D_MODEL), lambda b, *_: (b, 0, 0))


EVEN_PLANES = 6


def _even_in_kernel(x_ref, sh_ref, sc_ref, ng_ref, w_ref, o_ref):
    h = _rms_rows(x_ref[0], ng_ref[...]) * (1.0 + sc_ref[0]) + sh_ref[0]
    hb = h.astype(BF16)
    W = RET_WIDTH

    def proj(j):
        return _dot(hb, w_ref[:, j * W:(j + 1) * W])

    o_ref[0, 0] = proj(0).astype(BF16)
    o_ref[1, 0] = (proj(1) * (RET_DK ** -0.5)).astype(BF16)
    o_ref[2, 0] = proj(2).astype(BF16)
    o_ref[3, 0] = _silu(proj(3)).astype(BF16)
    o_ref[4, 0] = (_silu(proj(7)) * proj(4)).astype(BF16)
    o_ref[5, 0] = (proj(5) * proj(6)).astype(BF16)


def _even_in_call(x, sh, sc, ng, w, tm):
    B, T, _ = x.shape
    return pl.pallas_call(
        _even_in_kernel,
        grid=(B, T // tm),
        in_specs=[
            pl.BlockSpec((1, tm, D_MODEL), lambda b, i: (b, i, 0)),
            _mod_spec(sh), _mod_spec(sc),
            pl.BlockSpec((1, D_MODEL), lambda b, i: (0, 0)),
            pl.BlockSpec(w.shape, lambda b, i: (0, 0)),
        ],
        out_specs=pl.BlockSpec((EVEN_PLANES, 1, tm, RET_WIDTH), lambda b, i: (0, b, i, 0)),
        out_shape=jax.ShapeDtypeStruct((EVEN_PLANES, B, T, RET_WIDTH), BF16),
        compiler_params=_params("parallel", "arbitrary"),
        name="even_in",
    )(x, sh, sc, ng, w)


def _even_mix_kernel(k_ref, v_ref, q_ref, sga_ref, u_ref, z_ref, zp_ref, zn_ref,
                     x_ref, g_ref, cw_ref, wo_ref, mask_ref, dec_ref, cdec_ref,
                     s0f_ref, s0b_ref,
                     xo_ref, nsf_ref, nsb_ref,
                     sb_all, sf_run, sb_run, ymix,
                     *, nc, nb, zero_init):
    p = pl.program_id(1)
    j = pl.program_id(2)
    C = RET_CHUNK
    H = RET_HEADS
    dv = RET_DV

    @pl.when((p == 0) & (j == 0))
    def _():
        if zero_init:
            sf_run[...] = jnp.zeros_like(sf_run)
            sb_run[...] = jnp.zeros_like(sb_run)
        else:
            sf_run[...] = s0f_ref[0]
            sb_run[...] = s0b_ref[0]

    @pl.when(p == 0)
    def _():
        blk = nb - 1 - j
        for cc in reversed(range(nc)):
            gc = blk * nc + cc
            rows = slice(cc * C, (cc + 1) * C)
            for h in range(H):
                cols = slice(h * dv, (h + 1) * dv)
                sb_all[gc, h] = sb_run[h].astype(BF16)
                kd = (k_ref[0, rows, cols].astype(F32) * dec_ref[3, h]).astype(BF16)
                sb_run[h] = sb_run[h] * cdec_ref[1, h] + _dot_tn(kd, v_ref[0, rows, cols])

        @pl.when(j == nb - 1)
        def _():
            nsb_ref[0] = sb_run[...]

    @pl.when(p == 1)
    def _():
        items = [(cc, h) for cc in range(nc) for h in range(H)]

        def tile(cc, h):
            return slice(cc * C, (cc + 1) * C), slice(h * dv, (h + 1) * dv)

        def scores(cc, h):
            rows, cols = tile(cc, h)
            return _dot_nt(q_ref[0, rows, cols], k_ref[0, rows, cols])

        s_next = scores(*items[0])
        for i, (cc, h) in enumerate(items):
            s = s_next
            if i + 1 < len(items):
                s_next = scores(*items[i + 1])
            gc = j * nc + cc
            rows, cols = tile(cc, h)
            qh = q_ref[0, rows, cols]
            kh = k_ref[0, rows, cols]
            vh = v_ref[0, rows, cols]
            inter = (_dot(qh, sf_run[h].astype(BF16)) * dec_ref[0, h]
                     + _dot(qh, sb_all[gc, h]) * dec_ref[1, h])
            kd = (kh.astype(F32) * dec_ref[2, h]).astype(BF16)
            sf_run[h] = sf_run[h] * cdec_ref[0, h] + _dot_tn(kd, vh)
            att = (s * mask_ref[h]).astype(BF16)
            o = _dot(att, vh) + inter
            on = o * lax.rsqrt(jnp.mean(o * o, axis=-1, keepdims=True) + EPS)
            ymix[rows, cols] = (on * sga_ref[0, rows, cols].astype(F32)).astype(BF16)

        z = z_ref[0].astype(F32)
        tb = z.shape[0]
        row = lax.broadcasted_iota(jnp.int32, z.shape, 0)
        prev_row = zp_ref[0, BF16_SUBLANES - 1:BF16_SUBLANES, :].astype(F32)
        next_row = zn_ref[0, 0:1, :].astype(F32)
        prev_row = jnp.where(j > 0, prev_row, 0.0)
        next_row = jnp.where(j < nb - 1, next_row, 0.0)
        z_prev = jnp.where(row == 0, prev_row, pltpu.roll(z, 1, 0))
        z_next = jnp.where(row == tb - 1, next_row, pltpu.roll(z, tb - 1, 0))
        zc = z_prev * cw_ref[0:1, :] + z * cw_ref[1:2, :] + z_next * cw_ref[2:3, :]
        ymix[:, RET_WIDTH:] = (u_ref[0].astype(F32) * zc).astype(BF16)

        y = _dot(ymix[...], wo_ref[...])
        xo_ref[0] = x_ref[0] + g_ref[0] * y

        @pl.when(j == nb - 1)
        def _():
            nsf_ref[0] = sf_run[...]


def _even_mix_call(pe, x, g, conv_w, w_out, mask, dec, cdec, s0f, s0b, tb):
    B, T, _ = x.shape
    nb = T // tb
    nc = tb // RET_CHUNK
    zero_init = s0f is None
    if zero_init:
        s0f = jnp.zeros((1, RET_HEADS, RET_DK, RET_DV), F32)
        s0b = s0f
        s0_spec = pl.BlockSpec((1, RET_HEADS, RET_DK, RET_DV), lambda b, p, j: (0, 0, 0, 0))
    else:
        s0_spec = pl.BlockSpec((1, RET_HEADS, RET_DK, RET_DV), lambda b, p, j: (b, 0, 0, 0))
    W = RET_WIDTH
    hb = tb // BF16_SUBLANES

    def plane(k, rows, row_map):
        return pl.BlockSpec((None, 1, rows, W), lambda b, p, j: (k, b, row_map(p, j), 0))

    def bwd_blk(p, j):
        return p * j + (1 - p) * (nb - 1 - j)

    def fwd_blk(p, j):
        return p * j

    def fwd_map(col):
        return lambda b, p, j: (b, p * j, col)

    kernel = functools.partial(_even_mix_kernel, nc=nc, nb=nb, zero_init=zero_init)
    state_shape = jax.ShapeDtypeStruct((B, RET_HEADS, RET_DK, RET_DV), F32)
    state_spec = pl.BlockSpec((1, RET_HEADS, RET_DK, RET_DV), lambda b, p, j: (b, 0, 0, 0))
    return pl.pallas_call(
        kernel,
        grid=(B, 2, nb),
        in_specs=[
            plane(1, tb, bwd_blk),
            plane(2, tb, bwd_blk),
            plane(0, tb, fwd_blk),
            plane(3, tb, fwd_blk),
            plane(4, tb, fwd_blk),
            plane(5, tb, fwd_blk),
            plane(5, BF16_SUBLANES, lambda p, j: jnp.maximum(p * j * hb - 1, 0)),
            plane(5, BF16_SUBLANES, lambda p, j: jnp.minimum((p * j + 1) * hb, T // BF16_SUBLANES - 1)),
            pl.BlockSpec((1, tb, D_MODEL), fwd_map(0)),
            _mod_spec(g),
            pl.BlockSpec(conv_w.shape, lambda b, p, j: (0, 0)),
            pl.BlockSpec(w_out.shape, lambda b, p, j: (0, 0)),
            pl.BlockSpec(mask.shape, lambda b, p, j: (0, 0, 0)),
            pl.BlockSpec(dec.shape, lambda b, p, j: (0, 0, 0, 0)),
            pl.BlockSpec(memory_space=pltpu.SMEM),
            s0_spec, s0_spec,
        ],
        out_specs=[
            pl.BlockSpec((1, tb, D_MODEL), fwd_map(0)),
            state_spec, state_spec,
        ],
        out_shape=[jax.ShapeDtypeStruct((B, T, D_MODEL), F32), state_shape, state_shape],
        scratch_shapes=[
            pltpu.VMEM((T // RET_CHUNK, RET_HEADS, RET_DK, RET_DV), BF16),
            pltpu.VMEM((RET_HEADS, RET_DK, RET_DV), F32),
            pltpu.VMEM((RET_HEADS, RET_DK, RET_DV), F32),
            pltpu.VMEM((tb, D_MODEL), BF16),
        ],
        compiler_params=_params("parallel", "arbitrary", "arbitrary"),
        name="even_mix",
    )(pe, pe, pe, pe, pe, pe, pe, pe, x, g, conv_w, w_out, mask, dec, cdec, s0f, s0b)


def _retention_tables():
    C = RET_CHUNK
    hh = jnp.arange(RET_HEADS, dtype=F32)
    lg_f = jnp.log(1.0 - 2.0 ** (-5.0 - hh))
    lg_b = jnp.log(1.0 - 2.0 ** (-5.0 - hh - RET_BWD_OFFSET))
    idx = jnp.arange(C, dtype=F32)
    diff = idx[:, None] - idx[None, :]
    m_f = jnp.where(diff >= 0, jnp.exp(lg_f[:, None, None] * jnp.maximum(diff, 0.0)), 0.0)
    m_b = jnp.where(diff <= 0, jnp.exp(lg_b[:, None, None] * jnp.maximum(-diff, 0.0)), 0.0)
    mask = m_f + m_b
    q_f = jnp.exp(lg_f[:, None] * (idx[None, :] + 1.0))
    q_b = jnp.exp(lg_b[:, None] * (C - idx[None, :]))
    k_f = jnp.exp(lg_f[:, None] * (C - 1.0 - idx[None, :]))
    k_b = jnp.exp(lg_b[:, None] * idx[None, :])
    dec = jnp.stack([q_f, q_b, k_f, k_b])
    dec = jnp.broadcast_to(dec[..., None], dec.shape + (RET_DV,))
    cdec = jnp.stack([jnp.exp(lg_f * C), jnp.exp(lg_b * C)])
    return mask, dec, cdec


def _rope_mix(r, cs_ref, first):
    if cs_ref is not None:
        t = r * cs_ref[...]
        r = t + pltpu.roll(t, QK_ROPE, 1)
    return jnp.where(first, r, 0.0)


def _mla_in_kernel(*refs, rope, emit_cache, tq):
    (x_ref, sh_ref, sc_ref, ng_ref, wlat_ref, wg_ref, qng_ref, kvng_ref, qupt_ref,
     kup_ref, vupt_ref) = refs[:11]
    pos = 11
    cs_ref = cst_ref = None
    if rope:
        cs_ref, cst_ref = refs[pos:pos + 2]
        pos += 2
    q_ref, k_ref, v_ref, sg_ref = refs[pos:pos + 4]
    pos += 4
    if emit_cache:
        ckv_ref, kr_ref = refs[pos:pos + 2]

    h = _rms_rows(x_ref[0], ng_ref[...]) * (1.0 + sc_ref[0]) + sh_ref[0]
    hb = h.astype(BF16)
    lat = _dot(hb, wlat_ref[...])
    kr2 = lat[:, Q_LORA + KV_LORA:]
    sg_ref[0] = _silu(_dot(hb, wg_ref[...])).astype(BF16)
    qn = _rms_rows(lat[:, :Q_LORA], qng_ref[...]).astype(BF16)
    ckv = _rms_rows(lat[:, Q_LORA:Q_LORA + KV_LORA], kvng_ref[...])
    qt = _dot_nt(qupt_ref[...], qn)
    ckv_b = ckv.astype(BF16)
    kn = _dot(ckv_b, kup_ref[...])
    vt = _dot_nt(vupt_ref[...], ckv_b)

    scale = (QK_NOPE + QK_ROPE) ** -0.5 * LOG2_E
    tm = kr2.shape[0]
    first = lax.broadcasted_iota(jnp.int32, kr2.shape, 1) < QK_ROPE
    k_rot = _rope_mix(kr2, cs_ref, first).astype(BF16)
    q_pad = jnp.zeros((QK_PAD - QK_NOPE - QK_ROPE, tq), BF16)
    f = QK_ROPE // 4
    for hd in range(MLA_HEADS):
        base = hd * (QK_NOPE + QK_ROPE)
        q_nope = (qt[base:base + QK_NOPE, :] * scale).astype(BF16)
        q_rot = qt[base + QK_NOPE:base + QK_NOPE + QK_ROPE, :]
        if rope:
            partner = jnp.concatenate([q_rot[f:2 * f], q_rot[0:f], q_rot[3 * f:4 * f], q_rot[2 * f:3 * f]], axis=0)
            q_rot = q_rot * cst_ref[0:QK_ROPE, :] + partner * cst_ref[QK_ROPE:2 * QK_ROPE, :]
        q_rot = (q_rot * scale).astype(BF16)
        for j in range(tm // tq):
            cols = slice(j * tq, (j + 1) * tq)
            q_ref[0, hd, j, 0:QK_NOPE, :] = q_nope[:, cols]
            q_ref[0, hd, j, QK_NOPE:QK_NOPE + QK_ROPE, :] = q_rot[:, cols]
            q_ref[0, hd, j, QK_NOPE + QK_ROPE:QK_PAD, :] = q_pad
        k_ref[0, hd, :, 0:QK_NOPE] = kn[:, hd * QK_NOPE:(hd + 1) * QK_NOPE].astype(BF16)
        k_ref[0, hd, :, QK_NOPE:QK_PAD] = k_rot
        v_ref[0, hd] = vt[hd * V_HEAD:(hd + 1) * V_HEAD, :].astype(BF16)
    if emit_cache:
        ckv_ref[0] = ckv
        kr_ref[0] = kr2[:, :QK_ROPE]


def _mla_in_call(x, sh, sc, ng, wts, cs, tm, tq, emit_cache):
    B, T, _ = x.shape
    rope = cs is not None
    wlat, wg, qng, kvng, qup, kup, vupt = wts

    def full(a):
        return pl.BlockSpec(a.shape, lambda b, i: (0,) * a.ndim)

    in_specs = [
        pl.BlockSpec((1, tm, D_MODEL), lambda b, i: (b, i, 0)),
        _mod_spec(sh), _mod_spec(sc), full(ng),
        full(wlat), full(wg), full(qng), full(kvng), full(qup), full(kup), full(vupt),
    ]
    args = [x, sh, sc, ng, wlat, wg, qng, kvng, qup, kup, vupt]
    if rope:
        in_specs += [pl.BlockSpec((tm, 2 * QK_ROPE), lambda b, i: (i, 0)),
                     pl.BlockSpec((2 * QK_ROPE, tm), lambda b, i: (0, i))]
        f = QK_ROPE // 4
        sign = jnp.concatenate([jnp.ones((QK_ROPE,), F32)] + [-jnp.ones((f,), F32), jnp.ones((f,), F32)] * 2)
        args += [cs, (cs * sign).T]
    out_specs = [pl.BlockSpec((1, MLA_HEADS, tm // tq, QK_PAD, tq), lambda b, i: (b, 0, i, 0, 0)),
                 pl.BlockSpec((1, MLA_HEADS, tm, QK_PAD), lambda b, i: (b, 0, i, 0)),
                 pl.BlockSpec((1, MLA_HEADS, V_HEAD, tm), lambda b, i: (b, 0, 0, i)),
                 pl.BlockSpec((1, tm, D_MODEL), lambda b, i: (b, i, 0))]
    out_shape = [jax.ShapeDtypeStruct((B, MLA_HEADS, T // tq, QK_PAD, tq), BF16),
                 jax.ShapeDtypeStruct((B, MLA_HEADS, T, QK_PAD), BF16),
                 jax.ShapeDtypeStruct((B, MLA_HEADS, V_HEAD, T), BF16),
                 jax.ShapeDtypeStruct((B, T, D_MODEL), BF16)]
    if emit_cache:
        out_specs += [pl.BlockSpec((1, tm, KV_LORA), lambda b, i: (b, i, 0)),
                      pl.BlockSpec((1, tm, QK_ROPE), lambda b, i: (b, i, 0))]
        out_shape += [jax.ShapeDtypeStruct((B, T, KV_LORA), F32),
                      jax.ShapeDtypeStruct((B, T, QK_ROPE), F32)]
    return pl.pallas_call(
        functools.partial(_mla_in_kernel, rope=rope, emit_cache=emit_cache, tq=tq),
        grid=(B, T // tm),
        in_specs=in_specs, out_specs=out_specs, out_shape=out_shape,
        compiler_params=_params("parallel", "arbitrary"),
        name="mla_in",
    )(*args)


def _ctx_expand_kernel(ckv_ref, kr_ref, kup_ref, vupt_ref, k_ref, v_ref):
    ckv_b = ckv_ref[0].astype(BF16)
    kn = _dot(ckv_b, kup_ref[...])
    vt = _dot_nt(vupt_ref[...], ckv_b)
    kr = kr_ref[0].astype(BF16)
    zeros = jnp.zeros_like(kr)
    for hd in range(MLA_HEADS):
        k_ref[0, hd, :, 0:QK_NOPE] = kn[:, hd * QK_NOPE:(hd + 1) * QK_NOPE].astype(BF16)
        k_ref[0, hd, :, QK_NOPE:QK_NOPE + QK_ROPE] = kr
        k_ref[0, hd, :, QK_NOPE + QK_ROPE:QK_PAD] = zeros
        v_ref[0, hd] = vt[hd * V_HEAD:(hd + 1) * V_HEAD, :].astype(BF16)


def _ctx_expand_call(ckv, kr, kup, vupt):
    B, L, _ = ckv.shape
    return pl.pallas_call(
        _ctx_expand_kernel,
        grid=(B,),
        in_specs=[
            pl.BlockSpec((1, L, KV_LORA), lambda b: (b, 0, 0)),
            pl.BlockSpec((1, L, QK_ROPE), lambda b: (b, 0, 0)),
            pl.BlockSpec(kup.shape, lambda b: (0, 0)),
            pl.BlockSpec(vupt.shape, lambda b: (0, 0)),
        ],
        out_specs=[pl.BlockSpec((1, MLA_HEADS, L, QK_PAD), lambda b: (b, 0, 0, 0)),
                   pl.BlockSpec((1, MLA_HEADS, V_HEAD, L), lambda b: (b, 0, 0, 0))],
        out_shape=[jax.ShapeDtypeStruct((B, MLA_HEADS, L, QK_PAD), BF16),
                   jax.ShapeDtypeStruct((B, MLA_HEADS, V_HEAD, L), BF16)],
        compiler_params=_params("parallel"),
        name="ctx_expand",
    )(ckv, kr, kup, vupt)


ATTN_UNROLL = 4
ATTN_LEAD = 6


def _attn_ctx_kernel(q_ref, kc_ref, kl_ref, vc_ref, vl_ref, sg_ref, o_ref, s0_scr, s1_scr):
    nq, _, tq = q_ref.shape[2:]
    T = nq * tq
    Lc = kc_ref.shape[2]
    tiles = ([(kc_ref, vc_ref, j * MXU_DIM) for j in range(Lc // MXU_DIM)]
             + [(kl_ref, vl_ref, j * MXU_DIM) for j in range(T // MXU_DIM)])
    groups = MXU_DIM // F32_SUBLANES

    def score_tile(qb, t, s_dst):
        kref, _, off = tiles[t]
        s = _dot(kref[0, 0, off:off + MXU_DIM, :], qb)
        s_dst[t * MXU_DIM:(t + 1) * MXU_DIM, :] = s

    def tile_max(s_buf, t):
        s = s_buf[t * MXU_DIM:(t + 1) * MXU_DIM, :]
        return jnp.max(s.reshape(groups, F32_SUBLANES, tq), axis=0)

    def step(blk, blk_next, m_cur, s_cur, s_nxt):
        scoring = blk_next is not None
        row = pl.multiple_of(blk * tq, tq)
        if scoring:
            q_next = q_ref[0, 0, blk_next]
            for t in range(ATTN_LEAD):
                score_tile(q_next, t, s_nxt)
        acc = lacc = None
        parts = []
        for t, (_, vref, off) in enumerate(tiles):
            p = jnp.exp2(s_cur[t * MXU_DIM:(t + 1) * MXU_DIM, :] - m_cur)
            ps = jnp.sum(p.reshape(groups, F32_SUBLANES, tq), axis=0)
            lacc = ps if lacc is None else lacc + ps
            pv = _dot(vref[0, 0, :, off:off + MXU_DIM], p.astype(BF16))
            acc = pv if acc is None else acc + pv
            if scoring:
                if t + ATTN_LEAD < len(tiles):
                    score_tile(q_next, t + ATTN_LEAD, s_nxt)
                parts.append(tile_max(s_nxt, t))
        l = jnp.sum(lacc, axis=0, keepdims=True)
        gate = sg_ref[0, pl.ds(row, tq), :].astype(F32)
        o_ref[0, pl.ds(row, tq), :] = ((acc * (1.0 / l)).T * gate).astype(BF16)
        if not scoring:
            return None
        return jnp.max(functools.reduce(jnp.maximum, parts), axis=0, keepdims=True)

    q0 = q_ref[0, 0, 0]
    for t in range(len(tiles)):
        score_tile(q0, t, s0_scr)
    mrun = functools.reduce(jnp.maximum, [tile_max(s0_scr, t) for t in range(len(tiles))])
    m0 = jnp.max(mrun, axis=0, keepdims=True)

    def trip(i, m, last):
        bufs = (s0_scr, s1_scr)
        for u in range(ATTN_UNROLL):
            blk = i * ATTN_UNROLL + u
            blk_next = None if (last and u == ATTN_UNROLL - 1) else blk + 1
            m = step(blk, blk_next, m, bufs[u % 2], bufs[(u + 1) % 2])
        return m

    trips = nq // ATTN_UNROLL
    m = lax.fori_loop(0, trips - 1, lambda i, m: trip(i, m, False), m0)
    trip(trips - 1, m, True)


def _attn_ctx_call(q, kc, kl, vc, vl, sg):
    B, H, nq, _, tq = q.shape
    T = nq * tq
    Lc = kc.shape[2]
    assert nq % ATTN_UNROLL == 0 and ATTN_UNROLL % 2 == 0
    return pl.pallas_call(
        _attn_ctx_kernel,
        scratch_shapes=[pltpu.VMEM((Lc + T, tq), F32), pltpu.VMEM((Lc + T, tq), F32)],
        grid=(B, H),
        in_specs=[
            pl.BlockSpec((1, 1, nq, QK_PAD, tq), lambda b, h: (b, h, 0, 0, 0)),
            pl.BlockSpec((1, 1, Lc, QK_PAD), lambda b, h: (b, h, 0, 0)),
            pl.BlockSpec((1, 1, T, QK_PAD), lambda b, h: (b, h, 0, 0)),
            pl.BlockSpec((1, 1, V_HEAD, Lc), lambda b, h: (b, h, 0, 0)),
            pl.BlockSpec((1, 1, V_HEAD, T), lambda b, h: (b, h, 0, 0)),
            pl.BlockSpec((1, T, V_HEAD), lambda b, h: (b, 0, h)),
        ],
        out_specs=pl.BlockSpec((1, T, V_HEAD), lambda b, h: (b, 0, h)),
        out_shape=jax.ShapeDtypeStruct((B, T, H * V_HEAD), BF16),
        compiler_params=_params("parallel", "arbitrary"),
        name="attn_ctx",
    )(q, kc, kl, vc, vl, sg)


def _attn_self_kernel(q_ref, k_ref, v_ref, sg_ref, o_ref):
    heads = range(MLA_HEADS)
    s = [_dot(k_ref[0, hd], q_ref[0, hd, 0]) for hd in heads]
    p = [jnp.exp2(a - jnp.max(a, axis=0, keepdims=True)) for a in s]
    inv = [1.0 / jnp.sum(a, axis=0, keepdims=True) for a in p]
    o = [_dot(v_ref[0, hd], p[hd].astype(BF16)) * inv[hd] for hd in heads]
    for hd in heads:
        cols = slice(hd * V_HEAD, (hd + 1) * V_HEAD)
        o_ref[0, :, cols] = (o[hd].T * sg_ref[0, :, cols].astype(F32)).astype(BF16)


def _attn_self_call(q, k, v, sg):
    _, H, nseq, _, seq = q.shape
    N = nseq * seq
    return pl.pallas_call(
        _attn_self_kernel,
        grid=(nseq,),
        in_specs=[
            pl.BlockSpec((1, H, 1, QK_PAD, seq), lambda b: (0, 0, b, 0, 0)),
            pl.BlockSpec((1, H, seq, QK_PAD), lambda b: (0, 0, b, 0)),
            pl.BlockSpec((1, H, V_HEAD, seq), lambda b: (0, 0, 0, b)),
            pl.BlockSpec((1, seq, H * V_HEAD), lambda b: (0, b, 0)),
        ],
        out_specs=pl.BlockSpec((1, seq, H * V_HEAD), lambda b: (0, b, 0)),
        out_shape=jax.ShapeDtypeStruct((1, N, H * V_HEAD), BF16),
        compiler_params=_params("parallel"),
        name="attn_self",
    )(q, k, v, sg)


def _mla_out_kernel(o_ref, x_ref, g_ref, wo_ref, fg_ref, y_ref):
    x2 = x_ref[0] + g_ref[0] * _dot(o_ref[0], wo_ref[...])
    y_ref[0] = _rms_rows(x2, fg_ref[...])


def _mla_out_call(o, x, g, w_out, fg, tm):
    B, T, _ = x.shape
    tok = lambda: pl.BlockSpec((1, tm, D_MODEL), lambda b, i: (b, i, 0))
    return pl.pallas_call(
        _mla_out_kernel,
        grid=(B, T // tm),
        in_specs=[tok(), tok(), _mod_spec(g),
                  pl.BlockSpec(w_out.shape, lambda b, i: (0, 0)),
                  pl.BlockSpec((1, D_MODEL), lambda b, i: (0, 0))],
        out_specs=tok(),
        out_shape=jax.ShapeDtypeStruct((B, T, D_MODEL), F32),
        compiler_params=_params("parallel", "arbitrary"),
        name="mla_out",
    )(o, x, g, w_out, fg)


def _swap_halves(w):
    f = QK_ROPE // 4
    return jnp.concatenate([-w[..., f:2 * f], w[..., 0:f], -w[..., 3 * f:4 * f], w[..., 2 * f:3 * f]], axis=-1)


def _rope_table(T):
    f = QK_ROPE // 4
    rows = T // GRID_W
    inv = ROPE_BASE ** (-jnp.arange(f, dtype=F32) / f)
    ar = jnp.arange(rows, dtype=F32)[:, None] * inv
    ac = jnp.arange(GRID_W, dtype=F32)[:, None] * inv

    def by_row(a):
        return jnp.broadcast_to(a[:, None, :], (rows, GRID_W, f)).reshape(T, f)

    def by_col(a):
        return jnp.broadcast_to(a[None, :, :], (rows, GRID_W, f)).reshape(T, f)

    cr, sr, cc, sc = by_row(jnp.cos(ar)), by_row(jnp.sin(ar)), by_col(jnp.cos(ac)), by_col(jnp.sin(ac))
    return jnp.concatenate([cr, cr, cc, cc, sr, sr, sc, sc], axis=-1)


def kernel(x_prompt, x_sample, c, state_ret_fwd, state_ret_bwd, cache_mla_ckv, cache_mla_krope, c_ctx,
           ada_w, ada_b, norm_g, even_in_w, even_conv_w, even_out_w, odd_in_w, odd_q_norm_g,
           odd_kv_norm_g, odd_q_up_w, odd_kv_up_w, odd_out_w, final_norm_g):
    BP, SEQ, D = x_prompt.shape
    BS, TS, _ = x_sample.shape
    NP = BP * SEQ

    cvec = jnp.concatenate([c, c_ctx[None, :], jnp.zeros((ADA_ROWS - BS - 1, D), F32)], axis=0)
    mod = _ada_call(cvec, ada_w, ada_b)

    def mods(l):
        m = mod[l].reshape(ADA_ROWS, 3, 1, D)
        return ([m[BS:BS + 1, i] for i in range(3)], [m[:BS, i] for i in range(3)])

    (sh_p, sc_p, g_p), (sh_s, sc_s, g_s) = mods(0)
    ng = norm_g[0][None, :]
    w_in = even_in_w[0].astype(BF16)
    w_out = even_out_w[0].astype(BF16)
    conv_w = even_conv_w[0]
    mask, dec, cdec = _retention_tables()

    xp = x_prompt.reshape(1, NP, D)
    pe_p = _even_in_call(xp, sh_p, sc_p, ng, w_in, 512)
    pe_s = _even_in_call(x_sample, sh_s, sc_s, ng, w_in, 512)
    xp1, nsf, nsb = _even_mix_call(pe_p.reshape(EVEN_PLANES, BP, SEQ, RET_WIDTH), x_prompt, g_p, conv_w, w_out,
                                   mask, dec, cdec, None, None, SEQ)
    xs1, _, _ = _even_mix_call(pe_s, x_sample, g_s, conv_w, w_out, mask, dec, cdec,
                               state_ret_fwd[:, 0], state_ret_bwd[:, 0], 512)

    (sh_p, sc_p, g_p), (sh_s, sc_s, g_s) = mods(1)
    ng = norm_g[1][None, :]
    w_in = odd_in_w[0]
    nq = Q_LORA + KV_LORA
    wkr = w_in[:, nq:nq + QK_ROPE]
    wlat = jnp.concatenate([w_in[:, :nq + QK_ROPE], _swap_halves(wkr)], axis=-1).astype(BF16)
    wg = w_in[:, nq + QK_ROPE:].astype(BF16)
    qup = odd_q_up_w[0].T.astype(BF16)
    kvup = odd_kv_up_w[0].reshape(KV_LORA, MLA_HEADS, QK_NOPE + V_HEAD)
    kup = kvup[..., :QK_NOPE].reshape(KV_LORA, MLA_HEADS * QK_NOPE).astype(BF16)
    vupt = kvup[..., QK_NOPE:].reshape(KV_LORA, MLA_HEADS * V_HEAD).T.astype(BF16)
    wts = (wlat, wg, odd_q_norm_g[0][None, :], odd_kv_norm_g[0][None, :], qup, kup, vupt)
    w_out = odd_out_w[0].astype(BF16)
    fg = final_norm_g[None, :]

    xp1f = xp1.reshape(1, NP, D)
    q_p, k_p, v_p, sg_p, ckv_p, kr_p = _mla_in_call(xp1f, sh_p, sc_p, ng, wts, None, 512, SEQ, True)
    q_s, k_s, v_s, sg_s = _mla_in_call(xs1, sh_s, sc_s, ng, wts, _rope_table(TS), 512, MXU_DIM, False)
    k_c, v_c = _ctx_expand_call(cache_mla_ckv[:, 0], cache_mla_krope[:, 0], kup, vupt)

    o_p = _attn_self_call(q_p, k_p, v_p, sg_p)
    o_s = _attn_ctx_call(q_s, k_c, k_s, v_c, v_s, sg_s)

    y_p = _mla_out_call(o_p, xp1f, g_p, w_out, fg, 1024)
    y_s = _mla_out_call(o_s, xs1, g_s, w_out, fg, 1024)

    return (y_p.reshape(BP, SEQ, D), y_s,
            nsf[:, None], nsb[:, None],
            ckv_p.reshape(BP, 1, SEQ, KV_LORA), kr_p.reshape(BP, 1, SEQ, QK_ROPE))
```

```python
import functools

import jax
import jax.numpy as jnp
from jax import lax
from jax.experimental import pallas as pl
from jax.experimental.pallas import tpu as pltpu

F32 = jnp.float32
BF16 = jnp.bfloat16

D_MODEL = 1024
GRID_W = 64
EPS = 1e-6
RET_HEADS = 4
RET_DK = 128
RET_DV = 128
RET_WIDTH = RET_HEADS * RET_DV
RET_CHUNK = 128
RET_BWD_OFFSET = 0.5
CONV_WIDTH = D_MODEL - RET_WIDTH
MLA_HEADS = 8
QK_NOPE = 128
QK_ROPE = 64
V_HEAD = 128
Q_LORA = 384
KV_LORA = 256
ROPE_BASE = 10000.0
QK_PAD = 256
ADA_ROWS = 16
BF16_SUBLANES = 16
F32_SUBLANES = 8
MXU_DIM = 256
LOG2_E = 1.4426950408889634
VMEM_LIMIT = 56 * 1024 * 1024


def _silu(x):
    return x * (1.0 / (1.0 + jnp.exp(-x)))


def _rms_rows(x, g):
    return x * lax.rsqrt(jnp.mean(x * x, axis=-1, keepdims=True) + EPS) * g


def _dot(a, b):
    return jnp.dot(a, b, preferred_element_type=F32)


def _dot_nt(a, b):
    return lax.dot_general(a, b, (((1,), (1,)), ((), ())), preferred_element_type=F32)


def _dot_tn(a, b):
    return lax.dot_general(a, b, (((0,), (0,)), ((), ())), preferred_element_type=F32)


def _params(*sem):
    return pltpu.CompilerParams(dimension_semantics=sem, vmem_limit_bytes=VMEM_LIMIT)


def _ada_kernel(c_ref, w_ref, b_ref, o_ref):
    s = _silu(c_ref[...]).astype(BF16)
    o_ref[0] = _dot(s, w_ref[0].astype(BF16)) + b_ref[0]


def _ada_call(cvec, ada_w, ada_b):
    depth = ada_w.shape[0]
    return pl.pallas_call(
        _ada_kernel,
        grid=(depth, 3),
        in_specs=[
            pl.BlockSpec((ADA_ROWS, D_MODEL), lambda l, j: (0, 0)),
            pl.BlockSpec((1, D_MODEL, D_MODEL), lambda l, j: (l, 0, j)),
            pl.BlockSpec((1, 1, D_MODEL), lambda l, j: (l, 0, j)),
        ],
        out_specs=pl.BlockSpec((1, ADA_ROWS, D_MODEL), lambda l, j: (l, 0, j)),
        out_shape=jax.ShapeDtypeStruct((depth, ADA_ROWS, 3 * D_MODEL), F32),
        compiler_params=_params("arbitrary", "arbitrary"),
        name="ada",
    )(cvec, ada_w, ada_b.reshape(depth, 1, 3 * D_MODEL))


def _mod_spec(mod):
    if mod.shape[0] == 1:
        return pl.BlockSpec((1, 1, D_MODEL), lambda b, *_: (0, 0, 0))
    return pl.BlockSpec((1, 1, D_MODEL), lambda b, *_: (b, 0, 0))


EVEN_PLANES = 6


def _even_in_kernel(x_ref, sh_ref, sc_ref, ng_ref, w_ref, o_ref):
    h = _rms_rows(x_ref[0], ng_ref[...]) * (1.0 + sc_ref[0]) + sh_ref[0]
    hb = h.astype(BF16)
    W = RET_WIDTH

    def proj(j):
        return _dot(hb, w_ref[:, j * W:(j + 1) * W])

    o_ref[0, 0] = proj(0).astype(BF16)
    o_ref[1, 0] = (proj(1) * (RET_DK ** -0.5)).astype(BF16)
    o_ref[2, 0] = proj(2).astype(BF16)
    o_ref[3, 0] = _silu(proj(3)).astype(BF16)
    o_ref[4, 0] = (_silu(proj(7)) * proj(4)).astype(BF16)
    o_ref[5, 0] = (proj(5) * proj(6)).astype(BF16)


def _even_in_call(x, sh, sc, ng, w, tm):
    B, T, _ = x.shape
    return pl.pallas_call(
        _even_in_kernel,
        grid=(B, T // tm),
        in_specs=[
            pl.BlockSpec((1, tm, D_MODEL), lambda b, i: (b, i, 0)),
            _mod_spec(sh), _mod_spec(sc),
            pl.BlockSpec((1, D_MODEL), lambda b, i: (0, 0)),
            pl.BlockSpec(w.shape, lambda b, i: (0, 0)),
        ],
        out_specs=pl.BlockSpec((EVEN_PLANES, 1, tm, RET_WIDTH), lambda b, i: (0, b, i, 0)),
        out_shape=jax.ShapeDtypeStruct((EVEN_PLANES, B, T, RET_WIDTH), BF16),
        compiler_params=_params("parallel", "arbitrary"),
        name="even_in",
    )(x, sh, sc, ng, w)


def _even_mix_kernel(k_ref, v_ref, q_ref, sga_ref, u_ref, z_ref, zp_ref, zn_ref,
                     x_ref, g_ref, cw_ref, wo_ref, mask_ref, dec_ref, cdec_ref,
                     s0f_ref, s0b_ref,
                     xo_ref, nsf_ref, nsb_ref,
                     sb_all, sf_run, sb_run, ymix,
                     *, nc, nb, zero_init):
    p = pl.program_id(1)
    j = pl.program_id(2)
    C = RET_CHUNK
    H = RET_HEADS
    dv = RET_DV

    @pl.when((p == 0) & (j == 0))
    def _():
        if zero_init:
            sf_run[...] = jnp.zeros_like(sf_run)
            sb_run[...] = jnp.zeros_like(sb_run)
        else:
            sf_run[...] = s0f_ref[0]
            sb_run[...] = s0b_ref[0]

    @pl.when(p == 0)
    def _():
        blk = nb - 1 - j
        for cc in reversed(range(nc)):
            gc = blk * nc + cc
            rows = slice(cc * C, (cc + 1) * C)
            for h in range(H):
                cols = slice(h * dv, (h + 1) * dv)
                sb_all[gc, h] = sb_run[h].astype(BF16)
                kd = (k_ref[0, rows, cols].astype(F32) * dec_ref[3, h]).astype(BF16)
                sb_run[h] = sb_run[h] * cdec_ref[1, h] + _dot_tn(kd, v_ref[0, rows, cols])

        @pl.when(j == nb - 1)
        def _():
            nsb_ref[0] = sb_run[...]

    @pl.when(p == 1)
    def _():
        items = [(cc, h) for cc in range(nc) for h in range(H)]

        def tile(cc, h):
            return slice(cc * C, (cc + 1) * C), slice(h * dv, (h + 1) * dv)

        def scores(cc, h):
            rows, cols = tile(cc, h)
            return _dot_nt(q_ref[0, rows, cols], k_ref[0, rows, cols])

        s_next = scores(*items[0])
        for i, (cc, h) in enumerate(items):
            s = s_next
            if i + 1 < len(items):
                s_next = scores(*items[i + 1])
            gc = j * nc + cc
            rows, cols = tile(cc, h)
            qh = q_ref[0, rows, cols]
            kh = k_ref[0, rows, cols]
            vh = v_ref[0, rows, cols]
            inter = (_dot(qh, sf_run[h].astype(BF16)) * dec_ref[0, h]
                     + _dot(qh, sb_all[gc, h]) * dec_ref[1, h])
            kd = (kh.astype(F32) * dec_ref[2, h]).astype(BF16)
            sf_run[h] = sf_run[h] * cdec_ref[0, h] + _dot_tn(kd, vh)
            att = (s * mask_ref[h]).astype(BF16)
            o = _dot(att, vh) + inter
            on = o * lax.rsqrt(jnp.mean(o * o, axis=-1, keepdims=True) + EPS)
            ymix[rows, cols] = (on * sga_ref[0, rows, cols].astype(F32)).astype(BF16)

        z = z_ref[0].astype(F32)
        tb = z.shape[0]
        row = lax.broadcasted_iota(jnp.int32, z.shape, 0)
        prev_row = zp_ref[0, BF16_SUBLANES - 1:BF16_SUBLANES, :].astype(F32)
        next_row = zn_ref[0, 0:1, :].astype(F32)
        prev_row = jnp.where(j > 0, prev_row, 0.0)
        next_row = jnp.where(j < nb - 1, next_row, 0.0)
        z_prev = jnp.where(row == 0, prev_row, pltpu.roll(z, 1, 0))
        z_next = jnp.where(row == tb - 1, next_row, pltpu.roll(z, tb - 1, 0))
        zc = z_prev * cw_ref[0:1, :] + z * cw_ref[1:2, :] + z_next * cw_ref[2:3, :]
        ymix[:, RET_WIDTH:] = (u_ref[0].astype(F32) * zc).astype(BF16)

        y = _dot(ymix[...], wo_ref[...])
        xo_ref[0] = x_ref[0] + g_ref[0] * y

        @pl.when(j == nb - 1)
        def _():
            nsf_ref[0] = sf_run[...]


def _even_mix_call(pe, x, g, conv_w, w_out, mask, dec, cdec, s0f, s0b, tb):
    B, T, _ = x.shape
    nb = T // tb
    nc = tb // RET_CHUNK
    zero_init = s0f is None
    if zero_init:
        s0f = jnp.zeros((1, RET_HEADS, RET_DK, RET_DV), F32)
        s0b = s0f
        s0_spec = pl.BlockSpec((1, RET_HEADS, RET_DK, RET_DV), lambda b, p, j: (0, 0, 0, 0))
    else:
        s0_spec = pl.BlockSpec((1, RET_HEADS, RET_DK, RET_DV), lambda b, p, j: (b, 0, 0, 0))
    W = RET_WIDTH
    hb = tb // BF16_SUBLANES

    def plane(k, rows, row_map):
        return pl.BlockSpec((None, 1, rows, W), lambda b, p, j: (k, b, row_map(p, j), 0))

    def bwd_blk(p, j):
        return p * j + (1 - p) * (nb - 1 - j)

    def fwd_blk(p, j):
        return p * j

    def fwd_map(col):
        return lambda b, p, j: (b, p * j, col)

    kernel = functools.partial(_even_mix_kernel, nc=nc, nb=nb, zero_init=zero_init)
    state_shape = jax.ShapeDtypeStruct((B, RET_HEADS, RET_DK, RET_DV), F32)
    state_spec = pl.BlockSpec((1, RET_HEADS, RET_DK, RET_DV), lambda b, p, j: (b, 0, 0, 0))
    return pl.pallas_call(
        kernel,
        grid=(B, 2, nb),
        in_specs=[
            plane(1, tb, bwd_blk),
            plane(2, tb, bwd_blk),
            plane(0, tb, fwd_blk),
            plane(3, tb, fwd_blk),
            plane(4, tb, fwd_blk),
            plane(5, tb, fwd_blk),
            plane(5, BF16_SUBLANES, lambda p, j: jnp.maximum(p * j * hb - 1, 0)),
            plane(5, BF16_SUBLANES, lambda p, j: jnp.minimum((p * j + 1) * hb, T // BF16_SUBLANES - 1)),
            pl.BlockSpec((1, tb, D_MODEL), fwd_map(0)),
            _mod_spec(g),
            pl.BlockSpec(conv_w.shape, lambda b, p, j: (0, 0)),
            pl.BlockSpec(w_out.shape, lambda b, p, j: (0, 0)),
            pl.BlockSpec(mask.shape, lambda b, p, j: (0, 0, 0)),
            pl.BlockSpec(dec.shape, lambda b, p, j: (0, 0, 0, 0)),
            pl.BlockSpec(memory_space=pltpu.SMEM),
            s0_spec, s0_spec,
        ],
        out_specs=[
            pl.BlockSpec((1, tb, D_MODEL), fwd_map(0)),
            state_spec, state_spec,
        ],
        out_shape=[jax.ShapeDtypeStruct((B, T, D_MODEL), F32), state_shape, state_shape],
        scratch_shapes=[
            pltpu.VMEM((T // RET_CHUNK, RET_HEADS, RET_DK, RET_DV), BF16),
            pltpu.VMEM((RET_HEADS, RET_DK, RET_DV), F32),
            pltpu.VMEM((RET_HEADS, RET_DK, RET_DV), F32),
            pltpu.VMEM((tb, D_MODEL), BF16),
        ],
        compiler_params=_params("parallel", "arbitrary", "arbitrary"),
        name="even_mix",
    )(pe, pe, pe, pe, pe, pe, pe, pe, x, g, conv_w, w_out, mask, dec, cdec, s0f, s0b)


def _even_layer_kernel(x_ref, sh_ref, sc_ref, g_ref, ng_ref, w_ref, cw_ref, wo_ref, mask_ref, dec_ref,
                       cdec_ref, s0f_ref, s0b_ref,
                       xo_ref, nsf_ref, nsb_ref,
                       k_all, v_all, z_all, sb_all, sf_run, sb_run, ymix,
                       *, nc, nb, zero_init):
    p = pl.program_id(1)
    j = pl.program_id(2)
    C = RET_CHUNK
    H = RET_HEADS
    dv = RET_DV
    W = RET_WIDTH
    tb = nc * C
    T = nb * tb

    def tile(cc, h):
        return slice(cc * C, (cc + 1) * C), slice(h * dv, (h + 1) * dv)

    def project():
        h = _rms_rows(x_ref[0], ng_ref[...]) * (1.0 + sc_ref[0]) + sh_ref[0]
        hb = h.astype(BF16)
        return lambda g: _dot(hb, w_ref[:, g * W:(g + 1) * W])

    @pl.when((p == 0) & (j == 0))
    def _():
        if zero_init:
            sf_run[...] = jnp.zeros_like(sf_run)
            sb_run[...] = jnp.zeros_like(sb_run)
        else:
            sf_run[...] = s0f_ref[0]
            sb_run[...] = s0b_ref[0]

    @pl.when(p == 0)
    def _():
        blk = nb - 1 - j
        r0 = pl.multiple_of(blk * tb, tb)
        proj = project()
        k = (proj(1) * (RET_DK ** -0.5)).astype(BF16)
        v = proj(2).astype(BF16)
        k_all[pl.ds(r0, tb), :] = k
        v_all[pl.ds(r0, tb), :] = v
        z_all[pl.ds(r0, tb), :] = (proj(5) * proj(6)).astype(BF16)
        for cc in reversed(range(nc)):
            gc = blk * nc + cc
            for h in range(H):
                rows, cols = tile(cc, h)
                sb_all[gc, h] = sb_run[h].astype(BF16)
                kd = (k[rows, cols].astype(F32) * dec_ref[3, h]).astype(BF16)
                sb_run[h] = sb_run[h] * cdec_ref[1, h] + _dot_tn(kd, v[rows, cols])

        @pl.when(j == nb - 1)
        def _():
            nsb_ref[0] = sb_run[...]

    @pl.when(p == 1)
    def _():
        r0 = pl.multiple_of(j * tb, tb)
        proj = project()
        q = proj(0).astype(BF16)
        sga = _silu(proj(3))
        k = k_all[pl.ds(r0, tb), :]
        v = v_all[pl.ds(r0, tb), :]
        items = [(cc, h) for cc in range(nc) for h in range(H)]

        def scores(cc, h):
            rows, cols = tile(cc, h)
            return _dot_nt(q[rows, cols], k[rows, cols])

        s_next = scores(*items[0])
        for i, (cc, h) in enumerate(items):
            s = s_next
            if i + 1 < len(items):
                s_next = scores(*items[i + 1])
            gc = j * nc + cc
            rows, cols = tile(cc, h)
            qh, kh, vh = q[rows, cols], k[rows, cols], v[rows, cols]
            inter = (_dot(qh, sf_run[h].astype(BF16)) * dec_ref[0, h]
                     + _dot(qh, sb_all[gc, h]) * dec_ref[1, h])
            kd = (kh.astype(F32) * dec_ref[2, h]).astype(BF16)
            sf_run[h] = sf_run[h] * cdec_ref[0, h] + _dot_tn(kd, vh)
            att = (s * mask_ref[h]).astype(BF16)
            o = _dot(att, vh) + inter
            on = o * lax.rsqrt(jnp.mean(o * o, axis=-1, keepdims=True) + EPS)
            ymix[rows, cols] = (on * sga[rows, cols]).astype(BF16)

        u = _silu(proj(7)) * proj(4)
        z = z_all[pl.ds(r0, tb), :].astype(F32)
        row = lax.broadcasted_iota(jnp.int32, z.shape, 0)
        n = BF16_SUBLANES
        r_prev = pl.multiple_of(jnp.maximum(r0 - n, 0), n)
        r_next = pl.multiple_of(jnp.minimum(r0 + tb, T - n), n)
        prev_row = z_all[pl.ds(r_prev, n), :][n - 1:n, :].astype(F32)
        next_row = z_all[pl.ds(r_next, n), :][0:1, :].astype(F32)
        prev_row = jnp.where(j > 0, prev_row, 0.0)
        next_row = jnp.where(j < nb - 1, next_row, 0.0)
        z_prev = jnp.where(row == 0, prev_row, pltpu.roll(z, 1, 0))
        z_next = jnp.where(row == tb - 1, next_row, pltpu.roll(z, tb - 1, 0))
        zc = z_prev * cw_ref[0:1, :] + z * cw_ref[1:2, :] + z_next * cw_ref[2:3, :]
        ymix[:, W:] = (u * zc).astype(BF16)

        y = _dot(ymix[...], wo_ref[...])
        xo_ref[0] = x_ref[0] + g_ref[0] * y

        @pl.when(j == nb - 1)
        def _():
            nsf_ref[0] = sf_run[...]


def _even_layer_call(x, sh, sc, g, ng, w_in, conv_w, w_out, mask, dec, cdec, s0f, s0b, tb):
    B, T, _ = x.shape
    nb = T // tb
    nc = tb // RET_CHUNK
    zero_init = s0f is None
    state_block = (1, RET_HEADS, RET_DK, RET_DV)
    if zero_init:
        s0f = s0b = jnp.zeros(state_block, F32)
        s0_spec = pl.BlockSpec(state_block, lambda b, p, j: (0, 0, 0, 0))
    else:
        s0_spec = pl.BlockSpec(state_block, lambda b, p, j: (b, 0, 0, 0))

    def const(a):
        return pl.BlockSpec(a.shape, lambda b, p, j: (0,) * a.ndim, pipeline_mode=pl.Buffered(1))

    kernel = functools.partial(_even_layer_kernel, nc=nc, nb=nb, zero_init=zero_init)
    state_shape = jax.ShapeDtypeStruct((B,) + state_block[1:], F32)
    state_spec = pl.BlockSpec(state_block, lambda b, p, j: (b, 0, 0, 0))
    seq_buf = pltpu.VMEM((T, RET_WIDTH), BF16)
    return pl.pallas_call(
        kernel,
        grid=(B, 2, nb),
        in_specs=[
            pl.BlockSpec((1, tb, D_MODEL), lambda b, p, j: (b, p * j + (1 - p) * (nb - 1 - j), 0)),
            _mod_spec(sh), _mod_spec(sc), _mod_spec(g),
            const(ng), const(w_in), const(conv_w), const(w_out), const(mask), const(dec),
            pl.BlockSpec(memory_space=pltpu.SMEM),
            s0_spec, s0_spec,
        ],
        out_specs=[
            pl.BlockSpec((1, tb, D_MODEL), lambda b, p, j: (b, p * j, 0)),
            state_spec, state_spec,
        ],
        out_shape=[jax.ShapeDtypeStruct((B, T, D_MODEL), F32), state_shape, state_shape],
        scratch_shapes=[
            seq_buf, seq_buf, seq_buf,
            pltpu.VMEM((T // RET_CHUNK, RET_HEADS, RET_DK, RET_DV), BF16),
            pltpu.VMEM((RET_HEADS, RET_DK, RET_DV), F32),
            pltpu.VMEM((RET_HEADS, RET_DK, RET_DV), F32),
            pltpu.VMEM((tb, D_MODEL), BF16),
        ],
        compiler_params=_params("parallel", "arbitrary", "arbitrary"),
        name="even_layer",
    )(x, sh, sc, g, ng, w_in, conv_w, w_out, mask, dec, cdec, s0f, s0b)


def _retention_tables():
    C = RET_CHUNK
    hh = jnp.arange(RET_HEADS, dtype=F32)
    lg_f = jnp.log(1.0 - 2.0 ** (-5.0 - hh))
    lg_b = jnp.log(1.0 - 2.0 ** (-5.0 - hh - RET_BWD_OFFSET))
    idx = jnp.arange(C, dtype=F32)
    diff = idx[:, None] - idx[None, :]
    m_f = jnp.where(diff >= 0, jnp.exp(lg_f[:, None, None] * jnp.maximum(diff, 0.0)), 0.0)
    m_b = jnp.where(diff <= 0, jnp.exp(lg_b[:, None, None] * jnp.maximum(-diff, 0.0)), 0.0)
    mask = m_f + m_b
    q_f = jnp.exp(lg_f[:, None] * (idx[None, :] + 1.0))
    q_b = jnp.exp(lg_b[:, None] * (C - idx[None, :]))
    k_f = jnp.exp(lg_f[:, None] * (C - 1.0 - idx[None, :]))
    k_b = jnp.exp(lg_b[:, None] * idx[None, :])
    dec = jnp.stack([q_f, q_b, k_f, k_b])
    dec = jnp.broadcast_to(dec[..., None], dec.shape + (RET_DV,))
    cdec = jnp.stack([jnp.exp(lg_f * C), jnp.exp(lg_b * C)])
    return mask, dec, cdec


def _rope_mix(r, cs_ref, first):
    if cs_ref is not None:
        t = r * cs_ref[...]
        r = t + pltpu.roll(t, QK_ROPE, 1)
    return jnp.where(first, r, 0.0)


def _mla_in_kernel(*refs, rope, emit_cache, tq):
    (x_ref, sh_ref, sc_ref, ng_ref, wlat_ref, wg_ref, qng_ref, kvng_ref, qupt_ref,
     kup_ref, vupt_ref) = refs[:11]
    pos = 11
    cs_ref = cst_ref = None
    if rope:
        cs_ref, cst_ref = refs[pos:pos + 2]
        pos += 2
    q_ref, k_ref, v_ref, sg_ref = refs[pos:pos + 4]
    pos += 4
    if emit_cache:
        ckv_ref, kr_ref = refs[pos:pos + 2]

    h = _rms_rows(x_ref[0], ng_ref[...]) * (1.0 + sc_ref[0]) + sh_ref[0]
    hb = h.astype(BF16)
    lat = _dot(hb, wlat_ref[...])
    kr2 = lat[:, Q_LORA + KV_LORA:]
    sg_ref[0] = _silu(_dot(hb, wg_ref[...])).astype(BF16)
    qn = _rms_rows(lat[:, :Q_LORA], qng_ref[...]).astype(BF16)
    ckv = _rms_rows(lat[:, Q_LORA:Q_LORA + KV_LORA], kvng_ref[...])
    qt = _dot_nt(qupt_ref[...], qn)
    ckv_b = ckv.astype(BF16)
    kn = _dot(ckv_b, kup_ref[...])
    vt = _dot_nt(vupt_ref[...], ckv_b)

    scale = (QK_NOPE + QK_ROPE) ** -0.5 * LOG2_E
    tm = kr2.shape[0]
    first = lax.broadcasted_iota(jnp.int32, kr2.shape, 1) < QK_ROPE
    k_rot = _rope_mix(kr2, cs_ref, first).astype(BF16)
    q_pad = jnp.zeros((QK_PAD - QK_NOPE - QK_ROPE, tq), BF16)
    f = QK_ROPE // 4
    for hd in range(MLA_HEADS):
        base = hd * (QK_NOPE + QK_ROPE)
        q_nope = (qt[base:base + QK_NOPE, :] * scale).astype(BF16)
        q_rot = qt[base + QK_NOPE:base + QK_NOPE + QK_ROPE, :]
        if rope:
            partner = jnp.concatenate([q_rot[f:2 * f], q_rot[0:f], q_rot[3 * f:4 * f], q_rot[2 * f:3 * f]], axis=0)
            q_rot = q_rot * cst_ref[0:QK_ROPE, :] + partner * cst_ref[QK_ROPE:2 * QK_ROPE, :]
        q_rot = (q_rot * scale).astype(BF16)
        for j in range(tm // tq):
            cols = slice(j * tq, (j + 1) * tq)
            q_ref[0, hd, j, 0:QK_NOPE, :] = q_nope[:, cols]
            q_ref[0, hd, j, QK_NOPE:QK_NOPE + QK_ROPE, :] = q_rot[:, cols]
            q_ref[0, hd, j, QK_NOPE + QK_ROPE:QK_PAD, :] = q_pad
        k_ref[0, hd, :, 0:QK_NOPE] = kn[:, hd * QK_NOPE:(hd + 1) * QK_NOPE].astype(BF16)
        k_ref[0, hd, :, QK_NOPE:QK_PAD] = k_rot
        v_ref[0, hd] = vt[hd * V_HEAD:(hd + 1) * V_HEAD, :].astype(BF16)
    if emit_cache:
        ckv_ref[0] = ckv
        kr_ref[0] = kr2[:, :QK_ROPE]


def _mla_in_call(x, sh, sc, ng, wts, cs, tm, tq, emit_cache):
    B, T, _ = x.shape
    rope = cs is not None
    wlat, wg, qng, kvng, qup, kup, vupt = wts

    def full(a):
        return pl.BlockSpec(a.shape, lambda b, i: (0,) * a.ndim)

    in_specs = [
        pl.BlockSpec((1, tm, D_MODEL), lambda b, i: (b, i, 0)),
        _mod_spec(sh), _mod_spec(sc), full(ng),
        full(wlat), full(wg), full(qng), full(kvng), full(qup), full(kup), full(vupt),
    ]
    args = [x, sh, sc, ng, wlat, wg, qng, kvng, qup, kup, vupt]
    if rope:
        in_specs += [pl.BlockSpec((tm, 2 * QK_ROPE), lambda b, i: (i, 0)),
                     pl.BlockSpec((2 * QK_ROPE, tm), lambda b, i: (0, i))]
        f = QK_ROPE // 4
        sign = jnp.concatenate([jnp.ones((QK_ROPE,), F32)] + [-jnp.ones((f,), F32), jnp.ones((f,), F32)] * 2)
        args += [cs, (cs * sign).T]
    out_specs = [pl.BlockSpec((1, MLA_HEADS, tm // tq, QK_PAD, tq), lambda b, i: (b, 0, i, 0, 0)),
                 pl.BlockSpec((1, MLA_HEADS, tm, QK_PAD), lambda b, i: (b, 0, i, 0)),
                 pl.BlockSpec((1, MLA_HEADS, V_HEAD, tm), lambda b, i: (b, 0, 0, i)),
                 pl.BlockSpec((1, tm, D_MODEL), lambda b, i: (b, i, 0))]
    out_shape = [jax.ShapeDtypeStruct((B, MLA_HEADS, T // tq, QK_PAD, tq), BF16),
                 jax.ShapeDtypeStruct((B, MLA_HEADS, T, QK_PAD), BF16),
                 jax.ShapeDtypeStruct((B, MLA_HEADS, V_HEAD, T), BF16),
                 jax.ShapeDtypeStruct((B, T, D_MODEL), BF16)]
    if emit_cache:
        out_specs += [pl.BlockSpec((1, tm, KV_LORA), lambda b, i: (b, i, 0)),
                      pl.BlockSpec((1, tm, QK_ROPE), lambda b, i: (b, i, 0))]
        out_shape += [jax.ShapeDtypeStruct((B, T, KV_LORA), F32),
                      jax.ShapeDtypeStruct((B, T, QK_ROPE), F32)]
    return pl.pallas_call(
        functools.partial(_mla_in_kernel, rope=rope, emit_cache=emit_cache, tq=tq),
        grid=(B, T // tm),
        in_specs=in_specs, out_specs=out_specs, out_shape=out_shape,
        compiler_params=_params("parallel", "arbitrary"),
        name="mla_in",
    )(*args)


def _ctx_expand_kernel(ckv_ref, kr_ref, kup_ref, vupt_ref, k_ref, v_ref):
    ckv_b = ckv_ref[0].astype(BF16)
    kn = _dot(ckv_b, kup_ref[...])
    vt = _dot_nt(vupt_ref[...], ckv_b)
    kr = kr_ref[0].astype(BF16)
    zeros = jnp.zeros_like(kr)
    for hd in range(MLA_HEADS):
        k_ref[0, hd, :, 0:QK_NOPE] = kn[:, hd * QK_NOPE:(hd + 1) * QK_NOPE].astype(BF16)
        k_ref[0, hd, :, QK_NOPE:QK_NOPE + QK_ROPE] = kr
        k_ref[0, hd, :, QK_NOPE + QK_ROPE:QK_PAD] = zeros
        v_ref[0, hd] = vt[hd * V_HEAD:(hd + 1) * V_HEAD, :].astype(BF16)


def _ctx_expand_call(ckv, kr, kup, vupt):
    B, L, _ = ckv.shape
    return pl.pallas_call(
        _ctx_expand_kernel,
        grid=(B,),
        in_specs=[
            pl.BlockSpec((1, L, KV_LORA), lambda b: (b, 0, 0)),
            pl.BlockSpec((1, L, QK_ROPE), lambda b: (b, 0, 0)),
            pl.BlockSpec(kup.shape, lambda b: (0, 0)),
            pl.BlockSpec(vupt.shape, lambda b: (0, 0)),
        ],
        out_specs=[pl.BlockSpec((1, MLA_HEADS, L, QK_PAD), lambda b: (b, 0, 0, 0)),
                   pl.BlockSpec((1, MLA_HEADS, V_HEAD, L), lambda b: (b, 0, 0, 0))],
        out_shape=[jax.ShapeDtypeStruct((B, MLA_HEADS, L, QK_PAD), BF16),
                   jax.ShapeDtypeStruct((B, MLA_HEADS, V_HEAD, L), BF16)],
        compiler_params=_params("parallel"),
        name="ctx_expand",
    )(ckv, kr, kup, vupt)


ATTN_UNROLL = 4
ATTN_LEAD = 6


def _attn_ctx_kernel(q_ref, kc_ref, kl_ref, vc_ref, vl_ref, sg_ref, o_ref, s0_scr, s1_scr):
    nq, _, tq = q_ref.shape[2:]
    T = nq * tq
    Lc = kc_ref.shape[2]
    tiles = ([(kc_ref, vc_ref, j * MXU_DIM) for j in range(Lc // MXU_DIM)]
             + [(kl_ref, vl_ref, j * MXU_DIM) for j in range(T // MXU_DIM)])
    groups = MXU_DIM // F32_SUBLANES

    def score_tile(qb, t, s_dst):
        kref, _, off = tiles[t]
        s = _dot(kref[0, 0, off:off + MXU_DIM, :], qb)
        s_dst[t * MXU_DIM:(t + 1) * MXU_DIM, :] = s

    def tile_max(s_buf, t):
        s = s_buf[t * MXU_DIM:(t + 1) * MXU_DIM, :]
        return jnp.max(s.reshape(groups, F32_SUBLANES, tq), axis=0)

    def step(blk, blk_next, m_cur, s_cur, s_nxt):
        scoring = blk_next is not None
        row = pl.multiple_of(blk * tq, tq)
        if scoring:
            q_next = q_ref[0, 0, blk_next]
            for t in range(ATTN_LEAD):
                score_tile(q_next, t, s_nxt)
        acc = lacc = None
        parts = []
        for t, (_, vref, off) in enumerate(tiles):
            p = jnp.exp2(s_cur[t * MXU_DIM:(t + 1) * MXU_DIM, :] - m_cur)
            ps = jnp.sum(p.reshape(groups, F32_SUBLANES, tq), axis=0)
            lacc = ps if lacc is None else lacc + ps
            pv = _dot(vref[0, 0, :, off:off + MXU_DIM], p.astype(BF16))
            acc = pv if acc is None else acc + pv
            if scoring:
                if t + ATTN_LEAD < len(tiles):
                    score_tile(q_next, t + ATTN_LEAD, s_nxt)
                parts.append(tile_max(s_nxt, t))
        l = jnp.sum(lacc, axis=0, keepdims=True)
        gate = sg_ref[0, pl.ds(row, tq), :].astype(F32)
        o_ref[0, pl.ds(row, tq), :] = ((acc * (1.0 / l)).T * gate).astype(BF16)
        if not scoring:
            return None
        return jnp.max(functools.reduce(jnp.maximum, parts), axis=0, keepdims=True)

    q0 = q_ref[0, 0, 0]
    for t in range(len(tiles)):
        score_tile(q0, t, s0_scr)
    mrun = functools.reduce(jnp.maximum, [tile_max(s0_scr, t) for t in range(len(tiles))])
    m0 = jnp.max(mrun, axis=0, keepdims=True)

    def trip(i, m, last):
        bufs = (s0_scr, s1_scr)
        for u in range(ATTN_UNROLL):
            blk = i * ATTN_UNROLL + u
            blk_next = None if (last and u == ATTN_UNROLL - 1) else blk + 1
            m = step(blk, blk_next, m, bufs[u % 2], bufs[(u + 1) % 2])
        return m

    trips = nq // ATTN_UNROLL
    m = lax.fori_loop(0, trips - 1, lambda i, m: trip(i, m, False), m0)
    trip(trips - 1, m, True)


def _attn_ctx_call(q, kc, kl, vc, vl, sg):
    B, H, nq, _, tq = q.shape
    T = nq * tq
    Lc = kc.shape[2]
    assert nq % ATTN_UNROLL == 0 and ATTN_UNROLL % 2 == 0
    return pl.pallas_call(
        _attn_ctx_kernel,
        scratch_shapes=[pltpu.VMEM((Lc + T, tq), F32), pltpu.VMEM((Lc + T, tq), F32)],
        grid=(B, H),
        in_specs=[
            pl.BlockSpec((1, 1, nq, QK_PAD, tq), lambda b, h: (b, h, 0, 0, 0)),
            pl.BlockSpec((1, 1, Lc, QK_PAD), lambda b, h: (b, h, 0, 0)),
            pl.BlockSpec((1, 1, T, QK_PAD), lambda b, h: (b, h, 0, 0)),
            pl.BlockSpec((1, 1, V_HEAD, Lc), lambda b, h: (b, h, 0, 0)),
            pl.BlockSpec((1, 1, V_HEAD, T), lambda b, h: (b, h, 0, 0)),
            pl.BlockSpec((1, T, V_HEAD), lambda b, h: (b, 0, h)),
        ],
        out_specs=pl.BlockSpec((1, T, V_HEAD), lambda b, h: (b, 0, h)),
        out_shape=jax.ShapeDtypeStruct((B, T, H * V_HEAD), BF16),
        compiler_params=_params("parallel", "arbitrary"),
        name="attn_ctx",
    )(q, kc, kl, vc, vl, sg)


def _attn_self_kernel(q_ref, k_ref, v_ref, sg_ref, o_ref):
    heads = range(MLA_HEADS)
    s = [_dot(k_ref[0, hd], q_ref[0, hd, 0]) for hd in heads]
    p = [jnp.exp2(a - jnp.max(a, axis=0, keepdims=True)) for a in s]
    inv = [1.0 / jnp.sum(a, axis=0, keepdims=True) for a in p]
    o = [_dot(v_ref[0, hd], p[hd].astype(BF16)) * inv[hd] for hd in heads]
    for hd in heads:
        cols = slice(hd * V_HEAD, (hd + 1) * V_HEAD)
        o_ref[0, :, cols] = (o[hd].T * sg_ref[0, :, cols].astype(F32)).astype(BF16)


def _attn_self_call(q, k, v, sg):
    _, H, nseq, _, seq = q.shape
    N = nseq * seq
    return pl.pallas_call(
        _attn_self_kernel,
        grid=(nseq,),
        in_specs=[
            pl.BlockSpec((1, H, 1, QK_PAD, seq), lambda b: (0, 0, b, 0, 0)),
            pl.BlockSpec((1, H, seq, QK_PAD), lambda b: (0, 0, b, 0)),
            pl.BlockSpec((1, H, V_HEAD, seq), lambda b: (0, 0, 0, b)),
            pl.BlockSpec((1, seq, H * V_HEAD), lambda b: (0, b, 0)),
        ],
        out_specs=pl.BlockSpec((1, seq, H * V_HEAD), lambda b: (0, b, 0)),
        out_shape=jax.ShapeDtypeStruct((1, N, H * V_HEAD), BF16),
        compiler_params=_params("parallel"),
        name="attn_self",
    )(q, k, v, sg)


def _mla_out_kernel(o_ref, x_ref, g_ref, wo_ref, fg_ref, y_ref):
    x2 = x_ref[0] + g_ref[0] * _dot(o_ref[0], wo_ref[...])
    y_ref[0] = _rms_rows(x2, fg_ref[...])


def _mla_out_call(o, x, g, w_out, fg, tm):
    B, T, _ = x.shape
    tok = lambda: pl.BlockSpec((1, tm, D_MODEL), lambda b, i: (b, i, 0))
    return pl.pallas_call(
        _mla_out_kernel,
        grid=(B, T // tm),
        in_specs=[tok(), tok(), _mod_spec(g),
                  pl.BlockSpec(w_out.shape, lambda b, i: (0, 0)),
                  pl.BlockSpec((1, D_MODEL), lambda b, i: (0, 0))],
        out_specs=tok(),
        out_shape=jax.ShapeDtypeStruct((B, T, D_MODEL), F32),
        compiler_params=_params("parallel", "arbitrary"),
        name="mla_out",
    )(o, x, g, w_out, fg)


def _swap_halves(w):
    f = QK_ROPE // 4
    return jnp.concatenate([-w[..., f:2 * f], w[..., 0:f], -w[..., 3 * f:4 * f], w[..., 2 * f:3 * f]], axis=-1)


def _rope_table(T):
    f = QK_ROPE // 4
    rows = T // GRID_W
    inv = ROPE_BASE ** (-jnp.arange(f, dtype=F32) / f)
    ar = jnp.arange(rows, dtype=F32)[:, None] * inv
    ac = jnp.arange(GRID_W, dtype=F32)[:, None] * inv

    def by_row(a):
        return jnp.broadcast_to(a[:, None, :], (rows, GRID_W, f)).reshape(T, f)

    def by_col(a):
        return jnp.broadcast_to(a[None, :, :], (rows, GRID_W, f)).reshape(T, f)

    cr, sr, cc, sc = by_row(jnp.cos(ar)), by_row(jnp.sin(ar)), by_col(jnp.cos(ac)), by_col(jnp.sin(ac))
    return jnp.concatenate([cr, cr, cc, cc, sr, sr, sc, sc], axis=-1)


def kernel(x_prompt, x_sample, c, state_ret_fwd, state_ret_bwd, cache_mla_ckv, cache_mla_krope, c_ctx,
           ada_w, ada_b, norm_g, even_in_w, even_conv_w, even_out_w, odd_in_w, odd_q_norm_g,
           odd_kv_norm_g, odd_q_up_w, odd_kv_up_w, odd_out_w, final_norm_g):
    BP, SEQ, D = x_prompt.shape
    BS, TS, _ = x_sample.shape
    NP = BP * SEQ

    cvec = jnp.concatenate([c, c_ctx[None, :], jnp.zeros((ADA_ROWS - BS - 1, D), F32)], axis=0)
    mod = _ada_call(cvec, ada_w, ada_b)

    def mods(l):
        m = mod[l].reshape(ADA_ROWS, 3, 1, D)
        return ([m[BS:BS + 1, i] for i in range(3)], [m[:BS, i] for i in range(3)])

    (sh_p, sc_p, g_p), (sh_s, sc_s, g_s) = mods(0)
    ng = norm_g[0][None, :]
    w_in = even_in_w[0].astype(BF16)
    w_out = even_out_w[0].astype(BF16)
    conv_w = even_conv_w[0]
    mask, dec, cdec = _retention_tables()

    xp1, nsf, nsb = _even_layer_call(x_prompt, sh_p, sc_p, g_p, ng, w_in, conv_w, w_out,
                                     mask, dec, cdec, None, None, SEQ)
    xs1, _, _ = _even_layer_call(x_sample, sh_s, sc_s, g_s, ng, w_in, conv_w, w_out, mask, dec, cdec,
                                 state_ret_fwd[:, 0], state_ret_bwd[:, 0], 512)

    (sh_p, sc_p, g_p), (sh_s, sc_s, g_s) = mods(1)
    ng = norm_g[1][None, :]
    w_in = odd_in_w[0]
    nq = Q_LORA + KV_LORA
    wkr = w_in[:, nq:nq + QK_ROPE]
    wlat = jnp.concatenate([w_in[:, :nq + QK_ROPE], _swap_halves(wkr)], axis=-1).astype(BF16)
    wg = w_in[:, nq + QK_ROPE:].astype(BF16)
    qup = odd_q_up_w[0].T.astype(BF16)
    kvup = odd_kv_up_w[0].reshape(KV_LORA, MLA_HEADS, QK_NOPE + V_HEAD)
    kup = kvup[..., :QK_NOPE].reshape(KV_LORA, MLA_HEADS * QK_NOPE).astype(BF16)
    vupt = kvup[..., QK_NOPE:].reshape(KV_LORA, MLA_HEADS * V_HEAD).T.astype(BF16)
    wts = (wlat, wg, odd_q_norm_g[0][None, :], odd_kv_norm_g[0][None, :], qup, kup, vupt)
    w_out = odd_out_w[0].astype(BF16)
    fg = final_norm_g[None, :]

    xp1f = xp1.reshape(1, NP, D)
    q_p, k_p, v_p, sg_p, ckv_p, kr_p = _mla_in_call(xp1f, sh_p, sc_p, ng, wts, None, 512, SEQ, True)
    q_s, k_s, v_s, sg_s = _mla_in_call(xs1, sh_s, sc_s, ng, wts, _rope_table(TS), 512, MXU_DIM, False)
    k_c, v_c = _ctx_expand_call(cache_mla_ckv[:, 0], cache_mla_krope[:, 0], kup, vupt)

    o_p = _attn_self_call(q_p, k_p, v_p, sg_p)
    o_s = _attn_ctx_call(q_s, k_c, k_s, v_c, v_s, sg_s)

    y_p = _mla_out_call(o_p, xp1f, g_p, w_out, fg, 1024)
    y_s = _mla_out_call(o_s, xs1, g_s, w_out, fg, 1024)

    return (y_p.reshape(BP, SEQ, D), y_s,
            nsf[:, None], nsb[:, None],
            ckv_p.reshape(BP, 1, SEQ, KV_LORA), kr_p.reshape(BP, 1, SEQ, QK_ROPE))
```

```python
import functools
from typing import NamedTuple, Optional

import jax
import jax.numpy as jnp
from jax import lax
from jax.experimental import pallas as pl
from jax.experimental.pallas import tpu as pltpu

F32 = jnp.float32
BF16 = jnp.bfloat16

D_MODEL = 1024
GRID_W = 64
EPS = 1e-6
RET_HEADS = 4
RET_DK = 128
RET_DV = 128
RET_WIDTH = RET_HEADS * RET_DV
RET_CHUNK = 128
RET_BWD_OFFSET = 0.5
CONV_WIDTH = D_MODEL - RET_WIDTH
MLA_HEADS = 8
QK_NOPE = 128
QK_ROPE = 64
V_HEAD = 128
Q_LORA = 384
KV_LORA = 256
ROPE_BASE = 10000.0
QK_PAD = 256
ADA_ROWS = 16
BF16_SUBLANES = 16
F32_SUBLANES = 8
MXU_DIM = 256
LOG2_E = 1.4426950408889634
VMEM_LIMIT = 56 * 1024 * 1024
TOKEN_BLOCK = 512
OUT_BLOCK = 1024


def _silu(x):
    return x * (1.0 / (1.0 + jnp.exp(-x)))


def _rms_rows(x, g):
    return x * lax.rsqrt(jnp.mean(x * x, axis=-1, keepdims=True) + EPS) * g


def _dot(a, b):
    return jnp.dot(a, b, preferred_element_type=F32)


def _dot_nt(a, b):
    return lax.dot_general(a, b, (((1,), (1,)), ((), ())), preferred_element_type=F32)


def _dot_tn(a, b):
    return lax.dot_general(a, b, (((0,), (0,)), ((), ())), preferred_element_type=F32)


def _params(*sem):
    return pltpu.CompilerParams(dimension_semantics=sem, vmem_limit_bytes=VMEM_LIMIT)


def _ada_kernel(c_ref, w_ref, b_ref, o_ref):
    s = _silu(c_ref[...]).astype(BF16)
    o_ref[0] = _dot(s, w_ref[0].astype(BF16)) + b_ref[0]


def _ada_call(cvec, ada_w, ada_b):
    depth = ada_w.shape[0]
    return pl.pallas_call(
        _ada_kernel,
        grid=(depth, 3),
        in_specs=[
            pl.BlockSpec((ADA_ROWS, D_MODEL), lambda l, j: (0, 0)),
            pl.BlockSpec((1, D_MODEL, D_MODEL), lambda l, j: (l, 0, j)),
            pl.BlockSpec((1, 1, D_MODEL), lambda l, j: (l, 0, j)),
        ],
        out_specs=pl.BlockSpec((1, ADA_ROWS, D_MODEL), lambda l, j: (l, 0, j)),
        out_shape=jax.ShapeDtypeStruct((depth, ADA_ROWS, 3 * D_MODEL), F32),
        compiler_params=_params("arbitrary", "arbitrary"),
        name="ada",
    )(cvec, ada_w, ada_b.reshape(depth, 1, 3 * D_MODEL))


class Mod(NamedTuple):
    table: jax.Array
    layer: int
    part: int
    row: Optional[int]


def _mod_spec(m):
    row = (lambda b: b) if m.row is None else (lambda b: m.row)
    return pl.BlockSpec((None, None, 1, 1, D_MODEL), lambda b, *_: (m.layer, row(b), m.part, 0, 0))


def _even_layer_kernel(x_ref, sh_ref, sc_ref, g_ref, ng_ref, w_ref, cw_ref, wo_ref, mask_ref, dec_ref,
                       cdec_ref, s0f_ref, s0b_ref,
                       xo_ref, nsf_ref, nsb_ref,
                       k_all, v_all, z_all, sb_all, sf_run, sb_run, ymix,
                       *, nc, nb, zero_init):
    p = pl.program_id(1)
    j = pl.program_id(2)
    C = RET_CHUNK
    H = RET_HEADS
    dv = RET_DV
    W = RET_WIDTH
    tb = nc * C
    T = nb * tb

    def tile(cc, h):
        return slice(cc * C, (cc + 1) * C), slice(h * dv, (h + 1) * dv)

    def project():
        h = _rms_rows(x_ref[0], ng_ref[...]) * (1.0 + sc_ref[0]) + sh_ref[0]
        hb = h.astype(BF16)
        return lambda g: _dot(hb, w_ref[:, g * W:(g + 1) * W])

    @pl.when((p == 0) & (j == 0))
    def _():
        if zero_init:
            sf_run[...] = jnp.zeros_like(sf_run)
            sb_run[...] = jnp.zeros_like(sb_run)
        else:
            sf_run[...] = s0f_ref[0]
            sb_run[...] = s0b_ref[0]

    @pl.when(p == 0)
    def _():
        blk = nb - 1 - j
        r0 = pl.multiple_of(blk * tb, tb)
        proj = project()
        k = (proj(1) * (RET_DK ** -0.5)).astype(BF16)
        v = proj(2).astype(BF16)
        k_all[pl.ds(r0, tb), :] = k
        v_all[pl.ds(r0, tb), :] = v
        z_all[pl.ds(r0, tb), :] = (proj(5) * proj(6)).astype(BF16)
        for cc in reversed(range(nc)):
            gc = blk * nc + cc
            for h in range(H):
                rows, cols = tile(cc, h)
                sb_all[gc, h] = sb_run[h].astype(BF16)
                kd = (k[rows, cols].astype(F32) * dec_ref[3, h]).astype(BF16)
                sb_run[h] = sb_run[h] * cdec_ref[1, h] + _dot_tn(kd, v[rows, cols])

        @pl.when(j == nb - 1)
        def _():
            nsb_ref[0] = sb_run[...]

    @pl.when(p == 1)
    def _():
        r0 = pl.multiple_of(j * tb, tb)
        proj = project()
        q = proj(0).astype(BF16)
        sga = _silu(proj(3))
        k = k_all[pl.ds(r0, tb), :]
        v = v_all[pl.ds(r0, tb), :]
        items = [(cc, h) for cc in range(nc) for h in range(H)]

        def scores(cc, h):
            rows, cols = tile(cc, h)
            return _dot_nt(q[rows, cols], k[rows, cols])

        s_next = scores(*items[0])
        for i, (cc, h) in enumerate(items):
            s = s_next
            if i + 1 < len(items):
                s_next = scores(*items[i + 1])
            gc = j * nc + cc
            rows, cols = tile(cc, h)
            qh, kh, vh = q[rows, cols], k[rows, cols], v[rows, cols]
            inter = (_dot(qh, sf_run[h].astype(BF16)) * dec_ref[0, h]
                     + _dot(qh, sb_all[gc, h]) * dec_ref[1, h])
            kd = (kh.astype(F32) * dec_ref[2, h]).astype(BF16)
            sf_run[h] = sf_run[h] * cdec_ref[0, h] + _dot_tn(kd, vh)
            att = (s * mask_ref[h]).astype(BF16)
            o = _dot(att, vh) + inter
            on = o * lax.rsqrt(jnp.mean(o * o, axis=-1, keepdims=True) + EPS)
            ymix[rows, cols] = (on * sga[rows, cols]).astype(BF16)

        u = _silu(proj(7)) * proj(4)
        z = z_all[pl.ds(r0, tb), :].astype(F32)
        row = lax.broadcasted_iota(jnp.int32, z.shape, 0)
        n = BF16_SUBLANES
        r_prev = pl.multiple_of(jnp.maximum(r0 - n, 0), n)
        r_next = pl.multiple_of(jnp.minimum(r0 + tb, T - n), n)
        prev_row = z_all[pl.ds(r_prev, n), :][n - 1:n, :].astype(F32)
        next_row = z_all[pl.ds(r_next, n), :][0:1, :].astype(F32)
        prev_row = jnp.where(j > 0, prev_row, 0.0)
        next_row = jnp.where(j < nb - 1, next_row, 0.0)
        z_prev = jnp.where(row == 0, prev_row, pltpu.roll(z, 1, 0))
        z_next = jnp.where(row == tb - 1, next_row, pltpu.roll(z, tb - 1, 0))
        zc = z_prev * cw_ref[0:1, :] + z * cw_ref[1:2, :] + z_next * cw_ref[2:3, :]
        ymix[:, W:] = (u * zc).astype(BF16)

        y = _dot(ymix[...], wo_ref[...])
        xo_ref[0] = x_ref[0] + g_ref[0] * y

        @pl.when(j == nb - 1)
        def _():
            nsf_ref[0] = sf_run[...]


def _even_layer_call(x, sh, sc, g, ng, w_in, conv_w, w_out, mask, dec, cdec, s0f, s0b, tb):
    B, T, _ = x.shape
    nb = T // tb
    nc = tb // RET_CHUNK
    zero_init = s0f is None
    state_block = (1, RET_HEADS, RET_DK, RET_DV)
    if zero_init:
        s0f = s0b = jnp.zeros(state_block, F32)
        s0_spec = pl.BlockSpec(state_block, lambda b, p, j: (0, 0, 0, 0))
    else:
        s0_spec = pl.BlockSpec(state_block, lambda b, p, j: (b, 0, 0, 0))

    def const(a):
        return pl.BlockSpec(a.shape, lambda b, p, j: (0,) * a.ndim, pipeline_mode=pl.Buffered(1))

    kernel = functools.partial(_even_layer_kernel, nc=nc, nb=nb, zero_init=zero_init)
    state_shape = jax.ShapeDtypeStruct((B,) + state_block[1:], F32)
    state_spec = pl.BlockSpec(state_block, lambda b, p, j: (b, 0, 0, 0))
    seq_buf = pltpu.VMEM((T, RET_WIDTH), BF16)
    return pl.pallas_call(
        kernel,
        grid=(B, 2, nb),
        in_specs=[
            pl.BlockSpec((1, tb, D_MODEL), lambda b, p, j: (b, p * j + (1 - p) * (nb - 1 - j), 0)),
            _mod_spec(sh), _mod_spec(sc), _mod_spec(g),
            const(ng), const(w_in), const(conv_w), const(w_out), const(mask), const(dec),
            pl.BlockSpec(memory_space=pltpu.SMEM),
            s0_spec, s0_spec,
        ],
        out_specs=[
            pl.BlockSpec((1, tb, D_MODEL), lambda b, p, j: (b, p * j, 0)),
            state_spec, state_spec,
        ],
        out_shape=[jax.ShapeDtypeStruct((B, T, D_MODEL), F32), state_shape, state_shape],
        scratch_shapes=[
            seq_buf, seq_buf, seq_buf,
            pltpu.VMEM((T // RET_CHUNK, RET_HEADS, RET_DK, RET_DV), BF16),
            pltpu.VMEM((RET_HEADS, RET_DK, RET_DV), F32),
            pltpu.VMEM((RET_HEADS, RET_DK, RET_DV), F32),
            pltpu.VMEM((tb, D_MODEL), BF16),
        ],
        compiler_params=_params("parallel", "arbitrary", "arbitrary"),
        name="even_layer",
    )(x, sh.table, sc.table, g.table, ng, w_in, conv_w, w_out, mask, dec, cdec, s0f, s0b)


def _retention_tables():
    C = RET_CHUNK
    hh = jnp.arange(RET_HEADS, dtype=F32)
    lg_f = jnp.log(1.0 - 2.0 ** (-5.0 - hh))
    lg_b = jnp.log(1.0 - 2.0 ** (-5.0 - hh - RET_BWD_OFFSET))
    idx = jnp.arange(C, dtype=F32)
    diff = idx[:, None] - idx[None, :]
    m_f = jnp.where(diff >= 0, jnp.exp(lg_f[:, None, None] * jnp.maximum(diff, 0.0)), 0.0)
    m_b = jnp.where(diff <= 0, jnp.exp(lg_b[:, None, None] * jnp.maximum(-diff, 0.0)), 0.0)
    mask = m_f + m_b
    q_f = jnp.exp(lg_f[:, None] * (idx[None, :] + 1.0))
    q_b = jnp.exp(lg_b[:, None] * (C - idx[None, :]))
    k_f = jnp.exp(lg_f[:, None] * (C - 1.0 - idx[None, :]))
    k_b = jnp.exp(lg_b[:, None] * idx[None, :])
    dec = jnp.stack([q_f, q_b, k_f, k_b])
    dec = jnp.broadcast_to(dec[..., None], dec.shape + (RET_DV,))
    cdec = jnp.stack([jnp.exp(lg_f * C), jnp.exp(lg_b * C)])
    return mask, dec, cdec


def _rope_mix(r, cs_ref, first):
    if cs_ref is not None:
        t = r * cs_ref[...]
        r = t + pltpu.roll(t, QK_ROPE, 1)
    return jnp.where(first, r, 0.0)


def _mla_in_kernel(*refs, rope, emit_cache, tq):
    (x_ref, sh_ref, sc_ref, ng_ref, wlat_ref, wg_ref, qng_ref, kvng_ref, qupt_ref,
     kup_ref, vupt_ref) = refs[:11]
    pos = 11
    cs_ref = cst_ref = None
    if rope:
        cs_ref, cst_ref = refs[pos:pos + 2]
        pos += 2
    q_ref, k_ref, v_ref, sg_ref = refs[pos:pos + 4]
    pos += 4
    if emit_cache:
        ckv_ref, kr_ref = refs[pos:pos + 2]

    h = _rms_rows(x_ref[0], ng_ref[...]) * (1.0 + sc_ref[0]) + sh_ref[0]
    hb = h.astype(BF16)
    lat = _dot(hb, wlat_ref[...])
    kr2 = lat[:, Q_LORA + KV_LORA:]
    sg_ref[0] = _silu(_dot(hb, wg_ref[...])).astype(BF16)
    qn = _rms_rows(lat[:, :Q_LORA], qng_ref[...]).astype(BF16)
    ckv = _rms_rows(lat[:, Q_LORA:Q_LORA + KV_LORA], kvng_ref[...])
    qt = _dot_nt(qupt_ref[...], qn)
    ckv_b = ckv.astype(BF16)
    kn = _dot(ckv_b, kup_ref[...])
    vt = _dot_nt(vupt_ref[...], ckv_b)

    scale = (QK_NOPE + QK_ROPE) ** -0.5 * LOG2_E
    tm = kr2.shape[0]
    first = lax.broadcasted_iota(jnp.int32, kr2.shape, 1) < QK_ROPE
    k_rot = _rope_mix(kr2, cs_ref, first).astype(BF16)
    q_pad = jnp.zeros((QK_PAD - QK_NOPE - QK_ROPE, tq), BF16)
    f = QK_ROPE // 4
    for hd in range(MLA_HEADS):
        base = hd * (QK_NOPE + QK_ROPE)
        q_nope = (qt[base:base + QK_NOPE, :] * scale).astype(BF16)
        q_rot = qt[base + QK_NOPE:base + QK_NOPE + QK_ROPE, :]
        if rope:
            partner = jnp.concatenate([q_rot[f:2 * f], q_rot[0:f], q_rot[3 * f:4 * f], q_rot[2 * f:3 * f]], axis=0)
            q_rot = q_rot * cst_ref[0:QK_ROPE, :] + partner * cst_ref[QK_ROPE:2 * QK_ROPE, :]
        q_rot = (q_rot * scale).astype(BF16)
        for j in range(tm // tq):
            cols = slice(j * tq, (j + 1) * tq)
            q_ref[0, hd, j, 0:QK_NOPE, :] = q_nope[:, cols]
            q_ref[0, hd, j, QK_NOPE:QK_NOPE + QK_ROPE, :] = q_rot[:, cols]
            q_ref[0, hd, j, QK_NOPE + QK_ROPE:QK_PAD, :] = q_pad
        k_ref[0, hd, :, 0:QK_NOPE] = kn[:, hd * QK_NOPE:(hd + 1) * QK_NOPE].astype(BF16)
        k_ref[0, hd, :, QK_NOPE:QK_PAD] = k_rot
        v_ref[0, hd] = vt[hd * V_HEAD:(hd + 1) * V_HEAD, :].astype(BF16)
    if emit_cache:
        ckv_ref[0] = ckv
        kr_ref[0] = kr2[:, :QK_ROPE]


def _mla_in_call(x, sh, sc, ng, wts, cs, tm, tq, emit_cache):
    B, T, _ = x.shape
    rope = cs is not None
    wlat, wg, qng, kvng, qup, kup, vupt = wts

    def full(a):
        return pl.BlockSpec(a.shape, lambda b, i: (0,) * a.ndim)

    in_specs = [
        pl.BlockSpec((1, tm, D_MODEL), lambda b, i: (b, i, 0)),
        _mod_spec(sh), _mod_spec(sc), full(ng),
        full(wlat), full(wg), full(qng), full(kvng), full(qup), full(kup), full(vupt),
    ]
    args = [x, sh.table, sc.table, ng, wlat, wg, qng, kvng, qup, kup, vupt]
    if rope:
        in_specs += [pl.BlockSpec((tm, 2 * QK_ROPE), lambda b, i: (i, 0)),
                     pl.BlockSpec((2 * QK_ROPE, tm), lambda b, i: (0, i))]
        f = QK_ROPE // 4
        sign = jnp.concatenate([jnp.ones((QK_ROPE,), F32)] + [-jnp.ones((f,), F32), jnp.ones((f,), F32)] * 2)
        args += [cs, (cs * sign).T]
    out_specs = [pl.BlockSpec((1, MLA_HEADS, tm // tq, QK_PAD, tq), lambda b, i: (b, 0, i, 0, 0)),
                 pl.BlockSpec((1, MLA_HEADS, tm, QK_PAD), lambda b, i: (b, 0, i, 0)),
                 pl.BlockSpec((1, MLA_HEADS, V_HEAD, tm), lambda b, i: (b, 0, 0, i)),
                 pl.BlockSpec((1, tm, D_MODEL), lambda b, i: (b, i, 0))]
    out_shape = [jax.ShapeDtypeStruct((B, MLA_HEADS, T // tq, QK_PAD, tq), BF16),
                 jax.ShapeDtypeStruct((B, MLA_HEADS, T, QK_PAD), BF16),
                 jax.ShapeDtypeStruct((B, MLA_HEADS, V_HEAD, T), BF16),
                 jax.ShapeDtypeStruct((B, T, D_MODEL), BF16)]
    if emit_cache:
        out_specs += [pl.BlockSpec((1, tm, KV_LORA), lambda b, i: (b, i, 0)),
                      pl.BlockSpec((1, tm, QK_ROPE), lambda b, i: (b, i, 0))]
        out_shape += [jax.ShapeDtypeStruct((B, T, KV_LORA), F32),
                      jax.ShapeDtypeStruct((B, T, QK_ROPE), F32)]
    return pl.pallas_call(
        functools.partial(_mla_in_kernel, rope=rope, emit_cache=emit_cache, tq=tq),
        grid=(B, T // tm),
        in_specs=in_specs, out_specs=out_specs, out_shape=out_shape,
        compiler_params=_params("parallel", "arbitrary"),
        name="mla_in",
    )(*args)


def _ctx_expand_kernel(ckv_ref, kr_ref, kup_ref, vupt_ref, k_ref, v_ref):
    ckv_b = ckv_ref[0].astype(BF16)
    kn = _dot(ckv_b, kup_ref[...])
    vt = _dot_nt(vupt_ref[...], ckv_b)
    kr = kr_ref[0].astype(BF16)
    zeros = jnp.zeros_like(kr)
    for hd in range(MLA_HEADS):
        k_ref[0, hd, :, 0:QK_NOPE] = kn[:, hd * QK_NOPE:(hd + 1) * QK_NOPE].astype(BF16)
        k_ref[0, hd, :, QK_NOPE:QK_NOPE + QK_ROPE] = kr
        k_ref[0, hd, :, QK_NOPE + QK_ROPE:QK_PAD] = zeros
        v_ref[0, hd] = vt[hd * V_HEAD:(hd + 1) * V_HEAD, :].astype(BF16)


def _ctx_expand_call(ckv, kr, kup, vupt):
    B, L, _ = ckv.shape
    return pl.pallas_call(
        _ctx_expand_kernel,
        grid=(B,),
        in_specs=[
            pl.BlockSpec((1, L, KV_LORA), lambda b: (b, 0, 0)),
            pl.BlockSpec((1, L, QK_ROPE), lambda b: (b, 0, 0)),
            pl.BlockSpec(kup.shape, lambda b: (0, 0)),
            pl.BlockSpec(vupt.shape, lambda b: (0, 0)),
        ],
        out_specs=[pl.BlockSpec((1, MLA_HEADS, L, QK_PAD), lambda b: (b, 0, 0, 0)),
                   pl.BlockSpec((1, MLA_HEADS, V_HEAD, L), lambda b: (b, 0, 0, 0))],
        out_shape=[jax.ShapeDtypeStruct((B, MLA_HEADS, L, QK_PAD), BF16),
                   jax.ShapeDtypeStruct((B, MLA_HEADS, V_HEAD, L), BF16)],
        compiler_params=_params("parallel"),
        name="ctx_expand",
    )(ckv, kr, kup, vupt)


ATTN_UNROLL = 4
ATTN_LEAD = 6


def _attn_ctx_kernel(q_ref, kc_ref, kl_ref, vc_ref, vl_ref, sg_ref, o_ref, s0_scr, s1_scr):
    nq, _, tq = q_ref.shape[2:]
    T = nq * tq
    Lc = kc_ref.shape[2]
    tiles = ([(kc_ref, vc_ref, j * MXU_DIM) for j in range(Lc // MXU_DIM)]
             + [(kl_ref, vl_ref, j * MXU_DIM) for j in range(T // MXU_DIM)])
    groups = MXU_DIM // F32_SUBLANES

    def score_tile(qb, t, s_dst):
        kref, _, off = tiles[t]
        s = _dot(kref[0, 0, off:off + MXU_DIM, :], qb)
        s_dst[t * MXU_DIM:(t + 1) * MXU_DIM, :] = s

    def tile_max(s_buf, t):
        s = s_buf[t * MXU_DIM:(t + 1) * MXU_DIM, :]
        return jnp.max(s.reshape(groups, F32_SUBLANES, tq), axis=0)

    def step(blk, blk_next, m_cur, s_cur, s_nxt):
        scoring = blk_next is not None
        row = pl.multiple_of(blk * tq, tq)
        if scoring:
            q_next = q_ref[0, 0, blk_next]
            for t in range(ATTN_LEAD):
                score_tile(q_next, t, s_nxt)
        acc = lacc = None
        parts = []
        for t, (_, vref, off) in enumerate(tiles):
            p = jnp.exp2(s_cur[t * MXU_DIM:(t + 1) * MXU_DIM, :] - m_cur)
            ps = jnp.sum(p.reshape(groups, F32_SUBLANES, tq), axis=0)
            lacc = ps if lacc is None else lacc + ps
            pv = _dot(vref[0, 0, :, off:off + MXU_DIM], p.astype(BF16))
            acc = pv if acc is None else acc + pv
            if scoring:
                if t + ATTN_LEAD < len(tiles):
                    score_tile(q_next, t + ATTN_LEAD, s_nxt)
                parts.append(tile_max(s_nxt, t))
        l = jnp.sum(lacc, axis=0, keepdims=True)
        gate = sg_ref[0, pl.ds(row, tq), :].astype(F32)
        o_ref[0, pl.ds(row, tq), :] = ((acc * (1.0 / l)).T * gate).astype(BF16)
        if not scoring:
            return None
        return jnp.max(functools.reduce(jnp.maximum, parts), axis=0, keepdims=True)

    q0 = q_ref[0, 0, 0]
    for t in range(len(tiles)):
        score_tile(q0, t, s0_scr)
    mrun = functools.reduce(jnp.maximum, [tile_max(s0_scr, t) for t in range(len(tiles))])
    m0 = jnp.max(mrun, axis=0, keepdims=True)

    def trip(i, m, last):
        bufs = (s0_scr, s1_scr)
        for u in range(ATTN_UNROLL):
            blk = i * ATTN_UNROLL + u
            blk_next = None if (last and u == ATTN_UNROLL - 1) else blk + 1
            m = step(blk, blk_next, m, bufs[u % 2], bufs[(u + 1) % 2])
        return m

    trips = nq // ATTN_UNROLL
    m = lax.fori_loop(0, trips - 1, lambda i, m: trip(i, m, False), m0)
    trip(trips - 1, m, True)


def _attn_ctx_call(q, kc, kl, vc, vl, sg):
    B, H, nq, _, tq = q.shape
    T = nq * tq
    Lc = kc.shape[2]
    assert nq % ATTN_UNROLL == 0 and ATTN_UNROLL % 2 == 0
    return pl.pallas_call(
        _attn_ctx_kernel,
        scratch_shapes=[pltpu.VMEM((Lc + T, tq), F32), pltpu.VMEM((Lc + T, tq), F32)],
        grid=(B, H),
        in_specs=[
            pl.BlockSpec((1, 1, nq, QK_PAD, tq), lambda b, h: (b, h, 0, 0, 0)),
            pl.BlockSpec((1, 1, Lc, QK_PAD), lambda b, h: (b, h, 0, 0)),
            pl.BlockSpec((1, 1, T, QK_PAD), lambda b, h: (b, h, 0, 0)),
            pl.BlockSpec((1, 1, V_HEAD, Lc), lambda b, h: (b, h, 0, 0)),
            pl.BlockSpec((1, 1, V_HEAD, T), lambda b, h: (b, h, 0, 0)),
            pl.BlockSpec((1, T, V_HEAD), lambda b, h: (b, 0, h)),
        ],
        out_specs=pl.BlockSpec((1, T, V_HEAD), lambda b, h: (b, 0, h)),
        out_shape=jax.ShapeDtypeStruct((B, T, H * V_HEAD), BF16),
        compiler_params=_params("parallel", "arbitrary"),
        name="attn_ctx",
    )(q, kc, kl, vc, vl, sg)


def _attn_self_kernel(q_ref, k_ref, v_ref, sg_ref, o_ref):
    heads = range(MLA_HEADS)
    s = [_dot(k_ref[0, hd], q_ref[0, hd, 0]) for hd in heads]
    p = [jnp.exp2(a - jnp.max(a, axis=0, keepdims=True)) for a in s]
    inv = [1.0 / jnp.sum(a, axis=0, keepdims=True) for a in p]
    o = [_dot(v_ref[0, hd], p[hd].astype(BF16)) * inv[hd] for hd in heads]
    for hd in heads:
        cols = slice(hd * V_HEAD, (hd + 1) * V_HEAD)
        o_ref[0, :, cols] = (o[hd].T * sg_ref[0, :, cols].astype(F32)).astype(BF16)


def _attn_self_call(q, k, v, sg):
    _, H, nseq, _, seq = q.shape
    N = nseq * seq
    return pl.pallas_call(
        _attn_self_kernel,
        grid=(nseq,),
        in_specs=[
            pl.BlockSpec((1, H, 1, QK_PAD, seq), lambda b: (0, 0, b, 0, 0)),
            pl.BlockSpec((1, H, seq, QK_PAD), lambda b: (0, 0, b, 0)),
            pl.BlockSpec((1, H, V_HEAD, seq), lambda b: (0, 0, 0, b)),
            pl.BlockSpec((1, seq, H * V_HEAD), lambda b: (0, b, 0)),
        ],
        out_specs=pl.BlockSpec((1, seq, H * V_HEAD), lambda b: (0, b, 0)),
        out_shape=jax.ShapeDtypeStruct((1, N, H * V_HEAD), BF16),
        compiler_params=_params("parallel"),
        name="attn_self",
    )(q, k, v, sg)


def _mla_out_kernel(o_ref, x_ref, g_ref, wo_ref, fg_ref, y_ref):
    x2 = x_ref[0] + g_ref[0] * _dot(o_ref[0], wo_ref[...])
    y_ref[0] = _rms_rows(x2, fg_ref[...])


def _mla_out_call(o, x, g, w_out, fg, tm):
    B, T, _ = x.shape
    tok = lambda: pl.BlockSpec((1, tm, D_MODEL), lambda b, i: (b, i, 0))
    return pl.pallas_call(
        _mla_out_kernel,
        grid=(B, T // tm),
        in_specs=[tok(), tok(), _mod_spec(g),
                  pl.BlockSpec(w_out.shape, lambda b, i: (0, 0)),
                  pl.BlockSpec((1, D_MODEL), lambda b, i: (0, 0))],
        out_specs=tok(),
        out_shape=jax.ShapeDtypeStruct((B, T, D_MODEL), F32),
        compiler_params=_params("parallel", "arbitrary"),
        name="mla_out",
    )(o, x, g.table, w_out, fg)


def _swap_halves(w):
    f = QK_ROPE // 4
    return jnp.concatenate([-w[..., f:2 * f], w[..., 0:f], -w[..., 3 * f:4 * f], w[..., 2 * f:3 * f]], axis=-1)


def _rope_table(T):
    f = QK_ROPE // 4
    rows = T // GRID_W
    inv = ROPE_BASE ** (-jnp.arange(f, dtype=F32) / f)
    ar = jnp.arange(rows, dtype=F32)[:, None] * inv
    ac = jnp.arange(GRID_W, dtype=F32)[:, None] * inv

    def by_row(a):
        return jnp.broadcast_to(a[:, None, :], (rows, GRID_W, f)).reshape(T, f)

    def by_col(a):
        return jnp.broadcast_to(a[None, :, :], (rows, GRID_W, f)).reshape(T, f)

    cr, sr, cc, sc = by_row(jnp.cos(ar)), by_row(jnp.sin(ar)), by_col(jnp.cos(ac)), by_col(jnp.sin(ac))
    return jnp.concatenate([cr, cr, cc, cc, sr, sr, sc, sc], axis=-1)


def kernel(x_prompt, x_sample, c, state_ret_fwd, state_ret_bwd, cache_mla_ckv, cache_mla_krope, c_ctx,
           ada_w, ada_b, norm_g, even_in_w, even_conv_w, even_out_w, odd_in_w, odd_q_norm_g,
           odd_kv_norm_g, odd_q_up_w, odd_kv_up_w, odd_out_w, final_norm_g):
    BP, SEQ, D = x_prompt.shape
    BS, TS, _ = x_sample.shape
    NP = BP * SEQ

    cvec = jnp.concatenate([c, c_ctx[None, :], jnp.zeros((ADA_ROWS - BS - 1, D), F32)], axis=0)
    mod = _ada_call(cvec, ada_w, ada_b).reshape(-1, ADA_ROWS, 3, 1, D)

    def mods(l):
        return ([Mod(mod, l, i, BS) for i in range(3)], [Mod(mod, l, i, None) for i in range(3)])

    (sh_p, sc_p, g_p), (sh_s, sc_s, g_s) = mods(0)
    ng = norm_g[0][None, :]
    w_in = even_in_w[0].astype(BF16)
    w_out = even_out_w[0].astype(BF16)
    conv_w = even_conv_w[0]
    mask, dec, cdec = _retention_tables()

    xp1, nsf, nsb = _even_layer_call(x_prompt, sh_p, sc_p, g_p, ng, w_in, conv_w, w_out,
                                     mask, dec, cdec, None, None, SEQ)
    xs1, _, _ = _even_layer_call(x_sample, sh_s, sc_s, g_s, ng, w_in, conv_w, w_out, mask, dec, cdec,
                                 state_ret_fwd[:, 0], state_ret_bwd[:, 0], TOKEN_BLOCK)

    (sh_p, sc_p, g_p), (sh_s, sc_s, g_s) = mods(1)
    ng = norm_g[1][None, :]
    w_in = odd_in_w[0]
    nq = Q_LORA + KV_LORA
    wkr = w_in[:, nq:nq + QK_ROPE]
    wlat = jnp.concatenate([w_in[:, :nq + QK_ROPE], _swap_halves(wkr)], axis=-1).astype(BF16)
    wg = w_in[:, nq + QK_ROPE:].astype(BF16)
    qup = odd_q_up_w[0].T.astype(BF16)
    kvup = odd_kv_up_w[0].reshape(KV_LORA, MLA_HEADS, QK_NOPE + V_HEAD)
    kup = kvup[..., :QK_NOPE].reshape(KV_LORA, MLA_HEADS * QK_NOPE).astype(BF16)
    vupt = kvup[..., QK_NOPE:].reshape(KV_LORA, MLA_HEADS * V_HEAD).T.astype(BF16)
    wts = (wlat, wg, odd_q_norm_g[0][None, :], odd_kv_norm_g[0][None, :], qup, kup, vupt)
    w_out = odd_out_w[0].astype(BF16)
    fg = final_norm_g[None, :]

    xp1f = xp1.reshape(1, NP, D)
    q_p, k_p, v_p, sg_p, ckv_p, kr_p = _mla_in_call(xp1f, sh_p, sc_p, ng, wts, None, TOKEN_BLOCK, SEQ, True)
    q_s, k_s, v_s, sg_s = _mla_in_call(xs1, sh_s, sc_s, ng, wts, _rope_table(TS), TOKEN_BLOCK, MXU_DIM, False)
    k_c, v_c = _ctx_expand_call(cache_mla_ckv[:, 0], cache_mla_krope[:, 0], kup, vupt)

    o_p = _attn_self_call(q_p, k_p, v_p, sg_p)
    o_s = _attn_ctx_call(q_s, k_c, k_s, v_c, v_s, sg_s)

    y_p = _mla_out_call(o_p, xp1f, g_p, w_out, fg, OUT_BLOCK)
    y_s = _mla_out_call(o_s, xs1, g_s, w_out, fg, OUT_BLOCK)

    return (y_p.reshape(BP, SEQ, D), y_s,
            nsf[:, None], nsb[:, None],
            ckv_p.reshape(BP, 1, SEQ, KV_LORA), kr_p.reshape(BP, 1, SEQ, QK_ROPE))
```

```python
import functools
from typing import NamedTuple, Optional

import jax
import jax.numpy as jnp
from jax import lax
from jax.experimental import pallas as pl
from jax.experimental.pallas import tpu as pltpu

F32 = jnp.float32
BF16 = jnp.bfloat16

D_MODEL = 1024
GRID_W = 64
EPS = 1e-6
RET_HEADS = 4
RET_DK = 128
RET_DV = 128
RET_WIDTH = RET_HEADS * RET_DV
RET_CHUNK = 128
RET_BWD_OFFSET = 0.5
CONV_WIDTH = D_MODEL - RET_WIDTH
MLA_HEADS = 8
QK_NOPE = 128
QK_ROPE = 64
V_HEAD = 128
Q_LORA = 384
KV_LORA = 256
ROPE_BASE = 10000.0
QK_PAD = 256
ADA_ROWS = 16
BF16_SUBLANES = 16
F32_SUBLANES = 8
MXU_DIM = 256
LOG2_E = 1.4426950408889634
VMEM_LIMIT = 56 * 1024 * 1024
TOKEN_BLOCK = 512
OUT_BLOCK = 1024


def _silu(x):
    return x * (1.0 / (1.0 + jnp.exp(-x)))


def _rms_rows(x, g):
    return x * lax.rsqrt(jnp.mean(x * x, axis=-1, keepdims=True) + EPS) * g


def _dot(a, b):
    return jnp.dot(a, b, preferred_element_type=F32)


def _dot_nt(a, b):
    return lax.dot_general(a, b, (((1,), (1,)), ((), ())), preferred_element_type=F32)


def _dot_tn(a, b):
    return lax.dot_general(a, b, (((0,), (0,)), ((), ())), preferred_element_type=F32)


def _params(*sem):
    return pltpu.CompilerParams(dimension_semantics=sem, vmem_limit_bytes=VMEM_LIMIT)


def _ada_kernel(c_ref, w_ref, b_ref, o_ref):
    s = _silu(c_ref[...]).astype(BF16)
    o_ref[0] = _dot(s, w_ref[0].astype(BF16)) + b_ref[0]


def _ada_call(cvec, ada_w, ada_b):
    depth = ada_w.shape[0]
    return pl.pallas_call(
        _ada_kernel,
        grid=(depth, 3),
        in_specs=[
            pl.BlockSpec((ADA_ROWS, D_MODEL), lambda l, j: (0, 0)),
            pl.BlockSpec((1, D_MODEL, D_MODEL), lambda l, j: (l, 0, j)),
            pl.BlockSpec((1, 1, D_MODEL), lambda l, j: (l, 0, j)),
        ],
        out_specs=pl.BlockSpec((1, ADA_ROWS, D_MODEL), lambda l, j: (l, 0, j)),
        out_shape=jax.ShapeDtypeStruct((depth, ADA_ROWS, 3 * D_MODEL), F32),
        compiler_params=_params("arbitrary", "arbitrary"),
        name="ada",
    )(cvec, ada_w, ada_b.reshape(depth, 1, 3 * D_MODEL))


class Mod(NamedTuple):
    table: jax.Array
    layer: int
    part: int
    row: Optional[int]


def _mod_spec(m):
    row = (lambda b: b) if m.row is None else (lambda b: m.row)
    return pl.BlockSpec((None, None, 1, 1, D_MODEL), lambda b, *_: (m.layer, row(b), m.part, 0, 0))


def _even_layer_kernel(x_ref, sh_ref, sc_ref, g_ref, ng_ref, w_ref, cw_ref, wo_ref, mask_ref, dec_ref,
                       cdec_ref, s0f_ref, s0b_ref,
                       xo_ref, nsf_ref, nsb_ref,
                       k_all, v_all, z_all, sb_all, sf_run, sb_run, ymix,
                       *, nc, nb, zero_init):
    p = pl.program_id(1)
    j = pl.program_id(2)
    C = RET_CHUNK
    H = RET_HEADS
    dv = RET_DV
    W = RET_WIDTH
    tb = nc * C
    T = nb * tb

    def tile(cc, h):
        return slice(cc * C, (cc + 1) * C), slice(h * dv, (h + 1) * dv)

    def project():
        h = _rms_rows(x_ref[0], ng_ref[...]) * (1.0 + sc_ref[0]) + sh_ref[0]
        hb = h.astype(BF16)
        return lambda g: _dot(hb, w_ref[:, g * W:(g + 1) * W])

    @pl.when((p == 0) & (j == 0))
    def _():
        if zero_init:
            sf_run[...] = jnp.zeros_like(sf_run)
            sb_run[...] = jnp.zeros_like(sb_run)
        else:
            sf_run[...] = s0f_ref[0]
            sb_run[...] = s0b_ref[0]

    @pl.when(p == 0)
    def _():
        blk = nb - 1 - j
        r0 = pl.multiple_of(blk * tb, tb)
        proj = project()
        k = (proj(1) * (RET_DK ** -0.5)).astype(BF16)
        v = proj(2).astype(BF16)
        k_all[pl.ds(r0, tb), :] = k
        v_all[pl.ds(r0, tb), :] = v
        z_all[pl.ds(r0, tb), :] = (proj(5) * proj(6)).astype(BF16)
        for cc in reversed(range(nc)):
            gc = blk * nc + cc
            for h in range(H):
                rows, cols = tile(cc, h)
                sb_all[gc, h] = sb_run[h].astype(BF16)
                kd = (k[rows, cols].astype(F32) * dec_ref[3, h]).astype(BF16)
                sb_run[h] = sb_run[h] * cdec_ref[1, h] + _dot_tn(kd, v[rows, cols])

        @pl.when(j == nb - 1)
        def _():
            nsb_ref[0] = sb_run[...]

    @pl.when(p == 1)
    def _():
        r0 = pl.multiple_of(j * tb, tb)
        proj = project()
        q = proj(0).astype(BF16)
        sga = _silu(proj(3))
        k = k_all[pl.ds(r0, tb), :]
        v = v_all[pl.ds(r0, tb), :]
        items = [(cc, h) for cc in range(nc) for h in range(H)]

        def scores(cc, h):
            rows, cols = tile(cc, h)
            return _dot_nt(q[rows, cols], k[rows, cols])

        s_next = scores(*items[0])
        for i, (cc, h) in enumerate(items):
            s = s_next
            if i + 1 < len(items):
                s_next = scores(*items[i + 1])
            gc = j * nc + cc
            rows, cols = tile(cc, h)
            qh, kh, vh = q[rows, cols], k[rows, cols], v[rows, cols]
            inter = (_dot(qh, sf_run[h].astype(BF16)) * dec_ref[0, h]
                     + _dot(qh, sb_all[gc, h]) * dec_ref[1, h])
            kd = (kh.astype(F32) * dec_ref[2, h]).astype(BF16)
            sf_run[h] = sf_run[h] * cdec_ref[0, h] + _dot_tn(kd, vh)
            att = (s * mask_ref[h]).astype(BF16)
            o = _dot(att, vh) + inter
            on = o * lax.rsqrt(jnp.mean(o * o, axis=-1, keepdims=True) + EPS)
            ymix[rows, cols] = (on * sga[rows, cols]).astype(BF16)

        u = _silu(proj(7)) * proj(4)
        z = z_all[pl.ds(r0, tb), :].astype(F32)
        row = lax.broadcasted_iota(jnp.int32, z.shape, 0)
        n = BF16_SUBLANES
        r_prev = pl.multiple_of(jnp.maximum(r0 - n, 0), n)
        r_next = pl.multiple_of(jnp.minimum(r0 + tb, T - n), n)
        prev_row = z_all[pl.ds(r_prev, n), :][n - 1:n, :].astype(F32)
        next_row = z_all[pl.ds(r_next, n), :][0:1, :].astype(F32)
        prev_row = jnp.where(j > 0, prev_row, 0.0)
        next_row = jnp.where(j < nb - 1, next_row, 0.0)
        z_prev = jnp.where(row == 0, prev_row, pltpu.roll(z, 1, 0))
        z_next = jnp.where(row == tb - 1, next_row, pltpu.roll(z, tb - 1, 0))
        zc = z_prev * cw_ref[0:1, :] + z * cw_ref[1:2, :] + z_next * cw_ref[2:3, :]
        ymix[:, W:] = (u * zc).astype(BF16)

        y = _dot(ymix[...], wo_ref[...])
        xo_ref[0] = x_ref[0] + g_ref[0] * y

        @pl.when(j == nb - 1)
        def _():
            nsf_ref[0] = sf_run[...]


def _even_layer_call(x, sh, sc, g, ng, w_in, conv_w, w_out, mask, dec, cdec, s0f, s0b, tb):
    B, T, _ = x.shape
    nb = T // tb
    nc = tb // RET_CHUNK
    zero_init = s0f is None
    state_block = (1, RET_HEADS, RET_DK, RET_DV)
    if zero_init:
        s0f = s0b = jnp.zeros(state_block, F32)
        s0_spec = pl.BlockSpec(state_block, lambda b, p, j: (0, 0, 0, 0))
    else:
        s0_spec = pl.BlockSpec(state_block, lambda b, p, j: (b, 0, 0, 0))

    def const(a):
        return pl.BlockSpec(a.shape, lambda b, p, j: (0,) * a.ndim, pipeline_mode=pl.Buffered(1))

    kernel = functools.partial(_even_layer_kernel, nc=nc, nb=nb, zero_init=zero_init)
    state_shape = jax.ShapeDtypeStruct((B,) + state_block[1:], F32)
    state_spec = pl.BlockSpec(state_block, lambda b, p, j: (b, 0, 0, 0))
    seq_buf = pltpu.VMEM((T, RET_WIDTH), BF16)
    return pl.pallas_call(
        kernel,
        grid=(B, 2, nb),
        in_specs=[
            pl.BlockSpec((1, tb, D_MODEL), lambda b, p, j: (b, p * j + (1 - p) * (nb - 1 - j), 0)),
            _mod_spec(sh), _mod_spec(sc), _mod_spec(g),
            const(ng), const(w_in), const(conv_w), const(w_out), const(mask), const(dec),
            pl.BlockSpec(memory_space=pltpu.SMEM),
            s0_spec, s0_spec,
        ],
        out_specs=[
            pl.BlockSpec((1, tb, D_MODEL), lambda b, p, j: (b, p * j, 0)),
            state_spec, state_spec,
        ],
        out_shape=[jax.ShapeDtypeStruct((B, T, D_MODEL), F32), state_shape, state_shape],
        scratch_shapes=[
            seq_buf, seq_buf, seq_buf,
            pltpu.VMEM((T // RET_CHUNK, RET_HEADS, RET_DK, RET_DV), BF16),
            pltpu.VMEM((RET_HEADS, RET_DK, RET_DV), F32),
            pltpu.VMEM((RET_HEADS, RET_DK, RET_DV), F32),
            pltpu.VMEM((tb, D_MODEL), BF16),
        ],
        compiler_params=_params("parallel", "arbitrary", "arbitrary"),
        name="even_layer",
    )(x, sh.table, sc.table, g.table, ng, w_in, conv_w, w_out, mask, dec, cdec, s0f, s0b)


def _retention_tables():
    C = RET_CHUNK
    hh = jnp.arange(RET_HEADS, dtype=F32)
    lg_f = jnp.log(1.0 - 2.0 ** (-5.0 - hh))
    lg_b = jnp.log(1.0 - 2.0 ** (-5.0 - hh - RET_BWD_OFFSET))
    idx = jnp.arange(C, dtype=F32)
    diff = idx[:, None] - idx[None, :]
    m_f = jnp.where(diff >= 0, jnp.exp(lg_f[:, None, None] * jnp.maximum(diff, 0.0)), 0.0)
    m_b = jnp.where(diff <= 0, jnp.exp(lg_b[:, None, None] * jnp.maximum(-diff, 0.0)), 0.0)
    mask = m_f + m_b
    q_f = jnp.exp(lg_f[:, None] * (idx[None, :] + 1.0))
    q_b = jnp.exp(lg_b[:, None] * (C - idx[None, :]))
    k_f = jnp.exp(lg_f[:, None] * (C - 1.0 - idx[None, :]))
    k_b = jnp.exp(lg_b[:, None] * idx[None, :])
    dec = jnp.stack([q_f, q_b, k_f, k_b])
    dec = jnp.broadcast_to(dec[..., None], dec.shape + (RET_DV,))
    cdec = jnp.stack([jnp.exp(lg_f * C), jnp.exp(lg_b * C)])
    return mask, dec, cdec


def _rope_mix(r, cs_ref, first):
    if cs_ref is not None:
        t = r * cs_ref[...]
        r = t + pltpu.roll(t, QK_ROPE, 1)
    return jnp.where(first, r, 0.0)


def _mla_in_kernel(*refs, rope, emit_cache, tq):
    (x_ref, sh_ref, sc_ref, ng_ref, wlat_ref, wg_ref, qng_ref, kvng_ref, qupt_ref,
     kup_ref, vupt_ref) = refs[:11]
    pos = 11
    cs_ref = cst_ref = None
    if rope:
        cs_ref, cst_ref = refs[pos:pos + 2]
        pos += 2
    q_ref, k_ref, v_ref, sg_ref = refs[pos:pos + 4]
    pos += 4
    if emit_cache:
        ckv_ref, kr_ref = refs[pos:pos + 2]

    h = _rms_rows(x_ref[0], ng_ref[...]) * (1.0 + sc_ref[0]) + sh_ref[0]
    hb = h.astype(BF16)
    lat = _dot(hb, wlat_ref[...])
    kr2 = lat[:, Q_LORA + KV_LORA:]
    sg_ref[0] = _silu(_dot(hb, wg_ref[...])).astype(BF16)
    qn = _rms_rows(lat[:, :Q_LORA], qng_ref[...]).astype(BF16)
    ckv = _rms_rows(lat[:, Q_LORA:Q_LORA + KV_LORA], kvng_ref[...])
    qt = _dot_nt(qupt_ref[...], qn)
    ckv_b = ckv.astype(BF16)
    kn = _dot(ckv_b, kup_ref[...])
    vt = _dot_nt(vupt_ref[...], ckv_b)

    scale = (QK_NOPE + QK_ROPE) ** -0.5 * LOG2_E
    tm = kr2.shape[0]
    first = lax.broadcasted_iota(jnp.int32, kr2.shape, 1) < QK_ROPE
    k_rot = _rope_mix(kr2, cs_ref, first).astype(BF16)
    q_pad = jnp.zeros((QK_PAD - QK_NOPE - QK_ROPE, tq), BF16)
    f = QK_ROPE // 4
    for hd in range(MLA_HEADS):
        base = hd * (QK_NOPE + QK_ROPE)
        q_nope = (qt[base:base + QK_NOPE, :] * scale).astype(BF16)
        q_rot = qt[base + QK_NOPE:base + QK_NOPE + QK_ROPE, :]
        if rope:
            partner = jnp.concatenate([q_rot[f:2 * f], q_rot[0:f], q_rot[3 * f:4 * f], q_rot[2 * f:3 * f]], axis=0)
            q_rot = q_rot * cst_ref[0:QK_ROPE, :] + partner * cst_ref[QK_ROPE:2 * QK_ROPE, :]
        q_rot = (q_rot * scale).astype(BF16)
        for j in range(tm // tq):
            cols = slice(j * tq, (j + 1) * tq)
            q_ref[0, hd, j, 0:QK_NOPE, :] = q_nope[:, cols]
            q_ref[0, hd, j, QK_NOPE:QK_NOPE + QK_ROPE, :] = q_rot[:, cols]
            q_ref[0, hd, j, QK_NOPE + QK_ROPE:QK_PAD, :] = q_pad
        k_ref[0, hd, :, 0:QK_NOPE] = kn[:, hd * QK_NOPE:(hd + 1) * QK_NOPE].astype(BF16)
        k_ref[0, hd, :, QK_NOPE:QK_PAD] = k_rot
        v_ref[0, hd] = vt[hd * V_HEAD:(hd + 1) * V_HEAD, :].astype(BF16)
    if emit_cache:
        ckv_ref[0] = ckv
        kr_ref[0] = kr2[:, :QK_ROPE]


def _mla_in_call(x, sh, sc, ng, wts, cs, tm, tq, emit_cache):
    B, T, _ = x.shape
    rope = cs is not None
    wlat, wg, qng, kvng, qup, kup, vupt = wts

    def full(a):
        return pl.BlockSpec(a.shape, lambda b, i: (0,) * a.ndim)

    in_specs = [
        pl.BlockSpec((1, tm, D_MODEL), lambda b, i: (b, i, 0)),
        _mod_spec(sh), _mod_spec(sc), full(ng),
        full(wlat), full(wg), full(qng), full(kvng), full(qup), full(kup), full(vupt),
    ]
    args = [x, sh.table, sc.table, ng, wlat, wg, qng, kvng, qup, kup, vupt]
    if rope:
        in_specs += [pl.BlockSpec((tm, 2 * QK_ROPE), lambda b, i: (i, 0)),
                     pl.BlockSpec((2 * QK_ROPE, tm), lambda b, i: (0, i))]
        f = QK_ROPE // 4
        sign = jnp.concatenate([jnp.ones((QK_ROPE,), F32)] + [-jnp.ones((f,), F32), jnp.ones((f,), F32)] * 2)
        args += [cs, (cs * sign).T]
    out_specs = [pl.BlockSpec((1, MLA_HEADS, tm // tq, QK_PAD, tq), lambda b, i: (b, 0, i, 0, 0)),
                 pl.BlockSpec((1, MLA_HEADS, tm, QK_PAD), lambda b, i: (b, 0, i, 0)),
                 pl.BlockSpec((1, MLA_HEADS, V_HEAD, tm), lambda b, i: (b, 0, 0, i)),
                 pl.BlockSpec((1, tm, D_MODEL), lambda b, i: (b, i, 0))]
    out_shape = [jax.ShapeDtypeStruct((B, MLA_HEADS, T // tq, QK_PAD, tq), BF16),
                 jax.ShapeDtypeStruct((B, MLA_HEADS, T, QK_PAD), BF16),
                 jax.ShapeDtypeStruct((B, MLA_HEADS, V_HEAD, T), BF16),
                 jax.ShapeDtypeStruct((B, T, D_MODEL), BF16)]
    if emit_cache:
        out_specs += [pl.BlockSpec((1, tm, KV_LORA), lambda b, i: (b, i, 0)),
                      pl.BlockSpec((1, tm, QK_ROPE), lambda b, i: (b, i, 0))]
        out_shape += [jax.ShapeDtypeStruct((B, T, KV_LORA), F32),
                      jax.ShapeDtypeStruct((B, T, QK_ROPE), F32)]
    return pl.pallas_call(
        functools.partial(_mla_in_kernel, rope=rope, emit_cache=emit_cache, tq=tq),
        grid=(B, T // tm),
        in_specs=in_specs, out_specs=out_specs, out_shape=out_shape,
        compiler_params=_params("parallel", "arbitrary"),
        name="mla_in",
    )(*args)


def _ctx_expand_kernel(ckv_ref, kr_ref, kup_ref, vupt_ref, k_ref, v_ref):
    ckv_b = ckv_ref[0].astype(BF16)
    kn = _dot(ckv_b, kup_ref[...])
    vt = _dot_nt(vupt_ref[...], ckv_b)
    kr = kr_ref[0].astype(BF16)
    zeros = jnp.zeros_like(kr)
    for hd in range(MLA_HEADS):
        k_ref[0, hd, :, 0:QK_NOPE] = kn[:, hd * QK_NOPE:(hd + 1) * QK_NOPE].astype(BF16)
        k_ref[0, hd, :, QK_NOPE:QK_NOPE + QK_ROPE] = kr
        k_ref[0, hd, :, QK_NOPE + QK_ROPE:QK_PAD] = zeros
        v_ref[0, hd] = vt[hd * V_HEAD:(hd + 1) * V_HEAD, :].astype(BF16)


def _ctx_expand_call(ckv, kr, kup, vupt):
    B, L, _ = ckv.shape
    return pl.pallas_call(
        _ctx_expand_kernel,
        grid=(B,),
        in_specs=[
            pl.BlockSpec((1, L, KV_LORA), lambda b: (b, 0, 0)),
            pl.BlockSpec((1, L, QK_ROPE), lambda b: (b, 0, 0)),
            pl.BlockSpec(kup.shape, lambda b: (0, 0)),
            pl.BlockSpec(vupt.shape, lambda b: (0, 0)),
        ],
        out_specs=[pl.BlockSpec((1, MLA_HEADS, L, QK_PAD), lambda b: (b, 0, 0, 0)),
                   pl.BlockSpec((1, MLA_HEADS, V_HEAD, L), lambda b: (b, 0, 0, 0))],
        out_shape=[jax.ShapeDtypeStruct((B, MLA_HEADS, L, QK_PAD), BF16),
                   jax.ShapeDtypeStruct((B, MLA_HEADS, V_HEAD, L), BF16)],
        compiler_params=_params("parallel"),
        name="ctx_expand",
    )(ckv, kr, kup, vupt)


ATTN_UNROLL = 4
ATTN_LEAD = 4


def _attn_ctx_kernel(q_ref, kc_ref, kl_ref, vc_ref, vl_ref, sg_ref, o_ref, s0_scr, s1_scr):
    nq, _, tq = q_ref.shape[2:]
    T = nq * tq
    Lc = kc_ref.shape[2]
    tiles = ([(kc_ref, vc_ref, j * MXU_DIM) for j in range(Lc // MXU_DIM)]
             + [(kl_ref, vl_ref, j * MXU_DIM) for j in range(T // MXU_DIM)])
    groups = MXU_DIM // F32_SUBLANES

    def score_tile(qb, t, s_dst):
        kref, _, off = tiles[t]
        s = _dot(kref[0, 0, off:off + MXU_DIM, :], qb)
        s_dst[t * MXU_DIM:(t + 1) * MXU_DIM, :] = s

    def tile_max(s_buf, t):
        s = s_buf[t * MXU_DIM:(t + 1) * MXU_DIM, :]
        return jnp.max(s.reshape(groups, F32_SUBLANES, tq), axis=0)

    def step(blk, blk_next, m_cur, s_cur, s_nxt):
        scoring = blk_next is not None
        row = pl.multiple_of(blk * tq, tq)
        if scoring:
            q_next = q_ref[0, 0, blk_next]
            for t in range(ATTN_LEAD):
                score_tile(q_next, t, s_nxt)
        acc = lacc = None
        parts = []
        for t, (_, vref, off) in enumerate(tiles):
            p = jnp.exp2(s_cur[t * MXU_DIM:(t + 1) * MXU_DIM, :] - m_cur)
            ps = jnp.sum(p.reshape(groups, F32_SUBLANES, tq), axis=0)
            lacc = ps if lacc is None else lacc + ps
            pv = _dot(vref[0, 0, :, off:off + MXU_DIM], p.astype(BF16))
            acc = pv if acc is None else acc + pv
            if scoring:
                if t + ATTN_LEAD < len(tiles):
                    score_tile(q_next, t + ATTN_LEAD, s_nxt)
                parts.append(tile_max(s_nxt, t))
        l = jnp.sum(lacc, axis=0, keepdims=True)
        gate = sg_ref[0, pl.ds(row, tq), :].astype(F32)
        o_ref[0, pl.ds(row, tq), :] = ((acc * (1.0 / l)).T * gate).astype(BF16)
        if not scoring:
            return None
        return jnp.max(functools.reduce(jnp.maximum, parts), axis=0, keepdims=True)

    q0 = q_ref[0, 0, 0]
    for t in range(len(tiles)):
        score_tile(q0, t, s0_scr)
    mrun = functools.reduce(jnp.maximum, [tile_max(s0_scr, t) for t in range(len(tiles))])
    m0 = jnp.max(mrun, axis=0, keepdims=True)

    def trip(i, m, last):
        bufs = (s0_scr, s1_scr)
        for u in range(ATTN_UNROLL):
            blk = i * ATTN_UNROLL + u
            blk_next = None if (last and u == ATTN_UNROLL - 1) else blk + 1
            m = step(blk, blk_next, m, bufs[u % 2], bufs[(u + 1) % 2])
        return m

    trips = nq // ATTN_UNROLL
    m = lax.fori_loop(0, trips - 1, lambda i, m: trip(i, m, False), m0)
    trip(trips - 1, m, True)


def _attn_ctx_call(q, kc, kl, vc, vl, sg):
    B, H, nq, _, tq = q.shape
    T = nq * tq
    Lc = kc.shape[2]
    assert nq % ATTN_UNROLL == 0 and ATTN_UNROLL % 2 == 0
    return pl.pallas_call(
        _attn_ctx_kernel,
        scratch_shapes=[pltpu.VMEM((Lc + T, tq), F32), pltpu.VMEM((Lc + T, tq), F32)],
        grid=(B, H),
        in_specs=[
            pl.BlockSpec((1, 1, nq, QK_PAD, tq), lambda b, h: (b, h, 0, 0, 0)),
            pl.BlockSpec((1, 1, Lc, QK_PAD), lambda b, h: (b, h, 0, 0)),
            pl.BlockSpec((1, 1, T, QK_PAD), lambda b, h: (b, h, 0, 0)),
            pl.BlockSpec((1, 1, V_HEAD, Lc), lambda b, h: (b, h, 0, 0)),
            pl.BlockSpec((1, 1, V_HEAD, T), lambda b, h: (b, h, 0, 0)),
            pl.BlockSpec((1, T, V_HEAD), lambda b, h: (b, 0, h)),
        ],
        out_specs=pl.BlockSpec((1, T, V_HEAD), lambda b, h: (b, 0, h)),
        out_shape=jax.ShapeDtypeStruct((B, T, H * V_HEAD), BF16),
        compiler_params=_params("parallel", "arbitrary"),
        name="attn_ctx",
    )(q, kc, kl, vc, vl, sg)


def _attn_self_kernel(q_ref, k_ref, v_ref, sg_ref, o_ref):
    heads = range(MLA_HEADS)
    s = [_dot(k_ref[0, hd], q_ref[0, hd, 0]) for hd in heads]
    p = [jnp.exp2(a - jnp.max(a, axis=0, keepdims=True)) for a in s]
    inv = [1.0 / jnp.sum(a, axis=0, keepdims=True) for a in p]
    o = [_dot(v_ref[0, hd], p[hd].astype(BF16)) * inv[hd] for hd in heads]
    for hd in heads:
        cols = slice(hd * V_HEAD, (hd + 1) * V_HEAD)
        o_ref[0, :, cols] = (o[hd].T * sg_ref[0, :, cols].astype(F32)).astype(BF16)


def _attn_self_call(q, k, v, sg):
    _, H, nseq, _, seq = q.shape
    N = nseq * seq
    return pl.pallas_call(
        _attn_self_kernel,
        grid=(nseq,),
        in_specs=[
            pl.BlockSpec((1, H, 1, QK_PAD, seq), lambda b: (0, 0, b, 0, 0)),
            pl.BlockSpec((1, H, seq, QK_PAD), lambda b: (0, 0, b, 0)),
            pl.BlockSpec((1, H, V_HEAD, seq), lambda b: (0, 0, 0, b)),
            pl.BlockSpec((1, seq, H * V_HEAD), lambda b: (0, b, 0)),
        ],
        out_specs=pl.BlockSpec((1, seq, H * V_HEAD), lambda b: (0, b, 0)),
        out_shape=jax.ShapeDtypeStruct((1, N, H * V_HEAD), BF16),
        compiler_params=_params("parallel"),
        name="attn_self",
    )(q, k, v, sg)


def _mla_out_kernel(o_ref, x_ref, g_ref, wo_ref, fg_ref, y_ref):
    x2 = x_ref[0] + g_ref[0] * _dot(o_ref[0], wo_ref[...])
    y_ref[0] = _rms_rows(x2, fg_ref[...])


def _mla_out_call(o, x, g, w_out, fg, tm):
    B, T, _ = x.shape
    tok = lambda: pl.BlockSpec((1, tm, D_MODEL), lambda b, i: (b, i, 0))
    return pl.pallas_call(
        _mla_out_kernel,
        grid=(B, T // tm),
        in_specs=[tok(), tok(), _mod_spec(g),
                  pl.BlockSpec(w_out.shape, lambda b, i: (0, 0)),
                  pl.BlockSpec((1, D_MODEL), lambda b, i: (0, 0))],
        out_specs=tok(),
        out_shape=jax.ShapeDtypeStruct((B, T, D_MODEL), F32),
        compiler_params=_params("parallel", "arbitrary"),
        name="mla_out",
    )(o, x, g.table, w_out, fg)


def _swap_halves(w):
    f = QK_ROPE // 4
    return jnp.concatenate([-w[..., f:2 * f], w[..., 0:f], -w[..., 3 * f:4 * f], w[..., 2 * f:3 * f]], axis=-1)


def _rope_table(T):
    f = QK_ROPE // 4
    rows = T // GRID_W
    inv = ROPE_BASE ** (-jnp.arange(f, dtype=F32) / f)
    ar = jnp.arange(rows, dtype=F32)[:, None] * inv
    ac = jnp.arange(GRID_W, dtype=F32)[:, None] * inv

    def by_row(a):
        return jnp.broadcast_to(a[:, None, :], (rows, GRID_W, f)).reshape(T, f)

    def by_col(a):
        return jnp.broadcast_to(a[None, :, :], (rows, GRID_W, f)).reshape(T, f)

    cr, sr, cc, sc = by_row(jnp.cos(ar)), by_row(jnp.sin(ar)), by_col(jnp.cos(ac)), by_col(jnp.sin(ac))
    return jnp.concatenate([cr, cr, cc, cc, sr, sr, sc, sc], axis=-1)


def kernel(x_prompt, x_sample, c, state_ret_fwd, state_ret_bwd, cache_mla_ckv, cache_mla_krope, c_ctx,
           ada_w, ada_b, norm_g, even_in_w, even_conv_w, even_out_w, odd_in_w, odd_q_norm_g,
           odd_kv_norm_g, odd_q_up_w, odd_kv_up_w, odd_out_w, final_norm_g):
    BP, SEQ, D = x_prompt.shape
    BS, TS, _ = x_sample.shape
    NP = BP * SEQ

    cvec = jnp.concatenate([c, c_ctx[None, :], jnp.zeros((ADA_ROWS - BS - 1, D), F32)], axis=0)
    mod = _ada_call(cvec, ada_w, ada_b).reshape(-1, ADA_ROWS, 3, 1, D)

    def mods(l):
        return ([Mod(mod, l, i, BS) for i in range(3)], [Mod(mod, l, i, None) for i in range(3)])

    (sh_p, sc_p, g_p), (sh_s, sc_s, g_s) = mods(0)
    ng = norm_g[0][None, :]
    w_in = even_in_w[0].astype(BF16)
    w_out = even_out_w[0].astype(BF16)
    conv_w = even_conv_w[0]
    mask, dec, cdec = _retention_tables()

    xp1, nsf, nsb = _even_layer_call(x_prompt, sh_p, sc_p, g_p, ng, w_in, conv_w, w_out,
                                     mask, dec, cdec, None, None, SEQ)
    xs1, _, _ = _even_layer_call(x_sample, sh_s, sc_s, g_s, ng, w_in, conv_w, w_out, mask, dec, cdec,
                                 state_ret_fwd[:, 0], state_ret_bwd[:, 0], TOKEN_BLOCK)

    (sh_p, sc_p, g_p), (sh_s, sc_s, g_s) = mods(1)
    ng = norm_g[1][None, :]
    w_in = odd_in_w[0]
    nq = Q_LORA + KV_LORA
    wkr = w_in[:, nq:nq + QK_ROPE]
    wlat = jnp.concatenate([w_in[:, :nq + QK_ROPE], _swap_halves(wkr)], axis=-1).astype(BF16)
    wg = w_in[:, nq + QK_ROPE:].astype(BF16)
    qup = odd_q_up_w[0].T.astype(BF16)
    kvup = odd_kv_up_w[0].reshape(KV_LORA, MLA_HEADS, QK_NOPE + V_HEAD)
    kup = kvup[..., :QK_NOPE].reshape(KV_LORA, MLA_HEADS * QK_NOPE).astype(BF16)
    vupt = kvup[..., QK_NOPE:].reshape(KV_LORA, MLA_HEADS * V_HEAD).T.astype(BF16)
    wts = (wlat, wg, odd_q_norm_g[0][None, :], odd_kv_norm_g[0][None, :], qup, kup, vupt)
    w_out = odd_out_w[0].astype(BF16)
    fg = final_norm_g[None, :]

    xp1f = xp1.reshape(1, NP, D)
    q_p, k_p, v_p, sg_p, ckv_p, kr_p = _mla_in_call(xp1f, sh_p, sc_p, ng, wts, None, TOKEN_BLOCK, SEQ, True)
    q_s, k_s, v_s, sg_s = _mla_in_call(xs1, sh_s, sc_s, ng, wts, _rope_table(TS), TOKEN_BLOCK, MXU_DIM, False)
    k_c, v_c = _ctx_expand_call(cache_mla_ckv[:, 0], cache_mla_krope[:, 0], kup, vupt)

    o_p = _attn_self_call(q_p, k_p, v_p, sg_p)
    o_s = _attn_ctx_call(q_s, k_c, k_s, v_c, v_s, sg_s)

    y_p = _mla_out_call(o_p, xp1f, g_p, w_out, fg, OUT_BLOCK)
    y_s = _mla_out_call(o_s, xs1, g_s, w_out, fg, OUT_BLOCK)

    return (y_p.reshape(BP, SEQ, D), y_s,
            nsf[:, None], nsb[:, None],
            ckv_p.reshape(BP, 1, SEQ, KV_LORA), kr_p.reshape(BP, 1, SEQ, QK_ROPE))
```

```python
import functools
from typing import NamedTuple, Optional

import jax
import jax.numpy as jnp
from jax import lax
from jax.experimental import pallas as pl
from jax.experimental.pallas import tpu as pltpu

F32 = jnp.float32
BF16 = jnp.bfloat16

D_MODEL = 1024
GRID_W = 64
EPS = 1e-6
RET_HEADS = 4
RET_DK = 128
RET_DV = 128
RET_WIDTH = RET_HEADS * RET_DV
RET_CHUNK = 128
RET_BWD_OFFSET = 0.5
CONV_WIDTH = D_MODEL - RET_WIDTH
MLA_HEADS = 8
QK_NOPE = 128
QK_ROPE = 64
V_HEAD = 128
Q_LORA = 384
KV_LORA = 256
ROPE_BASE = 10000.0
QK_PAD = 256
ADA_ROWS = 16
BF16_SUBLANES = 16
F32_SUBLANES = 8
MXU_DIM = 256
LOG2_E = 1.4426950408889634
VMEM_LIMIT = 56 * 1024 * 1024
TOKEN_BLOCK = 512
OUT_BLOCK = 1024


def _silu(x):
    return x * (1.0 / (1.0 + jnp.exp(-x)))


def _rms_rows(x, g):
    return x * lax.rsqrt(jnp.mean(x * x, axis=-1, keepdims=True) + EPS) * g


def _dot(a, b):
    return jnp.dot(a, b, preferred_element_type=F32)


def _dot_nt(a, b):
    return lax.dot_general(a, b, (((1,), (1,)), ((), ())), preferred_element_type=F32)


def _dot_tn(a, b):
    return lax.dot_general(a, b, (((0,), (0,)), ((), ())), preferred_element_type=F32)


def _params(*sem):
    return pltpu.CompilerParams(dimension_semantics=sem, vmem_limit_bytes=VMEM_LIMIT)


def _ada_kernel(c_ref, w_ref, b_ref, o_ref):
    s = _silu(c_ref[...]).astype(BF16)
    o_ref[0] = _dot(s, w_ref[0].astype(BF16)) + b_ref[0]


def _ada_call(cvec, ada_w, ada_b):
    depth = ada_w.shape[0]
    return pl.pallas_call(
        _ada_kernel,
        grid=(depth, 3),
        in_specs=[
            pl.BlockSpec((ADA_ROWS, D_MODEL), lambda l, j: (0, 0)),
            pl.BlockSpec((1, D_MODEL, D_MODEL), lambda l, j: (l, 0, j)),
            pl.BlockSpec((1, 1, D_MODEL), lambda l, j: (l, 0, j)),
        ],
        out_specs=pl.BlockSpec((1, ADA_ROWS, D_MODEL), lambda l, j: (l, 0, j)),
        out_shape=jax.ShapeDtypeStruct((depth, ADA_ROWS, 3 * D_MODEL), F32),
        compiler_params=_params("arbitrary", "arbitrary"),
        name="ada",
    )(cvec, ada_w, ada_b.reshape(depth, 1, 3 * D_MODEL))


class Mod(NamedTuple):
    table: jax.Array
    layer: int
    part: int
    row: Optional[int]


def _mod_spec(m):
    row = (lambda b: b) if m.row is None else (lambda b: m.row)
    return pl.BlockSpec((None, None, 1, 1, D_MODEL), lambda b, *_: (m.layer, row(b), m.part, 0, 0))


def _even_layer_kernel(x_ref, sh_ref, sc_ref, g_ref, ng_ref, w_ref, cw_ref, wo_ref, mask_ref, dec_ref,
                       cdec_ref, s0f_ref, s0b_ref,
                       xo_ref, nsf_ref, nsb_ref,
                       k_all, v_all, z_all, sb_all, sf_run, sb_run, ymix,
                       *, nc, nb, zero_init):
    p = pl.program_id(1)
    j = pl.program_id(2)
    C = RET_CHUNK
    H = RET_HEADS
    dv = RET_DV
    W = RET_WIDTH
    tb = nc * C
    T = nb * tb

    def tile(cc, h):
        return slice(cc * C, (cc + 1) * C), slice(h * dv, (h + 1) * dv)

    def project():
        h = _rms_rows(x_ref[0], ng_ref[...]) * (1.0 + sc_ref[0]) + sh_ref[0]
        hb = h.astype(BF16)
        return lambda g: _dot(hb, w_ref[:, g * W:(g + 1) * W])

    @pl.when((p == 0) & (j == 0))
    def _():
        if zero_init:
            sf_run[...] = jnp.zeros_like(sf_run)
            sb_run[...] = jnp.zeros_like(sb_run)
        else:
            sf_run[...] = s0f_ref[0]
            sb_run[...] = s0b_ref[0]

    @pl.when(p == 0)
    def _():
        blk = nb - 1 - j
        r0 = pl.multiple_of(blk * tb, tb)
        proj = project()
        k = (proj(1) * (RET_DK ** -0.5)).astype(BF16)
        v = proj(2).astype(BF16)
        k_all[pl.ds(r0, tb), :] = k
        v_all[pl.ds(r0, tb), :] = v
        z_all[pl.ds(r0, tb), :] = (proj(5) * proj(6)).astype(BF16)
        for cc in reversed(range(nc)):
            gc = blk * nc + cc
            for h in range(H):
                rows, cols = tile(cc, h)
                sb_all[gc, h] = sb_run[h].astype(BF16)
                kd = (k[rows, cols].astype(F32) * dec_ref[3, h]).astype(BF16)
                sb_run[h] = sb_run[h] * cdec_ref[1, h] + _dot_tn(kd, v[rows, cols])

        @pl.when(j == nb - 1)
        def _():
            nsb_ref[0] = sb_run[...]

    @pl.when(p == 1)
    def _():
        r0 = pl.multiple_of(j * tb, tb)
        proj = project()
        q = proj(0).astype(BF16)
        sga = _silu(proj(3))
        k = k_all[pl.ds(r0, tb), :]
        v = v_all[pl.ds(r0, tb), :]
        items = [(cc, h) for cc in range(nc) for h in range(H)]

        def scores(cc, h):
            rows, cols = tile(cc, h)
            return _dot_nt(q[rows, cols], k[rows, cols])

        s_next = scores(*items[0])
        for i, (cc, h) in enumerate(items):
            s = s_next
            if i + 1 < len(items):
                s_next = scores(*items[i + 1])
            gc = j * nc + cc
            rows, cols = tile(cc, h)
            qh, kh, vh = q[rows, cols], k[rows, cols], v[rows, cols]
            inter = (_dot(qh, sf_run[h].astype(BF16)) * dec_ref[0, h]
                     + _dot(qh, sb_all[gc, h]) * dec_ref[1, h])
            kd = (kh.astype(F32) * dec_ref[2, h]).astype(BF16)
            sf_run[h] = sf_run[h] * cdec_ref[0, h] + _dot_tn(kd, vh)
            att = (s * mask_ref[h]).astype(BF16)
            o = _dot(att, vh) + inter
            on = o * lax.rsqrt(jnp.mean(o * o, axis=-1, keepdims=True) + EPS)
            ymix[rows, cols] = (on * sga[rows, cols]).astype(BF16)

        u = _silu(proj(7)) * proj(4)
        z = z_all[pl.ds(r0, tb), :].astype(F32)
        row = lax.broadcasted_iota(jnp.int32, z.shape, 0)
        n = BF16_SUBLANES
        r_prev = pl.multiple_of(jnp.maximum(r0 - n, 0), n)
        r_next = pl.multiple_of(jnp.minimum(r0 + tb, T - n), n)
        prev_row = z_all[pl.ds(r_prev, n), :][n - 1:n, :].astype(F32)
        next_row = z_all[pl.ds(r_next, n), :][0:1, :].astype(F32)
        prev_row = jnp.where(j > 0, prev_row, 0.0)
        next_row = jnp.where(j < nb - 1, next_row, 0.0)
        z_prev = jnp.where(row == 0, prev_row, pltpu.roll(z, 1, 0))
        z_next = jnp.where(row == tb - 1, next_row, pltpu.roll(z, tb - 1, 0))
        zc = z_prev * cw_ref[0:1, :] + z * cw_ref[1:2, :] + z_next * cw_ref[2:3, :]
        ymix[:, W:] = (u * zc).astype(BF16)

        y = _dot(ymix[...], wo_ref[...])
        xo_ref[0] = x_ref[0] + g_ref[0] * y

        @pl.when(j == nb - 1)
        def _():
            nsf_ref[0] = sf_run[...]


def _even_layer_call(x, sh, sc, g, ng, w_in, conv_w, w_out, mask, dec, cdec, s0f, s0b, tb):
    B, T, _ = x.shape
    nb = T // tb
    nc = tb // RET_CHUNK
    zero_init = s0f is None
    state_block = (1, RET_HEADS, RET_DK, RET_DV)
    if zero_init:
        s0f = s0b = jnp.zeros(state_block, F32)
        s0_spec = pl.BlockSpec(state_block, lambda b, p, j: (0, 0, 0, 0))
    else:
        s0_spec = pl.BlockSpec(state_block, lambda b, p, j: (b, 0, 0, 0))

    def const(a):
        return pl.BlockSpec(a.shape, lambda b, p, j: (0,) * a.ndim, pipeline_mode=pl.Buffered(1))

    kernel = functools.partial(_even_layer_kernel, nc=nc, nb=nb, zero_init=zero_init)
    state_shape = jax.ShapeDtypeStruct((B,) + state_block[1:], F32)
    state_spec = pl.BlockSpec(state_block, lambda b, p, j: (b, 0, 0, 0))
    seq_buf = pltpu.VMEM((T, RET_WIDTH), BF16)
    return pl.pallas_call(
        kernel,
        grid=(B, 2, nb),
        in_specs=[
            pl.BlockSpec((1, tb, D_MODEL), lambda b, p, j: (b, p * j + (1 - p) * (nb - 1 - j), 0)),
            _mod_spec(sh), _mod_spec(sc), _mod_spec(g),
            const(ng), const(w_in), const(conv_w), const(w_out), const(mask), const(dec),
            pl.BlockSpec(memory_space=pltpu.SMEM),
            s0_spec, s0_spec,
        ],
        out_specs=[
            pl.BlockSpec((1, tb, D_MODEL), lambda b, p, j: (b, p * j, 0)),
            state_spec, state_spec,
        ],
        out_shape=[jax.ShapeDtypeStruct((B, T, D_MODEL), F32), state_shape, state_shape],
        scratch_shapes=[
            seq_buf, seq_buf, seq_buf,
            pltpu.VMEM((T // RET_CHUNK, RET_HEADS, RET_DK, RET_DV), BF16),
            pltpu.VMEM((RET_HEADS, RET_DK, RET_DV), F32),
            pltpu.VMEM((RET_HEADS, RET_DK, RET_DV), F32),
            pltpu.VMEM((tb, D_MODEL), BF16),
        ],
        compiler_params=_params("parallel", "arbitrary", "arbitrary"),
        name="even_layer",
    )(x, sh.table, sc.table, g.table, ng, w_in, conv_w, w_out, mask, dec, cdec, s0f, s0b)


def _retention_tables():
    C = RET_CHUNK
    hh = jnp.arange(RET_HEADS, dtype=F32)
    lg_f = jnp.log(1.0 - 2.0 ** (-5.0 - hh))
    lg_b = jnp.log(1.0 - 2.0 ** (-5.0 - hh - RET_BWD_OFFSET))
    idx = jnp.arange(C, dtype=F32)
    diff = idx[:, None] - idx[None, :]
    m_f = jnp.where(diff >= 0, jnp.exp(lg_f[:, None, None] * jnp.maximum(diff, 0.0)), 0.0)
    m_b = jnp.where(diff <= 0, jnp.exp(lg_b[:, None, None] * jnp.maximum(-diff, 0.0)), 0.0)
    mask = m_f + m_b
    q_f = jnp.exp(lg_f[:, None] * (idx[None, :] + 1.0))
    q_b = jnp.exp(lg_b[:, None] * (C - idx[None, :]))
    k_f = jnp.exp(lg_f[:, None] * (C - 1.0 - idx[None, :]))
    k_b = jnp.exp(lg_b[:, None] * idx[None, :])
    dec = jnp.stack([q_f, q_b, k_f, k_b])
    dec = jnp.broadcast_to(dec[..., None], dec.shape + (RET_DV,))
    cdec = jnp.stack([jnp.exp(lg_f * C), jnp.exp(lg_b * C)])
    return mask, dec, cdec


def _rope_mix(r, cs_ref, first):
    if cs_ref is not None:
        t = r * cs_ref[...]
        r = t + pltpu.roll(t, QK_ROPE, 1)
    return jnp.where(first, r, 0.0)


def _mla_in_kernel(*refs, rope, emit_cache, tq):
    (x_ref, sh_ref, sc_ref, ng_ref, wlat_ref, wg_ref, qng_ref, kvng_ref, qupt_ref,
     kup_ref, vupt_ref) = refs[:11]
    pos = 11
    cs_ref = cst_ref = None
    if rope:
        cs_ref, cst_ref = refs[pos:pos + 2]
        pos += 2
    q_ref, k_ref, v_ref, sg_ref = refs[pos:pos + 4]
    pos += 4
    if emit_cache:
        ckv_ref, kr_ref = refs[pos:pos + 2]

    h = _rms_rows(x_ref[0], ng_ref[...]) * (1.0 + sc_ref[0]) + sh_ref[0]
    hb = h.astype(BF16)
    lat = _dot(hb, wlat_ref[...])
    kr2 = lat[:, Q_LORA + KV_LORA:]
    sg = _silu(_dot(hb, wg_ref[...])).astype(BF16)
    for hd in range(MLA_HEADS):
        sg_ref[0, hd] = sg[:, hd * V_HEAD:(hd + 1) * V_HEAD]
    qn = _rms_rows(lat[:, :Q_LORA], qng_ref[...]).astype(BF16)
    ckv = _rms_rows(lat[:, Q_LORA:Q_LORA + KV_LORA], kvng_ref[...])
    qt = _dot_nt(qupt_ref[...], qn)
    ckv_b = ckv.astype(BF16)
    kn = _dot(ckv_b, kup_ref[...])
    vt = _dot_nt(vupt_ref[...], ckv_b)

    scale = (QK_NOPE + QK_ROPE) ** -0.5 * LOG2_E
    tm = kr2.shape[0]
    first = lax.broadcasted_iota(jnp.int32, kr2.shape, 1) < QK_ROPE
    k_rot = _rope_mix(kr2, cs_ref, first).astype(BF16)
    q_pad = jnp.zeros((QK_PAD - QK_NOPE - QK_ROPE, tq), BF16)
    f = QK_ROPE // 4
    for hd in range(MLA_HEADS):
        base = hd * (QK_NOPE + QK_ROPE)
        q_nope = (qt[base:base + QK_NOPE, :] * scale).astype(BF16)
        q_rot = qt[base + QK_NOPE:base + QK_NOPE + QK_ROPE, :]
        if rope:
            partner = jnp.concatenate([q_rot[f:2 * f], q_rot[0:f], q_rot[3 * f:4 * f], q_rot[2 * f:3 * f]], axis=0)
            q_rot = q_rot * cst_ref[0:QK_ROPE, :] + partner * cst_ref[QK_ROPE:2 * QK_ROPE, :]
        q_rot = (q_rot * scale).astype(BF16)
        for j in range(tm // tq):
            cols = slice(j * tq, (j + 1) * tq)
            q_ref[0, hd, j, 0:QK_NOPE, :] = q_nope[:, cols]
            q_ref[0, hd, j, QK_NOPE:QK_NOPE + QK_ROPE, :] = q_rot[:, cols]
            q_ref[0, hd, j, QK_NOPE + QK_ROPE:QK_PAD, :] = q_pad
        k_ref[0, hd, :, 0:QK_NOPE] = kn[:, hd * QK_NOPE:(hd + 1) * QK_NOPE].astype(BF16)
        k_ref[0, hd, :, QK_NOPE:QK_PAD] = k_rot
        v_ref[0, hd] = vt[hd * V_HEAD:(hd + 1) * V_HEAD, :].astype(BF16)
    if emit_cache:
        ckv_ref[0] = ckv
        kr_ref[0] = kr2[:, :QK_ROPE]


def _mla_in_call(x, sh, sc, ng, wts, cs, tm, tq, emit_cache):
    B, T, _ = x.shape
    rope = cs is not None
    wlat, wg, qng, kvng, qup, kup, vupt = wts

    def full(a):
        return pl.BlockSpec(a.shape, lambda b, i: (0,) * a.ndim)

    in_specs = [
        pl.BlockSpec((1, tm, D_MODEL), lambda b, i: (b, i, 0)),
        _mod_spec(sh), _mod_spec(sc), full(ng),
        full(wlat), full(wg), full(qng), full(kvng), full(qup), full(kup), full(vupt),
    ]
    args = [x, sh.table, sc.table, ng, wlat, wg, qng, kvng, qup, kup, vupt]
    if rope:
        in_specs += [pl.BlockSpec((tm, 2 * QK_ROPE), lambda b, i: (i, 0)),
                     pl.BlockSpec((2 * QK_ROPE, tm), lambda b, i: (0, i))]
        f = QK_ROPE // 4
        sign = jnp.concatenate([jnp.ones((QK_ROPE,), F32)] + [-jnp.ones((f,), F32), jnp.ones((f,), F32)] * 2)
        args += [cs, (cs * sign).T]
    out_specs = [pl.BlockSpec((1, MLA_HEADS, tm // tq, QK_PAD, tq), lambda b, i: (b, 0, i, 0, 0)),
                 pl.BlockSpec((1, MLA_HEADS, tm, QK_PAD), lambda b, i: (b, 0, i, 0)),
                 pl.BlockSpec((1, MLA_HEADS, V_HEAD, tm), lambda b, i: (b, 0, 0, i)),
                 pl.BlockSpec((1, MLA_HEADS, tm, V_HEAD), lambda b, i: (b, 0, i, 0))]
    out_shape = [jax.ShapeDtypeStruct((B, MLA_HEADS, T // tq, QK_PAD, tq), BF16),
                 jax.ShapeDtypeStruct((B, MLA_HEADS, T, QK_PAD), BF16),
                 jax.ShapeDtypeStruct((B, MLA_HEADS, V_HEAD, T), BF16),
                 jax.ShapeDtypeStruct((B, MLA_HEADS, T, V_HEAD), BF16)]
    if emit_cache:
        out_specs += [pl.BlockSpec((1, tm, KV_LORA), lambda b, i: (b, i, 0)),
                      pl.BlockSpec((1, tm, QK_ROPE), lambda b, i: (b, i, 0))]
        out_shape += [jax.ShapeDtypeStruct((B, T, KV_LORA), F32),
                      jax.ShapeDtypeStruct((B, T, QK_ROPE), F32)]
    return pl.pallas_call(
        functools.partial(_mla_in_kernel, rope=rope, emit_cache=emit_cache, tq=tq),
        grid=(B, T // tm),
        in_specs=in_specs, out_specs=out_specs, out_shape=out_shape,
        compiler_params=_params("parallel", "arbitrary"),
        name="mla_in",
    )(*args)


def _ctx_expand_kernel(ckv_ref, kr_ref, kup_ref, vupt_ref, k_ref, v_ref):
    ckv_b = ckv_ref[0].astype(BF16)
    kn = _dot(ckv_b, kup_ref[...])
    vt = _dot_nt(vupt_ref[...], ckv_b)
    kr = kr_ref[0].astype(BF16)
    zeros = jnp.zeros_like(kr)
    for hd in range(MLA_HEADS):
        k_ref[0, hd, :, 0:QK_NOPE] = kn[:, hd * QK_NOPE:(hd + 1) * QK_NOPE].astype(BF16)
        k_ref[0, hd, :, QK_NOPE:QK_NOPE + QK_ROPE] = kr
        k_ref[0, hd, :, QK_NOPE + QK_ROPE:QK_PAD] = zeros
        v_ref[0, hd] = vt[hd * V_HEAD:(hd + 1) * V_HEAD, :].astype(BF16)


def _ctx_expand_call(ckv, kr, kup, vupt):
    B, L, _ = ckv.shape
    return pl.pallas_call(
        _ctx_expand_kernel,
        grid=(B,),
        in_specs=[
            pl.BlockSpec((1, L, KV_LORA), lambda b: (b, 0, 0)),
            pl.BlockSpec((1, L, QK_ROPE), lambda b: (b, 0, 0)),
            pl.BlockSpec(kup.shape, lambda b: (0, 0)),
            pl.BlockSpec(vupt.shape, lambda b: (0, 0)),
        ],
        out_specs=[pl.BlockSpec((1, MLA_HEADS, L, QK_PAD), lambda b: (b, 0, 0, 0)),
                   pl.BlockSpec((1, MLA_HEADS, V_HEAD, L), lambda b: (b, 0, 0, 0))],
        out_shape=[jax.ShapeDtypeStruct((B, MLA_HEADS, L, QK_PAD), BF16),
                   jax.ShapeDtypeStruct((B, MLA_HEADS, V_HEAD, L), BF16)],
        compiler_params=_params("parallel"),
        name="ctx_expand",
    )(ckv, kr, kup, vupt)


ATTN_UNROLL = 4
ATTN_LEAD = 4


def _attn_ctx_kernel(q_ref, kc_ref, kl_ref, vc_ref, vl_ref, sg_ref, o_ref, s0_scr, s1_scr):
    hp, nq, _, tq = q_ref.shape[1:]
    T = nq * tq
    Lc = kc_ref.shape[2]
    tiles = ([(kc_ref, vc_ref, j * MXU_DIM) for j in range(Lc // MXU_DIM)]
             + [(kl_ref, vl_ref, j * MXU_DIM) for j in range(T // MXU_DIM)])
    groups = MXU_DIM // F32_SUBLANES

    def score_tile(hd, qb, t, s_dst):
        kref, _, off = tiles[t]
        s = _dot(kref[0, hd, off:off + MXU_DIM, :], qb)
        s_dst[t * MXU_DIM:(t + 1) * MXU_DIM, :] = s

    def tile_max(s_buf, t):
        s = s_buf[t * MXU_DIM:(t + 1) * MXU_DIM, :]
        return jnp.max(s.reshape(groups, F32_SUBLANES, tq), axis=0)

    def step(blk, blk_next, m_cur, s_cur, s_nxt):
        scoring = blk_next is not None
        hd, qi = blk // nq, blk % nq
        row = pl.multiple_of(qi * tq, tq)
        if scoring:
            hd_next = blk_next // nq
            q_next = q_ref[0, hd_next, blk_next % nq]
            for t in range(ATTN_LEAD):
                score_tile(hd_next, q_next, t, s_nxt)
        acc = lacc = None
        parts = []
        for t, (_, vref, off) in enumerate(tiles):
            p = jnp.exp2(s_cur[t * MXU_DIM:(t + 1) * MXU_DIM, :] - m_cur)
            ps = jnp.sum(p.reshape(groups, F32_SUBLANES, tq), axis=0)
            lacc = ps if lacc is None else lacc + ps
            pv = _dot(vref[0, hd, :, off:off + MXU_DIM], p.astype(BF16))
            acc = pv if acc is None else acc + pv
            if scoring:
                if t + ATTN_LEAD < len(tiles):
                    score_tile(hd_next, q_next, t + ATTN_LEAD, s_nxt)
                parts.append(tile_max(s_nxt, t))
        l = jnp.sum(lacc, axis=0, keepdims=True)
        gate = sg_ref[0, hd, pl.ds(row, tq), :].astype(F32)
        o_ref[0, hd, pl.ds(row, tq), :] = ((acc * (1.0 / l)).T * gate).astype(BF16)
        if not scoring:
            return None
        return jnp.max(functools.reduce(jnp.maximum, parts), axis=0, keepdims=True)

    q0 = q_ref[0, 0, 0]
    for t in range(len(tiles)):
        score_tile(0, q0, t, s0_scr)
    mrun = functools.reduce(jnp.maximum, [tile_max(s0_scr, t) for t in range(len(tiles))])
    m0 = jnp.max(mrun, axis=0, keepdims=True)

    def trip(i, m, last):
        bufs = (s0_scr, s1_scr)
        for u in range(ATTN_UNROLL):
            blk = i * ATTN_UNROLL + u
            blk_next = None if (last and u == ATTN_UNROLL - 1) else blk + 1
            m = step(blk, blk_next, m, bufs[u % 2], bufs[(u + 1) % 2])
        return m

    trips = hp * nq // ATTN_UNROLL
    m = lax.fori_loop(0, trips - 1, lambda i, m: trip(i, m, False), m0)
    trip(trips - 1, m, True)


ATTN_HEADS_PER_STEP = 2


def _attn_ctx_call(q, kc, kl, vc, vl, sg):
    B, H, nq, _, tq = q.shape
    T = nq * tq
    Lc = kc.shape[2]
    hp = ATTN_HEADS_PER_STEP
    assert nq % ATTN_UNROLL == 0 and ATTN_UNROLL % 2 == 0 and H % hp == 0

    def heads(*tail):
        return pl.BlockSpec((1, hp) + tail, lambda b, h: (b, h) + (0,) * len(tail))

    return pl.pallas_call(
        _attn_ctx_kernel,
        scratch_shapes=[pltpu.VMEM((Lc + T, tq), F32), pltpu.VMEM((Lc + T, tq), F32)],
        grid=(B, H // hp),
        in_specs=[heads(nq, QK_PAD, tq), heads(Lc, QK_PAD), heads(T, QK_PAD),
                  heads(V_HEAD, Lc), heads(V_HEAD, T), heads(T, V_HEAD)],
        out_specs=heads(T, V_HEAD),
        out_shape=jax.ShapeDtypeStruct((B, H, T, V_HEAD), BF16),
        compiler_params=_params("parallel", "arbitrary"),
        name="attn_ctx",
    )(q, kc, kl, vc, vl, sg)


def _attn_self_kernel(q_ref, k_ref, v_ref, sg_ref, o_ref):
    heads = range(MLA_HEADS)
    s = [_dot(k_ref[0, hd], q_ref[0, hd, 0]) for hd in heads]
    p = [jnp.exp2(a - jnp.max(a, axis=0, keepdims=True)) for a in s]
    inv = [1.0 / jnp.sum(a, axis=0, keepdims=True) for a in p]
    o = [_dot(v_ref[0, hd], p[hd].astype(BF16)) * inv[hd] for hd in heads]
    for hd in heads:
        o_ref[0, hd] = (o[hd].T * sg_ref[0, hd].astype(F32)).astype(BF16)


def _attn_self_call(q, k, v, sg):
    _, H, nseq, _, seq = q.shape
    N = nseq * seq
    return pl.pallas_call(
        _attn_self_kernel,
        grid=(nseq,),
        in_specs=[
            pl.BlockSpec((1, H, 1, QK_PAD, seq), lambda b: (0, 0, b, 0, 0)),
            pl.BlockSpec((1, H, seq, QK_PAD), lambda b: (0, 0, b, 0)),
            pl.BlockSpec((1, H, V_HEAD, seq), lambda b: (0, 0, 0, b)),
            pl.BlockSpec((1, H, seq, V_HEAD), lambda b: (0, 0, b, 0)),
        ],
        out_specs=pl.BlockSpec((1, H, seq, V_HEAD), lambda b: (0, 0, b, 0)),
        out_shape=jax.ShapeDtypeStruct((1, H, N, V_HEAD), BF16),
        compiler_params=_params("parallel"),
        name="attn_self",
    )(q, k, v, sg)


def _mla_out_kernel(o_ref, x_ref, g_ref, wo_ref, fg_ref, y_ref):
    o = jnp.concatenate([o_ref[0, hd] for hd in range(MLA_HEADS)], axis=1)
    x2 = x_ref[0] + g_ref[0] * _dot(o, wo_ref[...])
    y_ref[0] = _rms_rows(x2, fg_ref[...])


def _mla_out_call(o, x, g, w_out, fg, tm):
    B, T, _ = x.shape
    tok = lambda: pl.BlockSpec((1, tm, D_MODEL), lambda b, i: (b, i, 0))
    return pl.pallas_call(
        _mla_out_kernel,
        grid=(B, T // tm),
        in_specs=[pl.BlockSpec((1, MLA_HEADS, tm, V_HEAD), lambda b, i: (b, 0, i, 0)), tok(), _mod_spec(g),
                  pl.BlockSpec(w_out.shape, lambda b, i: (0, 0)),
                  pl.BlockSpec((1, D_MODEL), lambda b, i: (0, 0))],
        out_specs=tok(),
        out_shape=jax.ShapeDtypeStruct((B, T, D_MODEL), F32),
        compiler_params=_params("parallel", "arbitrary"),
        name="mla_out",
    )(o, x, g.table, w_out, fg)


def _swap_halves(w):
    f = QK_ROPE // 4
    return jnp.concatenate([-w[..., f:2 * f], w[..., 0:f], -w[..., 3 * f:4 * f], w[..., 2 * f:3 * f]], axis=-1)


def _rope_table(T):
    f = QK_ROPE // 4
    rows = T // GRID_W
    inv = ROPE_BASE ** (-jnp.arange(f, dtype=F32) / f)
    ar = jnp.arange(rows, dtype=F32)[:, None] * inv
    ac = jnp.arange(GRID_W, dtype=F32)[:, None] * inv

    def by_row(a):
        return jnp.broadcast_to(a[:, None, :], (rows, GRID_W, f)).reshape(T, f)

    def by_col(a):
        return jnp.broadcast_to(a[None, :, :], (rows, GRID_W, f)).reshape(T, f)

    cr, sr, cc, sc = by_row(jnp.cos(ar)), by_row(jnp.sin(ar)), by_col(jnp.cos(ac)), by_col(jnp.sin(ac))
    return jnp.concatenate([cr, cr, cc, cc, sr, sr, sc, sc], axis=-1)


def kernel(x_prompt, x_sample, c, state_ret_fwd, state_ret_bwd, cache_mla_ckv, cache_mla_krope, c_ctx,
           ada_w, ada_b, norm_g, even_in_w, even_conv_w, even_out_w, odd_in_w, odd_q_norm_g,
           odd_kv_norm_g, odd_q_up_w, odd_kv_up_w, odd_out_w, final_norm_g):
    BP, SEQ, D = x_prompt.shape
    BS, TS, _ = x_sample.shape
    NP = BP * SEQ

    cvec = jnp.concatenate([c, c_ctx[None, :], jnp.zeros((ADA_ROWS - BS - 1, D), F32)], axis=0)
    mod = _ada_call(cvec, ada_w, ada_b).reshape(-1, ADA_ROWS, 3, 1, D)

    def mods(l):
        return ([Mod(mod, l, i, BS) for i in range(3)], [Mod(mod, l, i, None) for i in range(3)])

    (sh_p, sc_p, g_p), (sh_s, sc_s, g_s) = mods(0)
    ng = norm_g[0][None, :]
    w_in = even_in_w[0].astype(BF16)
    w_out = even_out_w[0].astype(BF16)
    conv_w = even_conv_w[0]
    mask, dec, cdec = _retention_tables()

    xp1, nsf, nsb = _even_layer_call(x_prompt, sh_p, sc_p, g_p, ng, w_in, conv_w, w_out,
                                     mask, dec, cdec, None, None, SEQ)
    xs1, _, _ = _even_layer_call(x_sample, sh_s, sc_s, g_s, ng, w_in, conv_w, w_out, mask, dec, cdec,
                                 state_ret_fwd[:, 0], state_ret_bwd[:, 0], TOKEN_BLOCK)

    (sh_p, sc_p, g_p), (sh_s, sc_s, g_s) = mods(1)
    ng = norm_g[1][None, :]
    w_in = odd_in_w[0]
    nq = Q_LORA + KV_LORA
    wkr = w_in[:, nq:nq + QK_ROPE]
    wlat = jnp.concatenate([w_in[:, :nq + QK_ROPE], _swap_halves(wkr)], axis=-1).astype(BF16)
    wg = w_in[:, nq + QK_ROPE:].astype(BF16)
    qup = odd_q_up_w[0].T.astype(BF16)
    kvup = odd_kv_up_w[0].reshape(KV_LORA, MLA_HEADS, QK_NOPE + V_HEAD)
    kup = kvup[..., :QK_NOPE].reshape(KV_LORA, MLA_HEADS * QK_NOPE).astype(BF16)
    vupt = kvup[..., QK_NOPE:].reshape(KV_LORA, MLA_HEADS * V_HEAD).T.astype(BF16)
    wts = (wlat, wg, odd_q_norm_g[0][None, :], odd_kv_norm_g[0][None, :], qup, kup, vupt)
    w_out = odd_out_w[0].astype(BF16)
    fg = final_norm_g[None, :]

    xp1f = xp1.reshape(1, NP, D)
    q_p, k_p, v_p, sg_p, ckv_p, kr_p = _mla_in_call(xp1f, sh_p, sc_p, ng, wts, None, TOKEN_BLOCK, SEQ, True)
    q_s, k_s, v_s, sg_s = _mla_in_call(xs1, sh_s, sc_s, ng, wts, _rope_table(TS), TOKEN_BLOCK, MXU_DIM, False)
    k_c, v_c = _ctx_expand_call(cache_mla_ckv[:, 0], cache_mla_krope[:, 0], kup, vupt)

    o_p = _attn_self_call(q_p, k_p, v_p, sg_p)
    o_s = _attn_ctx_call(q_s, k_c, k_s, v_c, v_s, sg_s)

    y_p = _mla_out_call(o_p, xp1f, g_p, w_out, fg, OUT_BLOCK)
    y_s = _mla_out_call(o_s, xs1, g_s, w_out, fg, OUT_BLOCK)

    return (y_p.reshape(BP, SEQ, D), y_s,
            nsf[:, None], nsb[:, None],
            ckv_p.reshape(BP, 1, SEQ, KV_LORA), kr_p.reshape(BP, 1, SEQ, QK_ROPE))
```

```python
import functools
from typing import NamedTuple, Optional

import jax
import jax.numpy as jnp
from jax import lax
from jax.experimental import pallas as pl
from jax.experimental.pallas import tpu as pltpu

F32 = jnp.float32
BF16 = jnp.bfloat16

D_MODEL = 1024
GRID_W = 64
EPS = 1e-6
RET_HEADS = 4
RET_DK = 128
RET_DV = 128
RET_WIDTH = RET_HEADS * RET_DV
RET_CHUNK = 128
RET_BWD_OFFSET = 0.5
CONV_WIDTH = D_MODEL - RET_WIDTH
MLA_HEADS = 8
QK_NOPE = 128
QK_ROPE = 64
V_HEAD = 128
Q_LORA = 384
KV_LORA = 256
ROPE_BASE = 10000.0
QK_PAD = 256
ADA_ROWS = 16
BF16_SUBLANES = 16
F32_SUBLANES = 8
MXU_DIM = 256
LOG2_E = 1.4426950408889634
VMEM_LIMIT = 56 * 1024 * 1024
TOKEN_BLOCK = 512
OUT_BLOCK = 1024


def _silu(x):
    return x * (1.0 / (1.0 + jnp.exp(-x)))


def _rms_rows(x, g):
    return x * lax.rsqrt(jnp.mean(x * x, axis=-1, keepdims=True) + EPS) * g


def _dot(a, b):
    return jnp.dot(a, b, preferred_element_type=F32)


def _dot_nt(a, b):
    return lax.dot_general(a, b, (((1,), (1,)), ((), ())), preferred_element_type=F32)


def _dot_tn(a, b):
    return lax.dot_general(a, b, (((0,), (0,)), ((), ())), preferred_element_type=F32)


def _params(*sem):
    return pltpu.CompilerParams(dimension_semantics=sem, vmem_limit_bytes=VMEM_LIMIT)


def _ada_kernel(c_ref, w_ref, b_ref, o_ref):
    s = _silu(c_ref[...]).astype(BF16)
    o_ref[0] = _dot(s, w_ref[0].astype(BF16)) + b_ref[0]


def _ada_call(cvec, ada_w, ada_b):
    depth = ada_w.shape[0]
    return pl.pallas_call(
        _ada_kernel,
        grid=(depth, 3),
        in_specs=[
            pl.BlockSpec((ADA_ROWS, D_MODEL), lambda l, j: (0, 0)),
            pl.BlockSpec((1, D_MODEL, D_MODEL), lambda l, j: (l, 0, j)),
            pl.BlockSpec((1, 1, D_MODEL), lambda l, j: (l, 0, j)),
        ],
        out_specs=pl.BlockSpec((1, ADA_ROWS, D_MODEL), lambda l, j: (l, 0, j)),
        out_shape=jax.ShapeDtypeStruct((depth, ADA_ROWS, 3 * D_MODEL), F32),
        compiler_params=_params("arbitrary", "arbitrary"),
        name="ada",
    )(cvec, ada_w, ada_b.reshape(depth, 1, 3 * D_MODEL))


class Mod(NamedTuple):
    table: jax.Array
    layer: int
    part: int
    row: Optional[int]


def _mod_spec(m):
    row = (lambda b: b) if m.row is None else (lambda b: m.row)
    return pl.BlockSpec((None, None, 1, 1, D_MODEL), lambda b, *_: (m.layer, row(b), m.part, 0, 0))


def _even_layer_kernel(x_ref, sh_ref, sc_ref, g_ref, ng_ref, w_ref, cw_ref, wo_ref, mask_ref, dec_ref,
                       cdec_ref, s0f_ref, s0b_ref,
                       xo_ref, nsf_ref, nsb_ref,
                       k_all, v_all, z_all, sb_all, sf_run, sb_run, ymix,
                       *, nc, nb, zero_init):
    p = pl.program_id(1)
    j = pl.program_id(2)
    C = RET_CHUNK
    H = RET_HEADS
    dv = RET_DV
    W = RET_WIDTH
    tb = nc * C
    T = nb * tb

    def tile(cc, h):
        return slice(cc * C, (cc + 1) * C), slice(h * dv, (h + 1) * dv)

    def project():
        h = _rms_rows(x_ref[0], ng_ref[...]) * (1.0 + sc_ref[0]) + sh_ref[0]
        hb = h.astype(BF16)
        return lambda g: _dot(hb, w_ref[:, g * W:(g + 1) * W])

    @pl.when((p == 0) & (j == 0))
    def _():
        if zero_init:
            sf_run[...] = jnp.zeros_like(sf_run)
            sb_run[...] = jnp.zeros_like(sb_run)
        else:
            sf_run[...] = s0f_ref[0]
            sb_run[...] = s0b_ref[0]

    @pl.when(p == 0)
    def _():
        blk = nb - 1 - j
        r0 = pl.multiple_of(blk * tb, tb)
        proj = project()
        k = (proj(1) * (RET_DK ** -0.5)).astype(BF16)
        v = proj(2).astype(BF16)
        k_all[pl.ds(r0, tb), :] = k
        v_all[pl.ds(r0, tb), :] = v
        z_all[pl.ds(r0, tb), :] = (proj(5) * proj(6)).astype(BF16)
        for cc in reversed(range(nc)):
            gc = blk * nc + cc
            for h in range(H):
                rows, cols = tile(cc, h)
                sb_all[gc, h] = sb_run[h].astype(BF16)
                kd = (k[rows, cols].astype(F32) * dec_ref[3, h]).astype(BF16)
                sb_run[h] = sb_run[h] * cdec_ref[1, h] + _dot_tn(kd, v[rows, cols])

        @pl.when(j == nb - 1)
        def _():
            nsb_ref[0] = sb_run[...]

    @pl.when(p == 1)
    def _():
        r0 = pl.multiple_of(j * tb, tb)
        proj = project()
        q = proj(0).astype(BF16)
        sga = _silu(proj(3))
        k = k_all[pl.ds(r0, tb), :]
        v = v_all[pl.ds(r0, tb), :]
        items = [(cc, h) for cc in range(nc) for h in range(H)]

        def scores(cc, h):
            rows, cols = tile(cc, h)
            return _dot_nt(q[rows, cols], k[rows, cols])

        s_next = scores(*items[0])
        for i, (cc, h) in enumerate(items):
            s = s_next
            if i + 1 < len(items):
                s_next = scores(*items[i + 1])
            gc = j * nc + cc
            rows, cols = tile(cc, h)
            qh, kh, vh = q[rows, cols], k[rows, cols], v[rows, cols]
            inter = (_dot(qh, sf_run[h].astype(BF16)) * dec_ref[0, h]
                     + _dot(qh, sb_all[gc, h]) * dec_ref[1, h])
            kd = (kh.astype(F32) * dec_ref[2, h]).astype(BF16)
            sf_run[h] = sf_run[h] * cdec_ref[0, h] + _dot_tn(kd, vh)
            att = (s * mask_ref[h]).astype(BF16)
            o = _dot(att, vh) + inter
            on = o * lax.rsqrt(jnp.mean(o * o, axis=-1, keepdims=True) + EPS)
            ymix[rows, cols] = (on * sga[rows, cols]).astype(BF16)

        u = _silu(proj(7)) * proj(4)
        z = z_all[pl.ds(r0, tb), :].astype(F32)
        row = lax.broadcasted_iota(jnp.int32, z.shape, 0)
        n = BF16_SUBLANES
        r_prev = pl.multiple_of(jnp.maximum(r0 - n, 0), n)
        r_next = pl.multiple_of(jnp.minimum(r0 + tb, T - n), n)
        prev_row = z_all[pl.ds(r_prev, n), :][n - 1:n, :].astype(F32)
        next_row = z_all[pl.ds(r_next, n), :][0:1, :].astype(F32)
        prev_row = jnp.where(j > 0, prev_row, 0.0)
        next_row = jnp.where(j < nb - 1, next_row, 0.0)
        z_prev = jnp.where(row == 0, prev_row, pltpu.roll(z, 1, 0))
        z_next = jnp.where(row == tb - 1, next_row, pltpu.roll(z, tb - 1, 0))
        zc = z_prev * cw_ref[0:1, :] + z * cw_ref[1:2, :] + z_next * cw_ref[2:3, :]
        ymix[:, W:] = (u * zc).astype(BF16)

        y = _dot(ymix[...], wo_ref[...])
        xo_ref[0] = x_ref[0] + g_ref[0] * y

        @pl.when(j == nb - 1)
        def _():
            nsf_ref[0] = sf_run[...]


def _even_layer_call(x, sh, sc, g, ng, w_in, conv_w, w_out, mask, dec, cdec, s0f, s0b, tb):
    B, T, _ = x.shape
    nb = T // tb
    nc = tb // RET_CHUNK
    zero_init = s0f is None
    state_block = (1, RET_HEADS, RET_DK, RET_DV)
    if zero_init:
        s0f = s0b = jnp.zeros(state_block, F32)
        s0_spec = pl.BlockSpec(state_block, lambda b, p, j: (0, 0, 0, 0))
    else:
        s0_spec = pl.BlockSpec(state_block, lambda b, p, j: (b, 0, 0, 0))

    def const(a):
        return pl.BlockSpec(a.shape, lambda b, p, j: (0,) * a.ndim, pipeline_mode=pl.Buffered(1))

    kernel = functools.partial(_even_layer_kernel, nc=nc, nb=nb, zero_init=zero_init)
    state_shape = jax.ShapeDtypeStruct((B,) + state_block[1:], F32)
    state_spec = pl.BlockSpec(state_block, lambda b, p, j: (b, 0, 0, 0))
    seq_buf = pltpu.VMEM((T, RET_WIDTH), BF16)
    return pl.pallas_call(
        kernel,
        grid=(B, 2, nb),
        in_specs=[
            pl.BlockSpec((1, tb, D_MODEL), lambda b, p, j: (b, p * j + (1 - p) * (nb - 1 - j), 0)),
            _mod_spec(sh), _mod_spec(sc), _mod_spec(g),
            const(ng), const(w_in), const(conv_w), const(w_out), const(mask), const(dec),
            pl.BlockSpec(memory_space=pltpu.SMEM),
            s0_spec, s0_spec,
        ],
        out_specs=[
            pl.BlockSpec((1, tb, D_MODEL), lambda b, p, j: (b, p * j, 0)),
            state_spec, state_spec,
        ],
        out_shape=[jax.ShapeDtypeStruct((B, T, D_MODEL), F32), state_shape, state_shape],
        scratch_shapes=[
            seq_buf, seq_buf, seq_buf,
            pltpu.VMEM((T // RET_CHUNK, RET_HEADS, RET_DK, RET_DV), BF16),
            pltpu.VMEM((RET_HEADS, RET_DK, RET_DV), F32),
            pltpu.VMEM((RET_HEADS, RET_DK, RET_DV), F32),
            pltpu.VMEM((tb, D_MODEL), BF16),
        ],
        compiler_params=_params("parallel", "arbitrary", "arbitrary"),
        name="even_layer",
    )(x, sh.table, sc.table, g.table, ng, w_in, conv_w, w_out, mask, dec, cdec, s0f, s0b)


def _retention_tables():
    C = RET_CHUNK
    hh = jnp.arange(RET_HEADS, dtype=F32)
    lg_f = jnp.log(1.0 - 2.0 ** (-5.0 - hh))
    lg_b = jnp.log(1.0 - 2.0 ** (-5.0 - hh - RET_BWD_OFFSET))
    idx = jnp.arange(C, dtype=F32)
    diff = idx[:, None] - idx[None, :]
    m_f = jnp.where(diff >= 0, jnp.exp(lg_f[:, None, None] * jnp.maximum(diff, 0.0)), 0.0)
    m_b = jnp.where(diff <= 0, jnp.exp(lg_b[:, None, None] * jnp.maximum(-diff, 0.0)), 0.0)
    mask = m_f + m_b
    q_f = jnp.exp(lg_f[:, None] * (idx[None, :] + 1.0))
    q_b = jnp.exp(lg_b[:, None] * (C - idx[None, :]))
    k_f = jnp.exp(lg_f[:, None] * (C - 1.0 - idx[None, :]))
    k_b = jnp.exp(lg_b[:, None] * idx[None, :])
    dec = jnp.stack([q_f, q_b, k_f, k_b])
    dec = jnp.broadcast_to(dec[..., None], dec.shape + (RET_DV,))
    cdec = jnp.stack([jnp.exp(lg_f * C), jnp.exp(lg_b * C)])
    return mask, dec, cdec


def _store_vt_tiles(v_ref, hd, vt):
    v = vt[hd * V_HEAD:(hd + 1) * V_HEAD, :].astype(BF16)
    for j in range(v.shape[1] // MXU_DIM):
        v_ref[0, hd, j] = v[:, j * MXU_DIM:(j + 1) * MXU_DIM]


def _rope_mix(r, cs_ref, first):
    if cs_ref is not None:
        t = r * cs_ref[...]
        r = t + pltpu.roll(t, QK_ROPE, 1)
    return jnp.where(first, r, 0.0)


def _mla_in_kernel(*refs, rope, emit_cache, tq):
    (x_ref, sh_ref, sc_ref, ng_ref, wlat_ref, wg_ref, qng_ref, kvng_ref, qupt_ref,
     kup_ref, vupt_ref) = refs[:11]
    pos = 11
    cs_ref = cst_ref = None
    if rope:
        cs_ref, cst_ref = refs[pos:pos + 2]
        pos += 2
    q_ref, k_ref, v_ref, sg_ref = refs[pos:pos + 4]
    pos += 4
    if emit_cache:
        ckv_ref, kr_ref = refs[pos:pos + 2]

    h = _rms_rows(x_ref[0], ng_ref[...]) * (1.0 + sc_ref[0]) + sh_ref[0]
    hb = h.astype(BF16)
    lat = _dot(hb, wlat_ref[...])
    kr2 = lat[:, Q_LORA + KV_LORA:]
    sg = _silu(_dot(hb, wg_ref[...])).astype(BF16)
    for hd in range(MLA_HEADS):
        sg_ref[0, hd] = sg[:, hd * V_HEAD:(hd + 1) * V_HEAD]
    qn = _rms_rows(lat[:, :Q_LORA], qng_ref[...]).astype(BF16)
    ckv = _rms_rows(lat[:, Q_LORA:Q_LORA + KV_LORA], kvng_ref[...])
    qt = _dot_nt(qupt_ref[...], qn)
    ckv_b = ckv.astype(BF16)
    kn = _dot(ckv_b, kup_ref[...])
    vt = _dot_nt(vupt_ref[...], ckv_b)

    scale = (QK_NOPE + QK_ROPE) ** -0.5 * LOG2_E
    tm = kr2.shape[0]
    first = lax.broadcasted_iota(jnp.int32, kr2.shape, 1) < QK_ROPE
    k_rot = _rope_mix(kr2, cs_ref, first).astype(BF16)
    q_pad = jnp.zeros((QK_PAD - QK_NOPE - QK_ROPE, tq), BF16)
    f = QK_ROPE // 4
    for hd in range(MLA_HEADS):
        base = hd * (QK_NOPE + QK_ROPE)
        q_nope = (qt[base:base + QK_NOPE, :] * scale).astype(BF16)
        q_rot = qt[base + QK_NOPE:base + QK_NOPE + QK_ROPE, :]
        if rope:
            partner = jnp.concatenate([q_rot[f:2 * f], q_rot[0:f], q_rot[3 * f:4 * f], q_rot[2 * f:3 * f]], axis=0)
            q_rot = q_rot * cst_ref[0:QK_ROPE, :] + partner * cst_ref[QK_ROPE:2 * QK_ROPE, :]
        q_rot = (q_rot * scale).astype(BF16)
        for j in range(tm // tq):
            cols = slice(j * tq, (j + 1) * tq)
            q_ref[0, hd, j, 0:QK_NOPE, :] = q_nope[:, cols]
            q_ref[0, hd, j, QK_NOPE:QK_NOPE + QK_ROPE, :] = q_rot[:, cols]
            q_ref[0, hd, j, QK_NOPE + QK_ROPE:QK_PAD, :] = q_pad
        k_ref[0, hd, :, 0:QK_NOPE] = kn[:, hd * QK_NOPE:(hd + 1) * QK_NOPE].astype(BF16)
        k_ref[0, hd, :, QK_NOPE:QK_PAD] = k_rot
        _store_vt_tiles(v_ref, hd, vt)
    if emit_cache:
        ckv_ref[0] = ckv
        kr_ref[0] = kr2[:, :QK_ROPE]


def _mla_in_call(x, sh, sc, ng, wts, cs, tm, tq, emit_cache):
    B, T, _ = x.shape
    rope = cs is not None
    wlat, wg, qng, kvng, qup, kup, vupt = wts

    def full(a):
        return pl.BlockSpec(a.shape, lambda b, i: (0,) * a.ndim)

    in_specs = [
        pl.BlockSpec((1, tm, D_MODEL), lambda b, i: (b, i, 0)),
        _mod_spec(sh), _mod_spec(sc), full(ng),
        full(wlat), full(wg), full(qng), full(kvng), full(qup), full(kup), full(vupt),
    ]
    args = [x, sh.table, sc.table, ng, wlat, wg, qng, kvng, qup, kup, vupt]
    if rope:
        in_specs += [pl.BlockSpec((tm, 2 * QK_ROPE), lambda b, i: (i, 0)),
                     pl.BlockSpec((2 * QK_ROPE, tm), lambda b, i: (0, i))]
        f = QK_ROPE // 4
        sign = jnp.concatenate([jnp.ones((QK_ROPE,), F32)] + [-jnp.ones((f,), F32), jnp.ones((f,), F32)] * 2)
        args += [cs, (cs * sign).T]
    out_specs = [pl.BlockSpec((1, MLA_HEADS, tm // tq, QK_PAD, tq), lambda b, i: (b, 0, i, 0, 0)),
                 pl.BlockSpec((1, MLA_HEADS, tm, QK_PAD), lambda b, i: (b, 0, i, 0)),
                 pl.BlockSpec((1, MLA_HEADS, tm // MXU_DIM, V_HEAD, MXU_DIM), lambda b, i: (b, 0, i, 0, 0)),
                 pl.BlockSpec((1, MLA_HEADS, tm, V_HEAD), lambda b, i: (b, 0, i, 0))]
    out_shape = [jax.ShapeDtypeStruct((B, MLA_HEADS, T // tq, QK_PAD, tq), BF16),
                 jax.ShapeDtypeStruct((B, MLA_HEADS, T, QK_PAD), BF16),
                 jax.ShapeDtypeStruct((B, MLA_HEADS, T // MXU_DIM, V_HEAD, MXU_DIM), BF16),
                 jax.ShapeDtypeStruct((B, MLA_HEADS, T, V_HEAD), BF16)]
    if emit_cache:
        out_specs += [pl.BlockSpec((1, tm, KV_LORA), lambda b, i: (b, i, 0)),
                      pl.BlockSpec((1, tm, QK_ROPE), lambda b, i: (b, i, 0))]
        out_shape += [jax.ShapeDtypeStruct((B, T, KV_LORA), F32),
                      jax.ShapeDtypeStruct((B, T, QK_ROPE), F32)]
    return pl.pallas_call(
        functools.partial(_mla_in_kernel, rope=rope, emit_cache=emit_cache, tq=tq),
        grid=(B, T // tm),
        in_specs=in_specs, out_specs=out_specs, out_shape=out_shape,
        compiler_params=_params("parallel", "arbitrary"),
        name="mla_in",
    )(*args)


def _ctx_expand_kernel(ckv_ref, kr_ref, kup_ref, vupt_ref, k_ref, v_ref):
    ckv_b = ckv_ref[0].astype(BF16)
    kn = _dot(ckv_b, kup_ref[...])
    vt = _dot_nt(vupt_ref[...], ckv_b)
    kr = kr_ref[0].astype(BF16)
    zeros = jnp.zeros_like(kr)
    for hd in range(MLA_HEADS):
        k_ref[0, hd, :, 0:QK_NOPE] = kn[:, hd * QK_NOPE:(hd + 1) * QK_NOPE].astype(BF16)
        k_ref[0, hd, :, QK_NOPE:QK_NOPE + QK_ROPE] = kr
        k_ref[0, hd, :, QK_NOPE + QK_ROPE:QK_PAD] = zeros
        _store_vt_tiles(v_ref, hd, vt)


def _ctx_expand_call(ckv, kr, kup, vupt):
    B, L, _ = ckv.shape
    return pl.pallas_call(
        _ctx_expand_kernel,
        grid=(B,),
        in_specs=[
            pl.BlockSpec((1, L, KV_LORA), lambda b: (b, 0, 0)),
            pl.BlockSpec((1, L, QK_ROPE), lambda b: (b, 0, 0)),
            pl.BlockSpec(kup.shape, lambda b: (0, 0)),
            pl.BlockSpec(vupt.shape, lambda b: (0, 0)),
        ],
        out_specs=[pl.BlockSpec((1, MLA_HEADS, L, QK_PAD), lambda b: (b, 0, 0, 0)),
                   pl.BlockSpec((1, MLA_HEADS, L // MXU_DIM, V_HEAD, MXU_DIM), lambda b: (b, 0, 0, 0, 0))],
        out_shape=[jax.ShapeDtypeStruct((B, MLA_HEADS, L, QK_PAD), BF16),
                   jax.ShapeDtypeStruct((B, MLA_HEADS, L // MXU_DIM, V_HEAD, MXU_DIM), BF16)],
        compiler_params=_params("parallel"),
        name="ctx_expand",
    )(ckv, kr, kup, vupt)


ATTN_UNROLL = 4
ATTN_LEAD = 4


def _attn_ctx_kernel(q_ref, kc_ref, kl_ref, vc_ref, vl_ref, sg_ref, o_ref, s0_scr, s1_scr):
    hp, nq, _, tq = q_ref.shape[1:]
    T = nq * tq
    Lc = kc_ref.shape[2]
    tiles = ([(kc_ref, vc_ref, j * MXU_DIM) for j in range(Lc // MXU_DIM)]
             + [(kl_ref, vl_ref, j * MXU_DIM) for j in range(T // MXU_DIM)])
    groups = MXU_DIM // F32_SUBLANES

    def score_tile(hd, qb, t, s_dst):
        kref, _, off = tiles[t]
        s = _dot(kref[0, hd, off:off + MXU_DIM, :], qb)
        s_dst[t * MXU_DIM:(t + 1) * MXU_DIM, :] = s

    def tile_max(s_buf, t):
        s = s_buf[t * MXU_DIM:(t + 1) * MXU_DIM, :]
        return jnp.max(s.reshape(groups, F32_SUBLANES, tq), axis=0)

    def step(blk, blk_next, m_cur, s_cur, s_nxt):
        scoring = blk_next is not None
        hd, qi = blk // nq, blk % nq
        row = pl.multiple_of(qi * tq, tq)
        if scoring:
            hd_next = blk_next // nq
            q_next = q_ref[0, hd_next, blk_next % nq]
            for t in range(ATTN_LEAD):
                score_tile(hd_next, q_next, t, s_nxt)
        acc = lacc = None
        parts = []
        for t, (_, vref, off) in enumerate(tiles):
            p = jnp.exp2(s_cur[t * MXU_DIM:(t + 1) * MXU_DIM, :] - m_cur)
            ps = jnp.sum(p.reshape(groups, F32_SUBLANES, tq), axis=0)
            lacc = ps if lacc is None else lacc + ps
            pv = _dot(vref[0, hd, off // MXU_DIM], p.astype(BF16))
            acc = pv if acc is None else acc + pv
            if scoring:
                if t + ATTN_LEAD < len(tiles):
                    score_tile(hd_next, q_next, t + ATTN_LEAD, s_nxt)
                parts.append(tile_max(s_nxt, t))
        l = jnp.sum(lacc, axis=0, keepdims=True)
        gate = sg_ref[0, hd, pl.ds(row, tq), :].astype(F32)
        o_ref[0, hd, pl.ds(row, tq), :] = ((acc * (1.0 / l)).T * gate).astype(BF16)
        if not scoring:
            return None
        return jnp.max(functools.reduce(jnp.maximum, parts), axis=0, keepdims=True)

    q0 = q_ref[0, 0, 0]
    for t in range(len(tiles)):
        score_tile(0, q0, t, s0_scr)
    mrun = functools.reduce(jnp.maximum, [tile_max(s0_scr, t) for t in range(len(tiles))])
    m0 = jnp.max(mrun, axis=0, keepdims=True)

    def trip(i, m, last):
        bufs = (s0_scr, s1_scr)
        for u in range(ATTN_UNROLL):
            blk = i * ATTN_UNROLL + u
            blk_next = None if (last and u == ATTN_UNROLL - 1) else blk + 1
            m = step(blk, blk_next, m, bufs[u % 2], bufs[(u + 1) % 2])
        return m

    trips = hp * nq // ATTN_UNROLL
    m = lax.fori_loop(0, trips - 1, lambda i, m: trip(i, m, False), m0)
    trip(trips - 1, m, True)


ATTN_HEADS_PER_STEP = 2


def _attn_ctx_call(q, kc, kl, vc, vl, sg):
    B, H, nq, _, tq = q.shape
    T = nq * tq
    Lc = kc.shape[2]
    hp = ATTN_HEADS_PER_STEP
    assert nq % ATTN_UNROLL == 0 and ATTN_UNROLL % 2 == 0 and H % hp == 0

    def heads(*tail):
        return pl.BlockSpec((1, hp) + tail, lambda b, h: (b, h) + (0,) * len(tail))

    return pl.pallas_call(
        _attn_ctx_kernel,
        scratch_shapes=[pltpu.VMEM((Lc + T, tq), F32), pltpu.VMEM((Lc + T, tq), F32)],
        grid=(B, H // hp),
        in_specs=[heads(nq, QK_PAD, tq), heads(Lc, QK_PAD), heads(T, QK_PAD),
                  heads(Lc // MXU_DIM, V_HEAD, MXU_DIM), heads(T // MXU_DIM, V_HEAD, MXU_DIM),
                  heads(T, V_HEAD)],
        out_specs=heads(T, V_HEAD),
        out_shape=jax.ShapeDtypeStruct((B, H, T, V_HEAD), BF16),
        compiler_params=_params("parallel", "arbitrary"),
        name="attn_ctx",
    )(q, kc, kl, vc, vl, sg)


def _attn_self_kernel(q_ref, k_ref, v_ref, sg_ref, o_ref):
    heads = range(MLA_HEADS)
    s = [_dot(k_ref[0, hd], q_ref[0, hd, 0]) for hd in heads]
    p = [jnp.exp2(a - jnp.max(a, axis=0, keepdims=True)) for a in s]
    inv = [1.0 / jnp.sum(a, axis=0, keepdims=True) for a in p]
    o = [_dot(v_ref[0, hd, 0], p[hd].astype(BF16)) * inv[hd] for hd in heads]
    for hd in heads:
        o_ref[0, hd] = (o[hd].T * sg_ref[0, hd].astype(F32)).astype(BF16)


def _attn_self_call(q, k, v, sg):
    _, H, nseq, _, seq = q.shape
    N = nseq * seq
    assert seq == MXU_DIM
    return pl.pallas_call(
        _attn_self_kernel,
        grid=(nseq,),
        in_specs=[
            pl.BlockSpec((1, H, 1, QK_PAD, seq), lambda b: (0, 0, b, 0, 0)),
            pl.BlockSpec((1, H, seq, QK_PAD), lambda b: (0, 0, b, 0)),
            pl.BlockSpec((1, H, 1, V_HEAD, seq), lambda b: (0, 0, b, 0, 0)),
            pl.BlockSpec((1, H, seq, V_HEAD), lambda b: (0, 0, b, 0)),
        ],
        out_specs=pl.BlockSpec((1, H, seq, V_HEAD), lambda b: (0, 0, b, 0)),
        out_shape=jax.ShapeDtypeStruct((1, H, N, V_HEAD), BF16),
        compiler_params=_params("parallel"),
        name="attn_self",
    )(q, k, v, sg)


def _mla_out_kernel(o_ref, x_ref, g_ref, wo_ref, fg_ref, y_ref):
    o = jnp.concatenate([o_ref[0, hd] for hd in range(MLA_HEADS)], axis=1)
    x2 = x_ref[0] + g_ref[0] * _dot(o, wo_ref[...])
    y_ref[0] = _rms_rows(x2, fg_ref[...])


def _mla_out_call(o, x, g, w_out, fg, tm):
    B, T, _ = x.shape
    tok = lambda: pl.BlockSpec((1, tm, D_MODEL), lambda b, i: (b, i, 0))
    return pl.pallas_call(
        _mla_out_kernel,
        grid=(B, T // tm),
        in_specs=[pl.BlockSpec((1, MLA_HEADS, tm, V_HEAD), lambda b, i: (b, 0, i, 0)), tok(), _mod_spec(g),
                  pl.BlockSpec(w_out.shape, lambda b, i: (0, 0)),
                  pl.BlockSpec((1, D_MODEL), lambda b, i: (0, 0))],
        out_specs=tok(),
        out_shape=jax.ShapeDtypeStruct((B, T, D_MODEL), F32),
        compiler_params=_params("parallel", "arbitrary"),
        name="mla_out",
    )(o, x, g.table, w_out, fg)


def _swap_halves(w):
    f = QK_ROPE // 4
    return jnp.concatenate([-w[..., f:2 * f], w[..., 0:f], -w[..., 3 * f:4 * f], w[..., 2 * f:3 * f]], axis=-1)


def _rope_table(T):
    f = QK_ROPE // 4
    rows = T // GRID_W
    inv = ROPE_BASE ** (-jnp.arange(f, dtype=F32) / f)
    ar = jnp.arange(rows, dtype=F32)[:, None] * inv
    ac = jnp.arange(GRID_W, dtype=F32)[:, None] * inv

    def by_row(a):
        return jnp.broadcast_to(a[:, None, :], (rows, GRID_W, f)).reshape(T, f)

    def by_col(a):
        return jnp.broadcast_to(a[None, :, :], (rows, GRID_W, f)).reshape(T, f)

    cr, sr, cc, sc = by_row(jnp.cos(ar)), by_row(jnp.sin(ar)), by_col(jnp.cos(ac)), by_col(jnp.sin(ac))
    return jnp.concatenate([cr, cr, cc, cc, sr, sr, sc, sc], axis=-1)


def kernel(x_prompt, x_sample, c, state_ret_fwd, state_ret_bwd, cache_mla_ckv, cache_mla_krope, c_ctx,
           ada_w, ada_b, norm_g, even_in_w, even_conv_w, even_out_w, odd_in_w, odd_q_norm_g,
           odd_kv_norm_g, odd_q_up_w, odd_kv_up_w, odd_out_w, final_norm_g):
    BP, SEQ, D = x_prompt.shape
    BS, TS, _ = x_sample.shape
    NP = BP * SEQ

    cvec = jnp.concatenate([c, c_ctx[None, :], jnp.zeros((ADA_ROWS - BS - 1, D), F32)], axis=0)
    mod = _ada_call(cvec, ada_w, ada_b).reshape(-1, ADA_ROWS, 3, 1, D)

    def mods(l):
        return ([Mod(mod, l, i, BS) for i in range(3)], [Mod(mod, l, i, None) for i in range(3)])

    (sh_p, sc_p, g_p), (sh_s, sc_s, g_s) = mods(0)
    ng = norm_g[0][None, :]
    w_in = even_in_w[0].astype(BF16)
    w_out = even_out_w[0].astype(BF16)
    conv_w = even_conv_w[0]
    mask, dec, cdec = _retention_tables()

    xp1, nsf, nsb = _even_layer_call(x_prompt, sh_p, sc_p, g_p, ng, w_in, conv_w, w_out,
                                     mask, dec, cdec, None, None, SEQ)
    xs1, _, _ = _even_layer_call(x_sample, sh_s, sc_s, g_s, ng, w_in, conv_w, w_out, mask, dec, cdec,
                                 state_ret_fwd[:, 0], state_ret_bwd[:, 0], TOKEN_BLOCK)

    (sh_p, sc_p, g_p), (sh_s, sc_s, g_s) = mods(1)
    ng = norm_g[1][None, :]
    w_in = odd_in_w[0]
    nq = Q_LORA + KV_LORA
    wkr = w_in[:, nq:nq + QK_ROPE]
    wlat = jnp.concatenate([w_in[:, :nq + QK_ROPE], _swap_halves(wkr)], axis=-1).astype(BF16)
    wg = w_in[:, nq + QK_ROPE:].astype(BF16)
    qup = odd_q_up_w[0].T.astype(BF16)
    kvup = odd_kv_up_w[0].reshape(KV_LORA, MLA_HEADS, QK_NOPE + V_HEAD)
    kup = kvup[..., :QK_NOPE].reshape(KV_LORA, MLA_HEADS * QK_NOPE).astype(BF16)
    vupt = kvup[..., QK_NOPE:].reshape(KV_LORA, MLA_HEADS * V_HEAD).T.astype(BF16)
    wts = (wlat, wg, odd_q_norm_g[0][None, :], odd_kv_norm_g[0][None, :], qup, kup, vupt)
    w_out = odd_out_w[0].astype(BF16)
    fg = final_norm_g[None, :]

    xp1f = xp1.reshape(1, NP, D)
    q_p, k_p, v_p, sg_p, ckv_p, kr_p = _mla_in_call(xp1f, sh_p, sc_p, ng, wts, None, TOKEN_BLOCK, SEQ, True)
    q_s, k_s, v_s, sg_s = _mla_in_call(xs1, sh_s, sc_s, ng, wts, _rope_table(TS), TOKEN_BLOCK, MXU_DIM, False)
    k_c, v_c = _ctx_expand_call(cache_mla_ckv[:, 0], cache_mla_krope[:, 0], kup, vupt)

    o_p = _attn_self_call(q_p, k_p, v_p, sg_p)
    o_s = _attn_ctx_call(q_s, k_c, k_s, v_c, v_s, sg_s)

    y_p = _mla_out_call(o_p, xp1f, g_p, w_out, fg, OUT_BLOCK)
    y_s = _mla_out_call(o_s, xs1, g_s, w_out, fg, OUT_BLOCK)

    return (y_p.reshape(BP, SEQ, D), y_s,
            nsf[:, None], nsb[:, None],
            ckv_p.reshape(BP, 1, SEQ, KV_LORA), kr_p.reshape(BP, 1, SEQ, QK_ROPE))
```

```python
import functools
from typing import NamedTuple, Optional

import jax
import jax.numpy as jnp
from jax import lax
from jax.experimental import pallas as pl
from jax.experimental.pallas import tpu as pltpu

F32 = jnp.float32
BF16 = jnp.bfloat16

D_MODEL = 1024
GRID_W = 64
EPS = 1e-6
RET_HEADS = 4
RET_DK = 128
RET_DV = 128
RET_WIDTH = RET_HEADS * RET_DV
RET_CHUNK = 128
RET_BWD_OFFSET = 0.5
CONV_WIDTH = D_MODEL - RET_WIDTH
MLA_HEADS = 8
QK_NOPE = 128
QK_ROPE = 64
V_HEAD = 128
Q_LORA = 384
KV_LORA = 256
ROPE_BASE = 10000.0
QK_PAD = 256
ADA_ROWS = 16
BF16_SUBLANES = 16
F32_SUBLANES = 8
MXU_DIM = 256
LOG2_E = 1.4426950408889634
VMEM_LIMIT = 56 * 1024 * 1024
TOKEN_BLOCK = 512
OUT_BLOCK = 1024


def _silu(x):
    return x * (1.0 / (1.0 + jnp.exp(-x)))


def _rms_rows(x, g):
    return x * lax.rsqrt(jnp.mean(x * x, axis=-1, keepdims=True) + EPS) * g


def _dot(a, b):
    return jnp.dot(a, b, preferred_element_type=F32)


def _dot_nt(a, b):
    return lax.dot_general(a, b, (((1,), (1,)), ((), ())), preferred_element_type=F32)


def _dot_tn(a, b):
    return lax.dot_general(a, b, (((0,), (0,)), ((), ())), preferred_element_type=F32)


def _params(*sem):
    return pltpu.CompilerParams(dimension_semantics=sem, vmem_limit_bytes=VMEM_LIMIT)


def _ada_kernel(c_ref, w_ref, b_ref, o_ref):
    s = _silu(c_ref[...]).astype(BF16)
    o_ref[0] = _dot(s, w_ref[0].astype(BF16)) + b_ref[0]


def _ada_call(cvec, ada_w, ada_b):
    depth = ada_w.shape[0]
    return pl.pallas_call(
        _ada_kernel,
        grid=(depth, 3),
        in_specs=[
            pl.BlockSpec((ADA_ROWS, D_MODEL), lambda l, j: (0, 0)),
            pl.BlockSpec((1, D_MODEL, D_MODEL), lambda l, j: (l, 0, j)),
            pl.BlockSpec((1, 1, D_MODEL), lambda l, j: (l, 0, j)),
        ],
        out_specs=pl.BlockSpec((1, ADA_ROWS, D_MODEL), lambda l, j: (l, 0, j)),
        out_shape=jax.ShapeDtypeStruct((depth, ADA_ROWS, 3 * D_MODEL), F32),
        compiler_params=_params("arbitrary", "arbitrary"),
        name="ada",
    )(cvec, ada_w, ada_b.reshape(depth, 1, 3 * D_MODEL))


class Mod(NamedTuple):
    table: jax.Array
    layer: int
    part: int
    row: Optional[int]


def _mod_spec(m):
    row = (lambda b: b) if m.row is None else (lambda b: m.row)
    return pl.BlockSpec((None, None, 1, 1, D_MODEL), lambda b, *_: (m.layer, row(b), m.part, 0, 0))


def _even_layer_kernel(x_ref, sh_ref, sc_ref, g_ref, ng_ref, w_ref, cw_ref, wo_ref, mask_ref, dec_ref,
                       cdec_ref, s0f_ref, s0b_ref,
                       xo_ref, nsf_ref, nsb_ref,
                       k_all, v_all, z_all, sb_all, sf_run, sb_run, ymix,
                       *, nc, nb, zero_init):
    p = pl.program_id(1)
    j = pl.program_id(2)
    C = RET_CHUNK
    H = RET_HEADS
    dv = RET_DV
    W = RET_WIDTH
    tb = nc * C
    T = nb * tb

    def tile(cc, h):
        return slice(cc * C, (cc + 1) * C), slice(h * dv, (h + 1) * dv)

    def project():
        h = _rms_rows(x_ref[0], ng_ref[...]) * (1.0 + sc_ref[0]) + sh_ref[0]
        hb = h.astype(BF16)
        return lambda g: _dot(hb, w_ref[:, g * W:(g + 1) * W])

    @pl.when((p == 0) & (j == 0))
    def _():
        if zero_init:
            sf_run[...] = jnp.zeros_like(sf_run)
            sb_run[...] = jnp.zeros_like(sb_run)
        else:
            sf_run[...] = s0f_ref[0]
            sb_run[...] = s0b_ref[0]

    @pl.when(p == 0)
    def _():
        blk = nb - 1 - j
        r0 = pl.multiple_of(blk * tb, tb)
        proj = project()
        k = (proj(1) * (RET_DK ** -0.5)).astype(BF16)
        v = proj(2).astype(BF16)
        k_all[pl.ds(r0, tb), :] = k
        v_all[pl.ds(r0, tb), :] = v
        z_all[pl.ds(r0, tb), :] = (proj(5) * proj(6)).astype(BF16)
        for cc in reversed(range(nc)):
            gc = blk * nc + cc
            for h in range(H):
                rows, cols = tile(cc, h)
                sb_all[gc, h] = sb_run[h].astype(BF16)
                kd = (k[rows, cols].astype(F32) * dec_ref[3, h]).astype(BF16)
                sb_run[h] = sb_run[h] * cdec_ref[1, h] + _dot_tn(kd, v[rows, cols])

        @pl.when(j == nb - 1)
        def _():
            nsb_ref[0] = sb_run[...]

    @pl.when(p == 1)
    def _():
        r0 = pl.multiple_of(j * tb, tb)
        proj = project()
        q = proj(0).astype(BF16)
        sga = _silu(proj(3))
        k = k_all[pl.ds(r0, tb), :]
        v = v_all[pl.ds(r0, tb), :]
        items = [(cc, h) for cc in range(nc) for h in range(H)]

        def scores(cc, h):
            rows, cols = tile(cc, h)
            return _dot_nt(q[rows, cols], k[rows, cols])

        s_next = scores(*items[0])
        for i, (cc, h) in enumerate(items):
            s = s_next
            if i + 1 < len(items):
                s_next = scores(*items[i + 1])
            gc = j * nc + cc
            rows, cols = tile(cc, h)
            qh, kh, vh = q[rows, cols], k[rows, cols], v[rows, cols]
            inter = (_dot(qh, sf_run[h].astype(BF16)) * dec_ref[0, h]
                     + _dot(qh, sb_all[gc, h]) * dec_ref[1, h])
            kd = (kh.astype(F32) * dec_ref[2, h]).astype(BF16)
            sf_run[h] = sf_run[h] * cdec_ref[0, h] + _dot_tn(kd, vh)
            att = (s * mask_ref[h]).astype(BF16)
            o = _dot(att, vh) + inter
            on = o * lax.rsqrt(jnp.mean(o * o, axis=-1, keepdims=True) + EPS)
            ymix[rows, cols] = (on * sga[rows, cols]).astype(BF16)

        u = _silu(proj(7)) * proj(4)
        z = z_all[pl.ds(r0, tb), :].astype(F32)
        row = lax.broadcasted_iota(jnp.int32, z.shape, 0)
        n = BF16_SUBLANES
        r_prev = pl.multiple_of(jnp.maximum(r0 - n, 0), n)
        r_next = pl.multiple_of(jnp.minimum(r0 + tb, T - n), n)
        prev_row = z_all[pl.ds(r_prev, n), :][n - 1:n, :].astype(F32)
        next_row = z_all[pl.ds(r_next, n), :][0:1, :].astype(F32)
        prev_row = jnp.where(j > 0, prev_row, 0.0)
        next_row = jnp.where(j < nb - 1, next_row, 0.0)
        z_prev = jnp.where(row == 0, prev_row, pltpu.roll(z, 1, 0))
        z_next = jnp.where(row == tb - 1, next_row, pltpu.roll(z, tb - 1, 0))
        zc = z_prev * cw_ref[0:1, :] + z * cw_ref[1:2, :] + z_next * cw_ref[2:3, :]
        ymix[:, W:] = (u * zc).astype(BF16)

        y = _dot(ymix[...], wo_ref[...])
        xo_ref[0] = x_ref[0] + g_ref[0] * y

        @pl.when(j == nb - 1)
        def _():
            nsf_ref[0] = sf_run[...]


def _even_layer_call(x, sh, sc, g, ng, w_in, conv_w, w_out, mask, dec, cdec, s0f, s0b, tb):
    B, T, _ = x.shape
    nb = T // tb
    nc = tb // RET_CHUNK
    zero_init = s0f is None
    state_block = (1, RET_HEADS, RET_DK, RET_DV)
    if zero_init:
        s0f = s0b = jnp.zeros(state_block, F32)
        s0_spec = pl.BlockSpec(state_block, lambda b, p, j: (0, 0, 0, 0))
    else:
        s0_spec = pl.BlockSpec(state_block, lambda b, p, j: (b, 0, 0, 0))

    def const(a):
        return pl.BlockSpec(a.shape, lambda b, p, j: (0,) * a.ndim, pipeline_mode=pl.Buffered(1))

    kernel = functools.partial(_even_layer_kernel, nc=nc, nb=nb, zero_init=zero_init)
    state_shape = jax.ShapeDtypeStruct((B,) + state_block[1:], F32)
    state_spec = pl.BlockSpec(state_block, lambda b, p, j: (b, 0, 0, 0))
    seq_buf = pltpu.VMEM((T, RET_WIDTH), BF16)
    return pl.pallas_call(
        kernel,
        grid=(B, 2, nb),
        in_specs=[
            pl.BlockSpec((1, tb, D_MODEL), lambda b, p, j: (b, p * j + (1 - p) * (nb - 1 - j), 0)),
            _mod_spec(sh), _mod_spec(sc), _mod_spec(g),
            const(ng), const(w_in), const(conv_w), const(w_out), const(mask), const(dec),
            pl.BlockSpec(memory_space=pltpu.SMEM),
            s0_spec, s0_spec,
        ],
        out_specs=[
            pl.BlockSpec((1, tb, D_MODEL), lambda b, p, j: (b, p * j, 0)),
            state_spec, state_spec,
        ],
        out_shape=[jax.ShapeDtypeStruct((B, T, D_MODEL), F32), state_shape, state_shape],
        scratch_shapes=[
            seq_buf, seq_buf, seq_buf,
            pltpu.VMEM((T // RET_CHUNK, RET_HEADS, RET_DK, RET_DV), BF16),
            pltpu.VMEM((RET_HEADS, RET_DK, RET_DV), F32),
            pltpu.VMEM((RET_HEADS, RET_DK, RET_DV), F32),
            pltpu.VMEM((tb, D_MODEL), BF16),
        ],
        compiler_params=_params("parallel", "arbitrary", "arbitrary"),
        name="even_layer",
    )(x, sh.table, sc.table, g.table, ng, w_in, conv_w, w_out, mask, dec, cdec, s0f, s0b)


def _retention_tables():
    C = RET_CHUNK
    hh = jnp.arange(RET_HEADS, dtype=F32)
    lg_f = jnp.log(1.0 - 2.0 ** (-5.0 - hh))
    lg_b = jnp.log(1.0 - 2.0 ** (-5.0 - hh - RET_BWD_OFFSET))
    idx = jnp.arange(C, dtype=F32)
    diff = idx[:, None] - idx[None, :]
    m_f = jnp.where(diff >= 0, jnp.exp(lg_f[:, None, None] * jnp.maximum(diff, 0.0)), 0.0)
    m_b = jnp.where(diff <= 0, jnp.exp(lg_b[:, None, None] * jnp.maximum(-diff, 0.0)), 0.0)
    mask = m_f + m_b
    q_f = jnp.exp(lg_f[:, None] * (idx[None, :] + 1.0))
    q_b = jnp.exp(lg_b[:, None] * (C - idx[None, :]))
    k_f = jnp.exp(lg_f[:, None] * (C - 1.0 - idx[None, :]))
    k_b = jnp.exp(lg_b[:, None] * idx[None, :])
    dec = jnp.stack([q_f, q_b, k_f, k_b])
    dec = jnp.broadcast_to(dec[..., None], dec.shape + (RET_DV,))
    cdec = jnp.stack([jnp.exp(lg_f * C), jnp.exp(lg_b * C)])
    return mask, dec, cdec


def _store_vt_tiles(v_ref, hd, vt):
    v = vt[hd * V_HEAD:(hd + 1) * V_HEAD, :].astype(BF16)
    for j in range(v.shape[1] // MXU_DIM):
        v_ref[0, hd, j] = v[:, j * MXU_DIM:(j + 1) * MXU_DIM]


def _rope_mix(r, cs_ref, first):
    if cs_ref is not None:
        t = r * cs_ref[...]
        r = t + pltpu.roll(t, QK_ROPE, 1)
    return jnp.where(first, r, 0.0)


def _mla_in_kernel(*refs, rope, emit_cache, tq):
    (x_ref, sh_ref, sc_ref, ng_ref, wlat_ref, wg_ref, qng_ref, kvng_ref, qupt_ref,
     kup_ref, vupt_ref) = refs[:11]
    pos = 11
    cs_ref = cst_ref = None
    if rope:
        cs_ref, cst_ref = refs[pos:pos + 2]
        pos += 2
    q_ref, k_ref, v_ref, sg_ref = refs[pos:pos + 4]
    pos += 4
    if emit_cache:
        ckv_ref, kr_ref = refs[pos:pos + 2]

    h = _rms_rows(x_ref[0], ng_ref[...]) * (1.0 + sc_ref[0]) + sh_ref[0]
    hb = h.astype(BF16)
    lat = _dot(hb, wlat_ref[...])
    kr2 = lat[:, Q_LORA + KV_LORA:]
    sg = _silu(_dot(hb, wg_ref[...])).astype(BF16)
    for hd in range(MLA_HEADS):
        sg_ref[0, hd] = sg[:, hd * V_HEAD:(hd + 1) * V_HEAD]
    qn = _rms_rows(lat[:, :Q_LORA], qng_ref[...]).astype(BF16)
    ckv = _rms_rows(lat[:, Q_LORA:Q_LORA + KV_LORA], kvng_ref[...])
    qt = _dot_nt(qupt_ref[...], qn)
    ckv_b = ckv.astype(BF16)
    kn = _dot(ckv_b, kup_ref[...])
    vt = _dot_nt(vupt_ref[...], ckv_b)

    scale = (QK_NOPE + QK_ROPE) ** -0.5 * LOG2_E
    tm = kr2.shape[0]
    first = lax.broadcasted_iota(jnp.int32, kr2.shape, 1) < QK_ROPE
    k_rot = _rope_mix(kr2, cs_ref, first).astype(BF16)
    q_pad = jnp.zeros((QK_PAD - QK_NOPE - QK_ROPE, tq), BF16)
    f = QK_ROPE // 4
    for hd in range(MLA_HEADS):
        base = hd * (QK_NOPE + QK_ROPE)
        q_nope = (qt[base:base + QK_NOPE, :] * scale).astype(BF16)
        q_rot = qt[base + QK_NOPE:base + QK_NOPE + QK_ROPE, :]
        if rope:
            partner = jnp.concatenate([q_rot[f:2 * f], q_rot[0:f], q_rot[3 * f:4 * f], q_rot[2 * f:3 * f]], axis=0)
            q_rot = q_rot * cst_ref[0:QK_ROPE, :] + partner * cst_ref[QK_ROPE:2 * QK_ROPE, :]
        q_rot = (q_rot * scale).astype(BF16)
        for j in range(tm // tq):
            cols = slice(j * tq, (j + 1) * tq)
            q_ref[0, hd, j, 0:QK_NOPE, :] = q_nope[:, cols]
            q_ref[0, hd, j, QK_NOPE:QK_NOPE + QK_ROPE, :] = q_rot[:, cols]
            q_ref[0, hd, j, QK_NOPE + QK_ROPE:QK_PAD, :] = q_pad
        k_ref[0, hd, :, 0:QK_NOPE] = kn[:, hd * QK_NOPE:(hd + 1) * QK_NOPE].astype(BF16)
        k_ref[0, hd, :, QK_NOPE:QK_PAD] = k_rot
        _store_vt_tiles(v_ref, hd, vt)
    if emit_cache:
        ckv_ref[0] = ckv
        kr_ref[0] = kr2[:, :QK_ROPE]


def _mla_in_call(x, sh, sc, ng, wts, cs, tm, tq, emit_cache):
    B, T, _ = x.shape
    rope = cs is not None
    wlat, wg, qng, kvng, qup, kup, vupt = wts

    def full(a):
        return pl.BlockSpec(a.shape, lambda b, i: (0,) * a.ndim)

    in_specs = [
        pl.BlockSpec((1, tm, D_MODEL), lambda b, i: (b, i, 0)),
        _mod_spec(sh), _mod_spec(sc), full(ng),
        full(wlat), full(wg), full(qng), full(kvng), full(qup), full(kup), full(vupt),
    ]
    args = [x, sh.table, sc.table, ng, wlat, wg, qng, kvng, qup, kup, vupt]
    if rope:
        in_specs += [pl.BlockSpec((tm, 2 * QK_ROPE), lambda b, i: (i, 0)),
                     pl.BlockSpec((2 * QK_ROPE, tm), lambda b, i: (0, i))]
        f = QK_ROPE // 4
        sign = jnp.concatenate([jnp.ones((QK_ROPE,), F32)] + [-jnp.ones((f,), F32), jnp.ones((f,), F32)] * 2)
        args += [cs, (cs * sign).T]
    out_specs = [pl.BlockSpec((1, MLA_HEADS, tm // tq, QK_PAD, tq), lambda b, i: (b, 0, i, 0, 0)),
                 pl.BlockSpec((1, MLA_HEADS, tm, QK_PAD), lambda b, i: (b, 0, i, 0)),
                 pl.BlockSpec((1, MLA_HEADS, tm // MXU_DIM, V_HEAD, MXU_DIM), lambda b, i: (b, 0, i, 0, 0)),
                 pl.BlockSpec((1, MLA_HEADS, tm, V_HEAD), lambda b, i: (b, 0, i, 0))]
    out_shape = [jax.ShapeDtypeStruct((B, MLA_HEADS, T // tq, QK_PAD, tq), BF16),
                 jax.ShapeDtypeStruct((B, MLA_HEADS, T, QK_PAD), BF16),
                 jax.ShapeDtypeStruct((B, MLA_HEADS, T // MXU_DIM, V_HEAD, MXU_DIM), BF16),
                 jax.ShapeDtypeStruct((B, MLA_HEADS, T, V_HEAD), BF16)]
    if emit_cache:
        out_specs += [pl.BlockSpec((1, tm, KV_LORA), lambda b, i: (b, i, 0)),
                      pl.BlockSpec((1, tm, QK_ROPE), lambda b, i: (b, i, 0))]
        out_shape += [jax.ShapeDtypeStruct((B, T, KV_LORA), F32),
                      jax.ShapeDtypeStruct((B, T, QK_ROPE), F32)]
    return pl.pallas_call(
        functools.partial(_mla_in_kernel, rope=rope, emit_cache=emit_cache, tq=tq),
        grid=(B, T // tm),
        in_specs=in_specs, out_specs=out_specs, out_shape=out_shape,
        compiler_params=_params("parallel", "arbitrary"),
        name="mla_in",
    )(*args)


def _ctx_expand_kernel(ckv_ref, kr_ref, kup_ref, vupt_ref, k_ref, v_ref):
    ckv_b = ckv_ref[0].astype(BF16)
    kn = _dot(ckv_b, kup_ref[...])
    vt = _dot_nt(vupt_ref[...], ckv_b)
    kr = kr_ref[0].astype(BF16)
    zeros = jnp.zeros_like(kr)
    for hd in range(MLA_HEADS):
        k_ref[0, hd, :, 0:QK_NOPE] = kn[:, hd * QK_NOPE:(hd + 1) * QK_NOPE].astype(BF16)
        k_ref[0, hd, :, QK_NOPE:QK_NOPE + QK_ROPE] = kr
        k_ref[0, hd, :, QK_NOPE + QK_ROPE:QK_PAD] = zeros
        _store_vt_tiles(v_ref, hd, vt)


def _ctx_expand_call(ckv, kr, kup, vupt):
    B, L, _ = ckv.shape
    return pl.pallas_call(
        _ctx_expand_kernel,
        grid=(B,),
        in_specs=[
            pl.BlockSpec((1, L, KV_LORA), lambda b: (b, 0, 0)),
            pl.BlockSpec((1, L, QK_ROPE), lambda b: (b, 0, 0)),
            pl.BlockSpec(kup.shape, lambda b: (0, 0)),
            pl.BlockSpec(vupt.shape, lambda b: (0, 0)),
        ],
        out_specs=[pl.BlockSpec((1, MLA_HEADS, L, QK_PAD), lambda b: (b, 0, 0, 0)),
                   pl.BlockSpec((1, MLA_HEADS, L // MXU_DIM, V_HEAD, MXU_DIM), lambda b: (b, 0, 0, 0, 0))],
        out_shape=[jax.ShapeDtypeStruct((B, MLA_HEADS, L, QK_PAD), BF16),
                   jax.ShapeDtypeStruct((B, MLA_HEADS, L // MXU_DIM, V_HEAD, MXU_DIM), BF16)],
        compiler_params=_params("parallel"),
        name="ctx_expand",
    )(ckv, kr, kup, vupt)


ATTN_UNROLL = 4
ATTN_LEAD = 4
ATTN_HEADS_PER_STEP = 2


def _attn_ctx_kernel(q_ref, kc_ref, kl_ref, vc_ref, vl_ref, sg_ref, o_ref, s0_scr, s1_scr):
    hp, nq, _, tq = q_ref.shape[1:]
    T = nq * tq
    Lc = kc_ref.shape[2]
    tiles = ([(kc_ref, vc_ref, j * MXU_DIM) for j in range(Lc // MXU_DIM)]
             + [(kl_ref, vl_ref, j * MXU_DIM) for j in range(T // MXU_DIM)])
    groups = MXU_DIM // F32_SUBLANES

    def score_tile(hd, qb, t, s_dst):
        kref, _, off = tiles[t]
        s = _dot(kref[0, hd, off:off + MXU_DIM, :], qb)
        s_dst[t * MXU_DIM:(t + 1) * MXU_DIM, :] = s

    def tile_max(s_buf, t):
        s = s_buf[t * MXU_DIM:(t + 1) * MXU_DIM, :]
        return jnp.max(s.reshape(groups, F32_SUBLANES, tq), axis=0)

    def step(blk, blk_next, m_cur, s_cur, s_nxt):
        scoring = blk_next is not None
        hd, qi = blk // nq, blk % nq
        row = pl.multiple_of(qi * tq, tq)
        if scoring:
            hd_next = blk_next // nq
            q_next = q_ref[0, hd_next, blk_next % nq]
            for t in range(ATTN_LEAD):
                score_tile(hd_next, q_next, t, s_nxt)
        acc = lacc = None
        parts = []
        for t, (_, vref, off) in enumerate(tiles):
            p = jnp.exp2(s_cur[t * MXU_DIM:(t + 1) * MXU_DIM, :] - m_cur)
            ps = jnp.sum(p.reshape(groups, F32_SUBLANES, tq), axis=0)
            lacc = ps if lacc is None else lacc + ps
            pv = _dot(vref[0, hd, off // MXU_DIM], p.astype(BF16))
            acc = pv if acc is None else acc + pv
            if scoring:
                if t + ATTN_LEAD < len(tiles):
                    score_tile(hd_next, q_next, t + ATTN_LEAD, s_nxt)
                parts.append(tile_max(s_nxt, t))
        l = jnp.sum(lacc, axis=0, keepdims=True)
        gate = sg_ref[0, hd, pl.ds(row, tq), :].astype(F32)
        o_ref[0, hd, pl.ds(row, tq), :] = ((acc * (1.0 / l)).T * gate).astype(BF16)
        if not scoring:
            return None
        return jnp.max(functools.reduce(jnp.maximum, parts), axis=0, keepdims=True)

    q0 = q_ref[0, 0, 0]
    for t in range(len(tiles)):
        score_tile(0, q0, t, s0_scr)
    mrun = functools.reduce(jnp.maximum, [tile_max(s0_scr, t) for t in range(len(tiles))])
    m0 = jnp.max(mrun, axis=0, keepdims=True)

    def trip(i, m, last):
        bufs = (s0_scr, s1_scr)
        for u in range(ATTN_UNROLL):
            blk = i * ATTN_UNROLL + u
            blk_next = None if (last and u == ATTN_UNROLL - 1) else blk + 1
            m = step(blk, blk_next, m, bufs[u % 2], bufs[(u + 1) % 2])
        return m

    trips = hp * nq // ATTN_UNROLL
    m = lax.fori_loop(0, trips - 1, lambda i, m: trip(i, m, False), m0)
    trip(trips - 1, m, True)


def _attn_ctx_call(q, kc, kl, vc, vl, sg):
    B, H, nq, _, tq = q.shape
    T = nq * tq
    Lc = kc.shape[2]
    hp = ATTN_HEADS_PER_STEP
    assert nq % ATTN_UNROLL == 0 and ATTN_UNROLL % 2 == 0 and H % hp == 0

    def heads(*tail):
        return pl.BlockSpec((1, hp) + tail, lambda b, h: (b, h) + (0,) * len(tail))

    return pl.pallas_call(
        _attn_ctx_kernel,
        scratch_shapes=[pltpu.VMEM((Lc + T, tq), F32), pltpu.VMEM((Lc + T, tq), F32)],
        grid=(B, H // hp),
        in_specs=[heads(nq, QK_PAD, tq), heads(Lc, QK_PAD), heads(T, QK_PAD),
                  heads(Lc // MXU_DIM, V_HEAD, MXU_DIM), heads(T // MXU_DIM, V_HEAD, MXU_DIM),
                  heads(T, V_HEAD)],
        out_specs=heads(T, V_HEAD),
        out_shape=jax.ShapeDtypeStruct((B, H, T, V_HEAD), BF16),
        compiler_params=_params("parallel", "arbitrary"),
        name="attn_ctx",
    )(q, kc, kl, vc, vl, sg)


def _attn_self_kernel(q_ref, k_ref, v_ref, sg_ref, o_ref):
    ns, seq = q_ref.shape[2], q_ref.shape[4]
    items = [(sq, hd) for sq in range(ns) for hd in range(MLA_HEADS)]
    rows = [slice(sq * seq, (sq + 1) * seq) for sq in range(ns)]
    s = [_dot(k_ref[0, hd, rows[sq], :], q_ref[0, hd, sq]) for sq, hd in items]
    p = [jnp.exp2(a - jnp.max(a, axis=0, keepdims=True)) for a in s]
    inv = [1.0 / jnp.sum(a, axis=0, keepdims=True) for a in p]
    o = [_dot(v_ref[0, hd, sq], p[i].astype(BF16)) * inv[i] for i, (sq, hd) in enumerate(items)]
    for i, (sq, hd) in enumerate(items):
        o_ref[0, hd, rows[sq], :] = (o[i].T * sg_ref[0, hd, rows[sq], :].astype(F32)).astype(BF16)


SELF_SEQS_PER_STEP = 2


def _attn_self_call(q, k, v, sg):
    _, H, nseq, _, seq = q.shape
    N = nseq * seq
    ns = SELF_SEQS_PER_STEP
    assert seq == MXU_DIM and nseq % ns == 0
    return pl.pallas_call(
        _attn_self_kernel,
        grid=(nseq // ns,),
        in_specs=[
            pl.BlockSpec((1, H, ns, QK_PAD, seq), lambda b: (0, 0, b, 0, 0)),
            pl.BlockSpec((1, H, ns * seq, QK_PAD), lambda b: (0, 0, b, 0)),
            pl.BlockSpec((1, H, ns, V_HEAD, seq), lambda b: (0, 0, b, 0, 0)),
            pl.BlockSpec((1, H, ns * seq, V_HEAD), lambda b: (0, 0, b, 0)),
        ],
        out_specs=pl.BlockSpec((1, H, ns * seq, V_HEAD), lambda b: (0, 0, b, 0)),
        out_shape=jax.ShapeDtypeStruct((1, H, N, V_HEAD), BF16),
        compiler_params=_params("parallel"),
        name="attn_self",
    )(q, k, v, sg)


def _mla_out_kernel(o_ref, x_ref, g_ref, wo_ref, fg_ref, y_ref):
    o = jnp.concatenate([o_ref[0, hd] for hd in range(MLA_HEADS)], axis=1)
    x2 = x_ref[0] + g_ref[0] * _dot(o, wo_ref[...])
    y_ref[0] = _rms_rows(x2, fg_ref[...])


def _mla_out_call(o, x, g, w_out, fg, tm):
    B, T, _ = x.shape
    tok = lambda: pl.BlockSpec((1, tm, D_MODEL), lambda b, i: (b, i, 0))
    return pl.pallas_call(
        _mla_out_kernel,
        grid=(B, T // tm),
        in_specs=[pl.BlockSpec((1, MLA_HEADS, tm, V_HEAD), lambda b, i: (b, 0, i, 0)), tok(), _mod_spec(g),
                  pl.BlockSpec(w_out.shape, lambda b, i: (0, 0)),
                  pl.BlockSpec((1, D_MODEL), lambda b, i: (0, 0))],
        out_specs=tok(),
        out_shape=jax.ShapeDtypeStruct((B, T, D_MODEL), F32),
        compiler_params=_params("parallel", "arbitrary"),
        name="mla_out",
    )(o, x, g.table, w_out, fg)


def _swap_halves(w):
    f = QK_ROPE // 4
    return jnp.concatenate([-w[..., f:2 * f], w[..., 0:f], -w[..., 3 * f:4 * f], w[..., 2 * f:3 * f]], axis=-1)


def _rope_table(T):
    f = QK_ROPE // 4
    rows = T // GRID_W
    inv = ROPE_BASE ** (-jnp.arange(f, dtype=F32) / f)
    ar = jnp.arange(rows, dtype=F32)[:, None] * inv
    ac = jnp.arange(GRID_W, dtype=F32)[:, None] * inv

    def by_row(a):
        return jnp.broadcast_to(a[:, None, :], (rows, GRID_W, f)).reshape(T, f)

    def by_col(a):
        return jnp.broadcast_to(a[None, :, :], (rows, GRID_W, f)).reshape(T, f)

    cr, sr, cc, sc = by_row(jnp.cos(ar)), by_row(jnp.sin(ar)), by_col(jnp.cos(ac)), by_col(jnp.sin(ac))
    return jnp.concatenate([cr, cr, cc, cc, sr, sr, sc, sc], axis=-1)


def kernel(x_prompt, x_sample, c, state_ret_fwd, state_ret_bwd, cache_mla_ckv, cache_mla_krope, c_ctx,
           ada_w, ada_b, norm_g, even_in_w, even_conv_w, even_out_w, odd_in_w, odd_q_norm_g,
           odd_kv_norm_g, odd_q_up_w, odd_kv_up_w, odd_out_w, final_norm_g):
    BP, SEQ, D = x_prompt.shape
    BS, TS, _ = x_sample.shape
    NP = BP * SEQ

    cvec = jnp.concatenate([c, c_ctx[None, :], jnp.zeros((ADA_ROWS - BS - 1, D), F32)], axis=0)
    mod = _ada_call(cvec, ada_w, ada_b).reshape(-1, ADA_ROWS, 3, 1, D)

    def mods(l):
        return ([Mod(mod, l, i, BS) for i in range(3)], [Mod(mod, l, i, None) for i in range(3)])

    (sh_p, sc_p, g_p), (sh_s, sc_s, g_s) = mods(0)
    ng = norm_g[0][None, :]
    w_in = even_in_w[0].astype(BF16)
    w_out = even_out_w[0].astype(BF16)
    conv_w = even_conv_w[0]
    mask, dec, cdec = _retention_tables()

    xp1, nsf, nsb = _even_layer_call(x_prompt, sh_p, sc_p, g_p, ng, w_in, conv_w, w_out,
                                     mask, dec, cdec, None, None, SEQ)
    xs1, _, _ = _even_layer_call(x_sample, sh_s, sc_s, g_s, ng, w_in, conv_w, w_out, mask, dec, cdec,
                                 state_ret_fwd[:, 0], state_ret_bwd[:, 0], TOKEN_BLOCK)

    (sh_p, sc_p, g_p), (sh_s, sc_s, g_s) = mods(1)
    ng = norm_g[1][None, :]
    w_in = odd_in_w[0]
    nq = Q_LORA + KV_LORA
    wkr = w_in[:, nq:nq + QK_ROPE]
    wlat = jnp.concatenate([w_in[:, :nq + QK_ROPE], _swap_halves(wkr)], axis=-1).astype(BF16)
    wg = w_in[:, nq + QK_ROPE:].astype(BF16)
    qup = odd_q_up_w[0].T.astype(BF16)
    kvup = odd_kv_up_w[0].reshape(KV_LORA, MLA_HEADS, QK_NOPE + V_HEAD)
    kup = kvup[..., :QK_NOPE].reshape(KV_LORA, MLA_HEADS * QK_NOPE).astype(BF16)
    vupt = kvup[..., QK_NOPE:].reshape(KV_LORA, MLA_HEADS * V_HEAD).T.astype(BF16)
    wts = (wlat, wg, odd_q_norm_g[0][None, :], odd_kv_norm_g[0][None, :], qup, kup, vupt)
    w_out = odd_out_w[0].astype(BF16)
    fg = final_norm_g[None, :]

    xp1f = xp1.reshape(1, NP, D)
    q_p, k_p, v_p, sg_p, ckv_p, kr_p = _mla_in_call(xp1f, sh_p, sc_p, ng, wts, None, TOKEN_BLOCK, SEQ, True)
    q_s, k_s, v_s, sg_s = _mla_in_call(xs1, sh_s, sc_s, ng, wts, _rope_table(TS), TOKEN_BLOCK, MXU_DIM, False)
    k_c, v_c = _ctx_expand_call(cache_mla_ckv[:, 0], cache_mla_krope[:, 0], kup, vupt)

    o_p = _attn_self_call(q_p, k_p, v_p, sg_p)
    o_s = _attn_ctx_call(q_s, k_c, k_s, v_c, v_s, sg_s)

    y_p = _mla_out_call(o_p, xp1f, g_p, w_out, fg, OUT_BLOCK)
    y_s = _mla_out_call(o_s, xs1, g_s, w_out, fg, OUT_BLOCK)

    return (y_p.reshape(BP, SEQ, D), y_s,
            nsf[:, None], nsb[:, None],
            ckv_p.reshape(BP, 1, SEQ, KV_LORA), kr_p.reshape(BP, 1, SEQ, QK_ROPE))
```

```python
import functools
from typing import NamedTuple, Optional

import jax
import jax.numpy as jnp
from jax import lax
from jax.experimental import pallas as pl
from jax.experimental.pallas import tpu as pltpu

F32 = jnp.float32
BF16 = jnp.bfloat16

D_MODEL = 1024
GRID_W = 64
EPS = 1e-6
RET_HEADS = 4
RET_DK = 128
RET_DV = 128
RET_WIDTH = RET_HEADS * RET_DV
RET_CHUNK = 128
RET_BWD_OFFSET = 0.5
CONV_WIDTH = D_MODEL - RET_WIDTH
MLA_HEADS = 8
QK_NOPE = 128
QK_ROPE = 64
V_HEAD = 128
Q_LORA = 384
KV_LORA = 256
ROPE_BASE = 10000.0
QK_PAD = 256
ADA_ROWS = 16
BF16_SUBLANES = 16
F32_SUBLANES = 8
MXU_DIM = 256
LOG2_E = 1.4426950408889634
VMEM_LIMIT = 56 * 1024 * 1024
TOKEN_BLOCK = 512
OUT_BLOCK = 2048


def _silu(x):
    return x * (1.0 / (1.0 + jnp.exp(-x)))


def _rms_rows(x, g):
    return x * lax.rsqrt(jnp.mean(x * x, axis=-1, keepdims=True) + EPS) * g


def _dot(a, b):
    return jnp.dot(a, b, preferred_element_type=F32)


def _dot_nt(a, b):
    return lax.dot_general(a, b, (((1,), (1,)), ((), ())), preferred_element_type=F32)


def _dot_tn(a, b):
    return lax.dot_general(a, b, (((0,), (0,)), ((), ())), preferred_element_type=F32)


def _params(*sem):
    return pltpu.CompilerParams(dimension_semantics=sem, vmem_limit_bytes=VMEM_LIMIT)


def _ada_kernel(c_ref, w_ref, b_ref, o_ref):
    s = _silu(c_ref[...]).astype(BF16)
    o_ref[0] = _dot(s, w_ref[0].astype(BF16)) + b_ref[0]


def _ada_call(cvec, ada_w, ada_b):
    depth = ada_w.shape[0]
    return pl.pallas_call(
        _ada_kernel,
        grid=(depth, 3),
        in_specs=[
            pl.BlockSpec((ADA_ROWS, D_MODEL), lambda l, j: (0, 0)),
            pl.BlockSpec((1, D_MODEL, D_MODEL), lambda l, j: (l, 0, j)),
            pl.BlockSpec((1, 1, D_MODEL), lambda l, j: (l, 0, j)),
        ],
        out_specs=pl.BlockSpec((1, ADA_ROWS, D_MODEL), lambda l, j: (l, 0, j)),
        out_shape=jax.ShapeDtypeStruct((depth, ADA_ROWS, 3 * D_MODEL), F32),
        compiler_params=_params("arbitrary", "arbitrary"),
        name="ada",
    )(cvec, ada_w, ada_b.reshape(depth, 1, 3 * D_MODEL))


class Mod(NamedTuple):
    table: jax.Array
    layer: int
    part: int
    row: Optional[int]


def _mod_spec(m):
    row = (lambda b: b) if m.row is None else (lambda b: m.row)
    return pl.BlockSpec((None, None, 1, 1, D_MODEL), lambda b, *_: (m.layer, row(b), m.part, 0, 0))


def _even_layer_kernel(x_ref, sh_ref, sc_ref, g_ref, ng_ref, w_ref, cw_ref, wo_ref, mask_ref, dec_ref,
                       cdec_ref, s0f_ref, s0b_ref,
                       xo_ref, nsf_ref, nsb_ref,
                       k_all, v_all, z_all, sb_all, sf_run, sb_run, ymix,
                       *, nc, nb, zero_init):
    p = pl.program_id(1)
    j = pl.program_id(2)
    C = RET_CHUNK
    H = RET_HEADS
    dv = RET_DV
    W = RET_WIDTH
    tb = nc * C
    T = nb * tb

    def tile(cc, h):
        return slice(cc * C, (cc + 1) * C), slice(h * dv, (h + 1) * dv)

    def project():
        h = _rms_rows(x_ref[0], ng_ref[...]) * (1.0 + sc_ref[0]) + sh_ref[0]
        hb = h.astype(BF16)
        return lambda g: _dot(hb, w_ref[:, g * W:(g + 1) * W])

    @pl.when((p == 0) & (j == 0))
    def _():
        if zero_init:
            sf_run[...] = jnp.zeros_like(sf_run)
            sb_run[...] = jnp.zeros_like(sb_run)
        else:
            sf_run[...] = s0f_ref[0]
            sb_run[...] = s0b_ref[0]

    @pl.when(p == 0)
    def _():
        blk = nb - 1 - j
        r0 = pl.multiple_of(blk * tb, tb)
        proj = project()
        k = (proj(1) * (RET_DK ** -0.5)).astype(BF16)
        v = proj(2).astype(BF16)
        k_all[pl.ds(r0, tb), :] = k
        v_all[pl.ds(r0, tb), :] = v
        z_all[pl.ds(r0, tb), :] = (proj(5) * proj(6)).astype(BF16)
        for cc in reversed(range(nc)):
            gc = blk * nc + cc
            for h in range(H):
                rows, cols = tile(cc, h)
                sb_all[gc, h] = sb_run[h].astype(BF16)
                kd = (k[rows, cols].astype(F32) * dec_ref[3, h]).astype(BF16)
                sb_run[h] = sb_run[h] * cdec_ref[1, h] + _dot_tn(kd, v[rows, cols])

        @pl.when(j == nb - 1)
        def _():
            nsb_ref[0] = sb_run[...]

    @pl.when(p == 1)
    def _():
        r0 = pl.multiple_of(j * tb, tb)
        proj = project()
        q = proj(0).astype(BF16)
        sga = _silu(proj(3))
        k = k_all[pl.ds(r0, tb), :]
        v = v_all[pl.ds(r0, tb), :]
        items = [(cc, h) for cc in range(nc) for h in range(H)]

        def scores(cc, h):
            rows, cols = tile(cc, h)
            return _dot_nt(q[rows, cols], k[rows, cols])

        s_next = scores(*items[0])
        for i, (cc, h) in enumerate(items):
            s = s_next
            if i + 1 < len(items):
                s_next = scores(*items[i + 1])
            gc = j * nc + cc
            rows, cols = tile(cc, h)
            qh, kh, vh = q[rows, cols], k[rows, cols], v[rows, cols]
            inter = (_dot(qh, sf_run[h].astype(BF16)) * dec_ref[0, h]
                     + _dot(qh, sb_all[gc, h]) * dec_ref[1, h])
            kd = (kh.astype(F32) * dec_ref[2, h]).astype(BF16)
            sf_run[h] = sf_run[h] * cdec_ref[0, h] + _dot_tn(kd, vh)
            att = (s * mask_ref[h]).astype(BF16)
            o = _dot(att, vh) + inter
            on = o * lax.rsqrt(jnp.mean(o * o, axis=-1, keepdims=True) + EPS)
            ymix[rows, cols] = (on * sga[rows, cols]).astype(BF16)

        u = _silu(proj(7)) * proj(4)
        z = z_all[pl.ds(r0, tb), :].astype(F32)
        row = lax.broadcasted_iota(jnp.int32, z.shape, 0)
        n = BF16_SUBLANES
        r_prev = pl.multiple_of(jnp.maximum(r0 - n, 0), n)
        r_next = pl.multiple_of(jnp.minimum(r0 + tb, T - n), n)
        prev_row = z_all[pl.ds(r_prev, n), :][n - 1:n, :].astype(F32)
        next_row = z_all[pl.ds(r_next, n), :][0:1, :].astype(F32)
        prev_row = jnp.where(j > 0, prev_row, 0.0)
        next_row = jnp.where(j < nb - 1, next_row, 0.0)
        z_prev = jnp.where(row == 0, prev_row, pltpu.roll(z, 1, 0))
        z_next = jnp.where(row == tb - 1, next_row, pltpu.roll(z, tb - 1, 0))
        zc = z_prev * cw_ref[0:1, :] + z * cw_ref[1:2, :] + z_next * cw_ref[2:3, :]
        ymix[:, W:] = (u * zc).astype(BF16)

        y = _dot(ymix[...], wo_ref[...])
        xo_ref[0] = x_ref[0] + g_ref[0] * y

        @pl.when(j == nb - 1)
        def _():
            nsf_ref[0] = sf_run[...]


def _even_layer_call(x, sh, sc, g, ng, w_in, conv_w, w_out, mask, dec, cdec, s0f, s0b, tb):
    B, T, _ = x.shape
    nb = T // tb
    nc = tb // RET_CHUNK
    zero_init = s0f is None
    state_block = (1, RET_HEADS, RET_DK, RET_DV)
    if zero_init:
        s0f = s0b = jnp.zeros(state_block, F32)
        s0_spec = pl.BlockSpec(state_block, lambda b, p, j: (0, 0, 0, 0))
    else:
        s0_spec = pl.BlockSpec(state_block, lambda b, p, j: (b, 0, 0, 0))

    def const(a):
        return pl.BlockSpec(a.shape, lambda b, p, j: (0,) * a.ndim, pipeline_mode=pl.Buffered(1))

    kernel = functools.partial(_even_layer_kernel, nc=nc, nb=nb, zero_init=zero_init)
    state_shape = jax.ShapeDtypeStruct((B,) + state_block[1:], F32)
    state_spec = pl.BlockSpec(state_block, lambda b, p, j: (b, 0, 0, 0))
    seq_buf = pltpu.VMEM((T, RET_WIDTH), BF16)
    return pl.pallas_call(
        kernel,
        grid=(B, 2, nb),
        in_specs=[
            pl.BlockSpec((1, tb, D_MODEL), lambda b, p, j: (b, p * j + (1 - p) * (nb - 1 - j), 0)),
            _mod_spec(sh), _mod_spec(sc), _mod_spec(g),
            const(ng), const(w_in), const(conv_w), const(w_out), const(mask), const(dec),
            pl.BlockSpec(memory_space=pltpu.SMEM),
            s0_spec, s0_spec,
        ],
        out_specs=[
            pl.BlockSpec((1, tb, D_MODEL), lambda b, p, j: (b, p * j, 0)),
            state_spec, state_spec,
        ],
        out_shape=[jax.ShapeDtypeStruct((B, T, D_MODEL), F32), state_shape, state_shape],
        scratch_shapes=[
            seq_buf, seq_buf, seq_buf,
            pltpu.VMEM((T // RET_CHUNK, RET_HEADS, RET_DK, RET_DV), BF16),
            pltpu.VMEM((RET_HEADS, RET_DK, RET_DV), F32),
            pltpu.VMEM((RET_HEADS, RET_DK, RET_DV), F32),
            pltpu.VMEM((tb, D_MODEL), BF16),
        ],
        compiler_params=_params("parallel", "arbitrary", "arbitrary"),
        name="even_layer",
    )(x, sh.table, sc.table, g.table, ng, w_in, conv_w, w_out, mask, dec, cdec, s0f, s0b)


def _retention_tables():
    C = RET_CHUNK
    hh = jnp.arange(RET_HEADS, dtype=F32)
    lg_f = jnp.log(1.0 - 2.0 ** (-5.0 - hh))
    lg_b = jnp.log(1.0 - 2.0 ** (-5.0 - hh - RET_BWD_OFFSET))
    idx = jnp.arange(C, dtype=F32)
    diff = idx[:, None] - idx[None, :]
    m_f = jnp.where(diff >= 0, jnp.exp(lg_f[:, None, None] * jnp.maximum(diff, 0.0)), 0.0)
    m_b = jnp.where(diff <= 0, jnp.exp(lg_b[:, None, None] * jnp.maximum(-diff, 0.0)), 0.0)
    mask = m_f + m_b
    q_f = jnp.exp(lg_f[:, None] * (idx[None, :] + 1.0))
    q_b = jnp.exp(lg_b[:, None] * (C - idx[None, :]))
    k_f = jnp.exp(lg_f[:, None] * (C - 1.0 - idx[None, :]))
    k_b = jnp.exp(lg_b[:, None] * idx[None, :])
    dec = jnp.stack([q_f, q_b, k_f, k_b])
    dec = jnp.broadcast_to(dec[..., None], dec.shape + (RET_DV,))
    cdec = jnp.stack([jnp.exp(lg_f * C), jnp.exp(lg_b * C)])
    return mask, dec, cdec


def _store_vt_tiles(v_ref, hd, vt):
    v = vt[hd * V_HEAD:(hd + 1) * V_HEAD, :].astype(BF16)
    for j in range(v.shape[1] // MXU_DIM):
        v_ref[0, hd, j] = v[:, j * MXU_DIM:(j + 1) * MXU_DIM]


def _rope_mix(r, cs_ref, first):
    if cs_ref is not None:
        t = r * cs_ref[...]
        r = t + pltpu.roll(t, QK_ROPE, 1)
    return jnp.where(first, r, 0.0)


def _mla_in_kernel(*refs, rope, emit_cache, tq):
    (x_ref, sh_ref, sc_ref, ng_ref, wlat_ref, wg_ref, qng_ref, kvng_ref, qupt_ref,
     kup_ref, vupt_ref) = refs[:11]
    pos = 11
    cs_ref = cst_ref = None
    if rope:
        cs_ref, cst_ref = refs[pos:pos + 2]
        pos += 2
    q_ref, k_ref, v_ref, sg_ref = refs[pos:pos + 4]
    pos += 4
    if emit_cache:
        ckv_ref, kr_ref = refs[pos:pos + 2]

    h = _rms_rows(x_ref[0], ng_ref[...]) * (1.0 + sc_ref[0]) + sh_ref[0]
    hb = h.astype(BF16)
    lat = _dot(hb, wlat_ref[...])
    kr2 = lat[:, Q_LORA + KV_LORA:]
    sg = _silu(_dot(hb, wg_ref[...])).astype(BF16)
    for hd in range(MLA_HEADS):
        sg_ref[0, hd] = sg[:, hd * V_HEAD:(hd + 1) * V_HEAD]
    qn = _rms_rows(lat[:, :Q_LORA], qng_ref[...]).astype(BF16)
    ckv = _rms_rows(lat[:, Q_LORA:Q_LORA + KV_LORA], kvng_ref[...])
    qt = _dot_nt(qupt_ref[...], qn)
    ckv_b = ckv.astype(BF16)
    kn = _dot(ckv_b, kup_ref[...])
    vt = _dot_nt(vupt_ref[...], ckv_b)

    scale = (QK_NOPE + QK_ROPE) ** -0.5 * LOG2_E
    tm = kr2.shape[0]
    first = lax.broadcasted_iota(jnp.int32, kr2.shape, 1) < QK_ROPE
    k_rot = _rope_mix(kr2, cs_ref, first).astype(BF16)
    q_pad = jnp.zeros((QK_PAD - QK_NOPE - QK_ROPE, tq), BF16)
    f = QK_ROPE // 4
    for hd in range(MLA_HEADS):
        base = hd * (QK_NOPE + QK_ROPE)
        q_nope = (qt[base:base + QK_NOPE, :] * scale).astype(BF16)
        q_rot = qt[base + QK_NOPE:base + QK_NOPE + QK_ROPE, :]
        if rope:
            partner = jnp.concatenate([q_rot[f:2 * f], q_rot[0:f], q_rot[3 * f:4 * f], q_rot[2 * f:3 * f]], axis=0)
            q_rot = q_rot * cst_ref[0:QK_ROPE, :] + partner * cst_ref[QK_ROPE:2 * QK_ROPE, :]
        q_rot = (q_rot * scale).astype(BF16)
        for j in range(tm // tq):
            cols = slice(j * tq, (j + 1) * tq)
            q_ref[0, hd, j, 0:QK_NOPE, :] = q_nope[:, cols]
            q_ref[0, hd, j, QK_NOPE:QK_NOPE + QK_ROPE, :] = q_rot[:, cols]
            q_ref[0, hd, j, QK_NOPE + QK_ROPE:QK_PAD, :] = q_pad
        k_ref[0, hd, :, 0:QK_NOPE] = kn[:, hd * QK_NOPE:(hd + 1) * QK_NOPE].astype(BF16)
        k_ref[0, hd, :, QK_NOPE:QK_PAD] = k_rot
        _store_vt_tiles(v_ref, hd, vt)
    if emit_cache:
        ckv_ref[0] = ckv
        kr_ref[0] = kr2[:, :QK_ROPE]


def _mla_in_call(x, sh, sc, ng, wts, cs, tm, tq, emit_cache):
    B, T, _ = x.shape
    rope = cs is not None
    wlat, wg, qng, kvng, qup, kup, vupt = wts

    def full(a):
        return pl.BlockSpec(a.shape, lambda b, i: (0,) * a.ndim)

    in_specs = [
        pl.BlockSpec((1, tm, D_MODEL), lambda b, i: (b, i, 0)),
        _mod_spec(sh), _mod_spec(sc), full(ng),
        full(wlat), full(wg), full(qng), full(kvng), full(qup), full(kup), full(vupt),
    ]
    args = [x, sh.table, sc.table, ng, wlat, wg, qng, kvng, qup, kup, vupt]
    if rope:
        in_specs += [pl.BlockSpec((tm, 2 * QK_ROPE), lambda b, i: (i, 0)),
                     pl.BlockSpec((2 * QK_ROPE, tm), lambda b, i: (0, i))]
        f = QK_ROPE // 4
        sign = jnp.concatenate([jnp.ones((QK_ROPE,), F32)] + [-jnp.ones((f,), F32), jnp.ones((f,), F32)] * 2)
        args += [cs, (cs * sign).T]
    out_specs = [pl.BlockSpec((1, MLA_HEADS, tm // tq, QK_PAD, tq), lambda b, i: (b, 0, i, 0, 0)),
                 pl.BlockSpec((1, MLA_HEADS, tm, QK_PAD), lambda b, i: (b, 0, i, 0)),
                 pl.BlockSpec((1, MLA_HEADS, tm // MXU_DIM, V_HEAD, MXU_DIM), lambda b, i: (b, 0, i, 0, 0)),
                 pl.BlockSpec((1, MLA_HEADS, tm, V_HEAD), lambda b, i: (b, 0, i, 0))]
    out_shape = [jax.ShapeDtypeStruct((B, MLA_HEADS, T // tq, QK_PAD, tq), BF16),
                 jax.ShapeDtypeStruct((B, MLA_HEADS, T, QK_PAD), BF16),
                 jax.ShapeDtypeStruct((B, MLA_HEADS, T // MXU_DIM, V_HEAD, MXU_DIM), BF16),
                 jax.ShapeDtypeStruct((B, MLA_HEADS, T, V_HEAD), BF16)]
    if emit_cache:
        out_specs += [pl.BlockSpec((1, tm, KV_LORA), lambda b, i: (b, i, 0)),
                      pl.BlockSpec((1, tm, QK_ROPE), lambda b, i: (b, i, 0))]
        out_shape += [jax.ShapeDtypeStruct((B, T, KV_LORA), F32),
                      jax.ShapeDtypeStruct((B, T, QK_ROPE), F32)]
    return pl.pallas_call(
        functools.partial(_mla_in_kernel, rope=rope, emit_cache=emit_cache, tq=tq),
        grid=(B, T // tm),
        in_specs=in_specs, out_specs=out_specs, out_shape=out_shape,
        compiler_params=_params("parallel", "arbitrary"),
        name="mla_in",
    )(*args)


def _ctx_expand_kernel(ckv_ref, kr_ref, kup_ref, vupt_ref, k_ref, v_ref):
    ckv_b = ckv_ref[0].astype(BF16)
    kn = _dot(ckv_b, kup_ref[...])
    vt = _dot_nt(vupt_ref[...], ckv_b)
    kr = kr_ref[0].astype(BF16)
    zeros = jnp.zeros_like(kr)
    for hd in range(MLA_HEADS):
        k_ref[0, hd, :, 0:QK_NOPE] = kn[:, hd * QK_NOPE:(hd + 1) * QK_NOPE].astype(BF16)
        k_ref[0, hd, :, QK_NOPE:QK_NOPE + QK_ROPE] = kr
        k_ref[0, hd, :, QK_NOPE + QK_ROPE:QK_PAD] = zeros
        _store_vt_tiles(v_ref, hd, vt)


def _ctx_expand_call(ckv, kr, kup, vupt):
    B, L, _ = ckv.shape
    return pl.pallas_call(
        _ctx_expand_kernel,
        grid=(B,),
        in_specs=[
            pl.BlockSpec((1, L, KV_LORA), lambda b: (b, 0, 0)),
            pl.BlockSpec((1, L, QK_ROPE), lambda b: (b, 0, 0)),
            pl.BlockSpec(kup.shape, lambda b: (0, 0)),
            pl.BlockSpec(vupt.shape, lambda b: (0, 0)),
        ],
        out_specs=[pl.BlockSpec((1, MLA_HEADS, L, QK_PAD), lambda b: (b, 0, 0, 0)),
                   pl.BlockSpec((1, MLA_HEADS, L // MXU_DIM, V_HEAD, MXU_DIM), lambda b: (b, 0, 0, 0, 0))],
        out_shape=[jax.ShapeDtypeStruct((B, MLA_HEADS, L, QK_PAD), BF16),
                   jax.ShapeDtypeStruct((B, MLA_HEADS, L // MXU_DIM, V_HEAD, MXU_DIM), BF16)],
        compiler_params=_params("parallel"),
        name="ctx_expand",
    )(ckv, kr, kup, vupt)


ATTN_UNROLL = 4
ATTN_LEAD = 4
ATTN_HEADS_PER_STEP = 2


def _attn_ctx_kernel(q_ref, kc_ref, kl_ref, vc_ref, vl_ref, sg_ref, o_ref, s0_scr, s1_scr):
    hp, nq, _, tq = q_ref.shape[1:]
    T = nq * tq
    Lc = kc_ref.shape[2]
    tiles = ([(kc_ref, vc_ref, j * MXU_DIM) for j in range(Lc // MXU_DIM)]
             + [(kl_ref, vl_ref, j * MXU_DIM) for j in range(T // MXU_DIM)])
    groups = MXU_DIM // F32_SUBLANES

    def score_tile(hd, qb, t, s_dst):
        kref, _, off = tiles[t]
        s = _dot(kref[0, hd, off:off + MXU_DIM, :], qb)
        s_dst[t * MXU_DIM:(t + 1) * MXU_DIM, :] = s

    def tile_max(s_buf, t):
        s = s_buf[t * MXU_DIM:(t + 1) * MXU_DIM, :]
        return jnp.max(s.reshape(groups, F32_SUBLANES, tq), axis=0)

    def step(blk, blk_next, m_cur, s_cur, s_nxt):
        scoring = blk_next is not None
        hd, qi = blk // nq, blk % nq
        row = pl.multiple_of(qi * tq, tq)
        if scoring:
            hd_next = blk_next // nq
            q_next = q_ref[0, hd_next, blk_next % nq]
            for t in range(ATTN_LEAD):
                score_tile(hd_next, q_next, t, s_nxt)
        acc = lacc = None
        parts = []
        for t, (_, vref, off) in enumerate(tiles):
            p = jnp.exp2(s_cur[t * MXU_DIM:(t + 1) * MXU_DIM, :] - m_cur)
            ps = jnp.sum(p.reshape(groups, F32_SUBLANES, tq), axis=0)
            lacc = ps if lacc is None else lacc + ps
            pv = _dot(vref[0, hd, off // MXU_DIM], p.astype(BF16))
            acc = pv if acc is None else acc + pv
            if scoring:
                if t + ATTN_LEAD < len(tiles):
                    score_tile(hd_next, q_next, t + ATTN_LEAD, s_nxt)
                parts.append(tile_max(s_nxt, t))
        l = jnp.sum(lacc, axis=0, keepdims=True)
        gate = sg_ref[0, hd, pl.ds(row, tq), :].astype(F32)
        o_ref[0, hd, pl.ds(row, tq), :] = ((acc * (1.0 / l)).T * gate).astype(BF16)
        if not scoring:
            return None
        return jnp.max(functools.reduce(jnp.maximum, parts), axis=0, keepdims=True)

    q0 = q_ref[0, 0, 0]
    for t in range(len(tiles)):
        score_tile(0, q0, t, s0_scr)
    mrun = functools.reduce(jnp.maximum, [tile_max(s0_scr, t) for t in range(len(tiles))])
    m0 = jnp.max(mrun, axis=0, keepdims=True)

    def trip(i, m, last):
        bufs = (s0_scr, s1_scr)
        for u in range(ATTN_UNROLL):
            blk = i * ATTN_UNROLL + u
            blk_next = None if (last and u == ATTN_UNROLL - 1) else blk + 1
            m = step(blk, blk_next, m, bufs[u % 2], bufs[(u + 1) % 2])
        return m

    trips = hp * nq // ATTN_UNROLL
    m = lax.fori_loop(0, trips - 1, lambda i, m: trip(i, m, False), m0)
    trip(trips - 1, m, True)


def _attn_ctx_call(q, kc, kl, vc, vl, sg):
    B, H, nq, _, tq = q.shape
    T = nq * tq
    Lc = kc.shape[2]
    hp = ATTN_HEADS_PER_STEP
    assert nq % ATTN_UNROLL == 0 and ATTN_UNROLL % 2 == 0 and H % hp == 0

    def heads(*tail):
        return pl.BlockSpec((1, hp) + tail, lambda b, h: (b, h) + (0,) * len(tail))

    return pl.pallas_call(
        _attn_ctx_kernel,
        scratch_shapes=[pltpu.VMEM((Lc + T, tq), F32), pltpu.VMEM((Lc + T, tq), F32)],
        grid=(B, H // hp),
        in_specs=[heads(nq, QK_PAD, tq), heads(Lc, QK_PAD), heads(T, QK_PAD),
                  heads(Lc // MXU_DIM, V_HEAD, MXU_DIM), heads(T // MXU_DIM, V_HEAD, MXU_DIM),
                  heads(T, V_HEAD)],
        out_specs=heads(T, V_HEAD),
        out_shape=jax.ShapeDtypeStruct((B, H, T, V_HEAD), BF16),
        compiler_params=_params("parallel", "arbitrary"),
        name="attn_ctx",
    )(q, kc, kl, vc, vl, sg)


def _attn_self_kernel(q_ref, k_ref, v_ref, sg_ref, o_ref):
    ns, seq = q_ref.shape[2], q_ref.shape[4]
    items = [(sq, hd) for sq in range(ns) for hd in range(MLA_HEADS)]
    rows = [slice(sq * seq, (sq + 1) * seq) for sq in range(ns)]
    s = [_dot(k_ref[0, hd, rows[sq], :], q_ref[0, hd, sq]) for sq, hd in items]
    p = [jnp.exp2(a - jnp.max(a, axis=0, keepdims=True)) for a in s]
    inv = [1.0 / jnp.sum(a, axis=0, keepdims=True) for a in p]
    o = [_dot(v_ref[0, hd, sq], p[i].astype(BF16)) * inv[i] for i, (sq, hd) in enumerate(items)]
    for i, (sq, hd) in enumerate(items):
        o_ref[0, hd, rows[sq], :] = (o[i].T * sg_ref[0, hd, rows[sq], :].astype(F32)).astype(BF16)


SELF_SEQS_PER_STEP = 4


def _attn_self_call(q, k, v, sg):
    _, H, nseq, _, seq = q.shape
    N = nseq * seq
    ns = SELF_SEQS_PER_STEP
    assert seq == MXU_DIM and nseq % ns == 0
    return pl.pallas_call(
        _attn_self_kernel,
        grid=(nseq // ns,),
        in_specs=[
            pl.BlockSpec((1, H, ns, QK_PAD, seq), lambda b: (0, 0, b, 0, 0)),
            pl.BlockSpec((1, H, ns * seq, QK_PAD), lambda b: (0, 0, b, 0)),
            pl.BlockSpec((1, H, ns, V_HEAD, seq), lambda b: (0, 0, b, 0, 0)),
            pl.BlockSpec((1, H, ns * seq, V_HEAD), lambda b: (0, 0, b, 0)),
        ],
        out_specs=pl.BlockSpec((1, H, ns * seq, V_HEAD), lambda b: (0, 0, b, 0)),
        out_shape=jax.ShapeDtypeStruct((1, H, N, V_HEAD), BF16),
        compiler_params=_params("parallel"),
        name="attn_self",
    )(q, k, v, sg)


def _mla_out_kernel(o_ref, x_ref, g_ref, wo_ref, fg_ref, y_ref):
    o = jnp.concatenate([o_ref[0, hd] for hd in range(MLA_HEADS)], axis=1)
    x2 = x_ref[0] + g_ref[0] * _dot(o, wo_ref[...])
    y_ref[0] = _rms_rows(x2, fg_ref[...])


def _mla_out_call(o, x, g, w_out, fg, tm):
    B, T, _ = x.shape
    tok = lambda: pl.BlockSpec((1, tm, D_MODEL), lambda b, i: (b, i, 0))
    return pl.pallas_call(
        _mla_out_kernel,
        grid=(B, T // tm),
        in_specs=[pl.BlockSpec((1, MLA_HEADS, tm, V_HEAD), lambda b, i: (b, 0, i, 0)), tok(), _mod_spec(g),
                  pl.BlockSpec(w_out.shape, lambda b, i: (0, 0)),
                  pl.BlockSpec((1, D_MODEL), lambda b, i: (0, 0))],
        out_specs=tok(),
        out_shape=jax.ShapeDtypeStruct((B, T, D_MODEL), F32),
        compiler_params=_params("parallel", "arbitrary"),
        name="mla_out",
    )(o, x, g.table, w_out, fg)


def _swap_halves(w):
    f = QK_ROPE // 4
    return jnp.concatenate([-w[..., f:2 * f], w[..., 0:f], -w[..., 3 * f:4 * f], w[..., 2 * f:3 * f]], axis=-1)


def _rope_table(T):
    f = QK_ROPE // 4
    rows = T // GRID_W
    inv = ROPE_BASE ** (-jnp.arange(f, dtype=F32) / f)
    ar = jnp.arange(rows, dtype=F32)[:, None] * inv
    ac = jnp.arange(GRID_W, dtype=F32)[:, None] * inv

    def by_row(a):
        return jnp.broadcast_to(a[:, None, :], (rows, GRID_W, f)).reshape(T, f)

    def by_col(a):
        return jnp.broadcast_to(a[None, :, :], (rows, GRID_W, f)).reshape(T, f)

    cr, sr, cc, sc = by_row(jnp.cos(ar)), by_row(jnp.sin(ar)), by_col(jnp.cos(ac)), by_col(jnp.sin(ac))
    return jnp.concatenate([cr, cr, cc, cc, sr, sr, sc, sc], axis=-1)


def kernel(x_prompt, x_sample, c, state_ret_fwd, state_ret_bwd, cache_mla_ckv, cache_mla_krope, c_ctx,
           ada_w, ada_b, norm_g, even_in_w, even_conv_w, even_out_w, odd_in_w, odd_q_norm_g,
           odd_kv_norm_g, odd_q_up_w, odd_kv_up_w, odd_out_w, final_norm_g):
    BP, SEQ, D = x_prompt.shape
    BS, TS, _ = x_sample.shape
    NP = BP * SEQ

    cvec = jnp.concatenate([c, c_ctx[None, :], jnp.zeros((ADA_ROWS - BS - 1, D), F32)], axis=0)
    mod = _ada_call(cvec, ada_w, ada_b).reshape(-1, ADA_ROWS, 3, 1, D)

    def mods(l):
        return ([Mod(mod, l, i, BS) for i in range(3)], [Mod(mod, l, i, None) for i in range(3)])

    (sh_p, sc_p, g_p), (sh_s, sc_s, g_s) = mods(0)
    ng = norm_g[0][None, :]
    w_in = even_in_w[0].astype(BF16)
    w_out = even_out_w[0].astype(BF16)
    conv_w = even_conv_w[0]
    mask, dec, cdec = _retention_tables()

    xp1, nsf, nsb = _even_layer_call(x_prompt, sh_p, sc_p, g_p, ng, w_in, conv_w, w_out,
                                     mask, dec, cdec, None, None, SEQ)
    xs1, _, _ = _even_layer_call(x_sample, sh_s, sc_s, g_s, ng, w_in, conv_w, w_out, mask, dec, cdec,
                                 state_ret_fwd[:, 0], state_ret_bwd[:, 0], TOKEN_BLOCK)

    (sh_p, sc_p, g_p), (sh_s, sc_s, g_s) = mods(1)
    ng = norm_g[1][None, :]
    w_in = odd_in_w[0]
    nq = Q_LORA + KV_LORA
    wkr = w_in[:, nq:nq + QK_ROPE]
    wlat = jnp.concatenate([w_in[:, :nq + QK_ROPE], _swap_halves(wkr)], axis=-1).astype(BF16)
    wg = w_in[:, nq + QK_ROPE:].astype(BF16)
    qup = odd_q_up_w[0].T.astype(BF16)
    kvup = odd_kv_up_w[0].reshape(KV_LORA, MLA_HEADS, QK_NOPE + V_HEAD)
    kup = kvup[..., :QK_NOPE].reshape(KV_LORA, MLA_HEADS * QK_NOPE).astype(BF16)
    vupt = kvup[..., QK_NOPE:].reshape(KV_LORA, MLA_HEADS * V_HEAD).T.astype(BF16)
    wts = (wlat, wg, odd_q_norm_g[0][None, :], odd_kv_norm_g[0][None, :], qup, kup, vupt)
    w_out = odd_out_w[0].astype(BF16)
    fg = final_norm_g[None, :]

    xp1f = xp1.reshape(1, NP, D)
    q_p, k_p, v_p, sg_p, ckv_p, kr_p = _mla_in_call(xp1f, sh_p, sc_p, ng, wts, None, TOKEN_BLOCK, SEQ, True)
    q_s, k_s, v_s, sg_s = _mla_in_call(xs1, sh_s, sc_s, ng, wts, _rope_table(TS), TOKEN_BLOCK, MXU_DIM, False)
    k_c, v_c = _ctx_expand_call(cache_mla_ckv[:, 0], cache_mla_krope[:, 0], kup, vupt)

    o_p = _attn_self_call(q_p, k_p, v_p, sg_p)
    o_s = _attn_ctx_call(q_s, k_c, k_s, v_c, v_s, sg_s)

    y_p = _mla_out_call(o_p, xp1f, g_p, w_out, fg, OUT_BLOCK)
    y_s = _mla_out_call(o_s, xs1, g_s, w_out, fg, OUT_BLOCK)

    return (y_p.reshape(BP, SEQ, D), y_s,
            nsf[:, None], nsb[:, None],
            ckv_p.reshape(BP, 1, SEQ, KV_LORA), kr_p.reshape(BP, 1, SEQ, QK_ROPE))
```
